```python
import jax, jax.numpy as jnp
from jax import lax
import numpy as np

D_MODEL = 1024
BATCH = 8
SEQ = 8192
DEPTH = 2

N_MIXERS = 2
N_POOL_GROUPS = 4
POOL_WINDOWS = (2, 4, 8, 16)
POOL_GROUP_DIM = D_MODEL // N_POOL_GROUPS
CONV_WIDTH = 31
D_FF = 4 * D_MODEL
ALPHA = (2.0 * DEPTH) ** 0.25
BETA = (8.0 * DEPTH) ** -0.25
LN_EPS = 1e-5
N_POOL_LAYERS = (DEPTH + 1) // 2
N_CONV_LAYERS = DEPTH // 2

kernel_name = "hybrid_pool_conformer_sqrelu_deepnorm"


def layer_norm(x, g, b):
    xf = x.astype(jnp.float32)
    mu = jnp.mean(xf, axis=-1, keepdims=True)
    var = jnp.mean(jnp.square(xf - mu), axis=-1, keepdims=True)
    y = (xf - mu) * lax.rsqrt(var + LN_EPS)
    return (y * g.astype(jnp.float32) + b.astype(jnp.float32)).astype(x.dtype)


def pool_mixer(x, pool_w, pool_scale):
    B, S, _ = x.shape
    t = jnp.arange(S, dtype=jnp.float32)[None, :, None]
    outs = []
    for g, w in enumerate(POOL_WINDOWS):
        xg = x[..., g * POOL_GROUP_DIM:(g + 1) * POOL_GROUP_DIM]
        c = jnp.cumsum(xg.astype(jnp.float32), axis=1)
        c_pad = jnp.concatenate([jnp.zeros((B, 1, POOL_GROUP_DIM), jnp.float32), c], axis=1)
        hi = c_pad[:, 1:]
        lo = jnp.pad(c_pad[:, :S + 1 - w], ((0, 0), (w - 1, 0), (0, 0)))
        count = jnp.minimum(t + 1.0, float(w))
        d = ((hi - lo) / count).astype(x.dtype) - xg
        outs.append(jnp.einsum('bsc,cd->bsd', d, pool_w[g]))
    return jnp.concatenate(outs, axis=-1) * pool_scale


def conv_module(x, w_in, b_in, dw, dw_b, ln_g, ln_b, w_out, b_out):
    h = jnp.einsum('bsd,de->bse', x, w_in) + b_in
    a, gate = jnp.split(h, 2, axis=-1)
    h = a * jax.nn.sigmoid(gate)
    h = lax.conv_general_dilated(
        h, dw.reshape(CONV_WIDTH, 1, D_MODEL).astype(h.dtype),
        window_strides=(1,), padding=[(CONV_WIDTH - 1, 0)],
        dimension_numbers=('NWC', 'WIO', 'NWC'),
        feature_group_count=D_MODEL) + dw_b
    h = layer_norm(h, ln_g, ln_b)
    h = jax.nn.silu(h)
    return jnp.einsum('bsd,de->bse', h, w_out) + b_out


def sqrelu_mlp(x, w1, b1, w2, b2):
    h = jnp.einsum('bsd,df->bsf', x, w1) + b1
    h = jnp.square(jax.nn.relu(h))
    return jnp.einsum('bsf,fd->bsd', h, w2) + b2


def _fwd_setup_inputs(seed: int = 0) -> dict:
    key = jax.random.key(seed)
    ks = jax.random.split(key, 24)
    D, F, G, Dg, K = D_MODEL, D_FF, N_POOL_GROUPS, POOL_GROUP_DIM, CONV_WIDTH
    nrm = jax.random.normal
    P, C, L = N_POOL_LAYERS, N_CONV_LAYERS, DEPTH
    return {
        "x": nrm(ks[0], (BATCH, SEQ, D), jnp.float32),
        "pool_w": nrm(ks[1], (P, G, Dg, Dg), jnp.float32) * (Dg ** -0.5) * BETA,
        "pool_scale": 1.0 + 0.1 * nrm(ks[2], (P, D), jnp.float32),
        "conv_w_in": nrm(ks[3], (C, D, 2 * D), jnp.float32) * (D ** -0.5),
        "conv_b_in": 0.02 * nrm(ks[4], (C, 2 * D), jnp.float32),
        "conv_dw": nrm(ks[5], (C, K, D), jnp.float32) * (K ** -0.5),
        "conv_dw_b": 0.02 * nrm(ks[6], (C, D), jnp.float32),
        "conv_ln_g": 1.0 + 0.05 * nrm(ks[7], (C, D), jnp.float32),
        "conv_ln_b": 0.02 * nrm(ks[8], (C, D), jnp.float32),
        "conv_w_out": nrm(ks[9], (C, D, D), jnp.float32) * (D ** -0.5) * BETA,
        "conv_b_out": 0.02 * nrm(ks[10], (C, D), jnp.float32),
        "mix_ln_g": 1.0 + 0.05 * nrm(ks[11], (L, D), jnp.float32),
        "mix_ln_b": 0.02 * nrm(ks[12], (L, D), jnp.float32),
        "mlp_w1": nrm(ks[13], (L, D, F), jnp.float32) * (D ** -0.5) * BETA,
        "mlp_b1": 0.02 * nrm(ks[14], (L, F), jnp.float32),
        "mlp_w2": nrm(ks[15], (L, F, D), jnp.float32) * (F ** -0.5) * BETA,
        "mlp_b2": 0.02 * nrm(ks[16], (L, D), jnp.float32),
        "mlp_ln_g": 1.0 + 0.05 * nrm(ks[17], (L, D), jnp.float32),
        "mlp_ln_b": 0.02 * nrm(ks[18], (L, D), jnp.float32),
    }


def _fwd_reference(x, pool_w, pool_scale, conv_w_in, conv_b_in, conv_dw, conv_dw_b,
              conv_ln_g, conv_ln_b, conv_w_out, conv_b_out, mix_ln_g, mix_ln_b,
              mlp_w1, mlp_b1, mlp_w2, mlp_b2, mlp_ln_g, mlp_ln_b):
    for i in range(DEPTH):
        j = i // N_MIXERS
        if i % N_MIXERS == 0:
            mix = pool_mixer(x, pool_w[j], pool_scale[j])
        else:
            mix = conv_module(x, conv_w_in[j], conv_b_in[j], conv_dw[j], conv_dw_b[j],
                              conv_ln_g[j], conv_ln_b[j], conv_w_out[j], conv_b_out[j])
        x = layer_norm(ALPHA * x + mix, mix_ln_g[i], mix_ln_b[i])
        x = layer_norm(ALPHA * x + sqrelu_mlp(x, mlp_w1[i], mlp_b1[i], mlp_w2[i], mlp_b2[i]),
                       mlp_ln_g[i], mlp_ln_b[i])
    return x


import jax as _jax
import jax.numpy as _jnp

TWIN_FORMAT = 'train_step'
FWD_PARAMS = ['x', 'pool_w', 'pool_scale', 'conv_w_in', 'conv_b_in', 'conv_dw', 'conv_dw_b', 'conv_ln_g', 'conv_ln_b', 'conv_w_out', 'conv_b_out', 'mix_ln_g', 'mix_ln_b', 'mlp_w1', 'mlp_b1', 'mlp_w2', 'mlp_b2', 'mlp_ln_g', 'mlp_ln_b']
TWIN_WEIGHTS = ['pool_w', 'pool_scale', 'conv_w_in', 'conv_b_in', 'conv_dw', 'conv_dw_b', 'conv_ln_g', 'conv_ln_b', 'conv_w_out', 'conv_b_out', 'mix_ln_g', 'mix_ln_b', 'mlp_w1', 'mlp_b1', 'mlp_w2', 'mlp_b2', 'mlp_ln_g', 'mlp_ln_b']
TWIN_DIFF_INPUT = 'x'
TWIN_INPUTS = ['x', 'pool_w', 'pool_scale', 'conv_w_in', 'conv_b_in', 'conv_dw', 'conv_dw_b', 'conv_ln_g', 'conv_ln_b', 'conv_w_out', 'conv_b_out', 'mix_ln_g', 'mix_ln_b', 'mlp_w1', 'mlp_b1', 'mlp_w2', 'mlp_b2', 'mlp_ln_g', 'mlp_ln_b', 'loss_target', 'm_pool_w', 'm_pool_scale', 'm_conv_w_in', 'm_conv_b_in', 'm_conv_dw', 'm_conv_dw_b', 'm_conv_ln_g', 'm_conv_ln_b', 'm_conv_w_out', 'm_conv_b_out', 'm_mix_ln_g', 'm_mix_ln_b', 'm_mlp_w1', 'm_mlp_b1', 'm_mlp_w2', 'm_mlp_b2', 'm_mlp_ln_g', 'm_mlp_ln_b', 'v_pool_w', 'v_pool_scale', 'v_conv_w_in', 'v_conv_b_in', 'v_conv_dw', 'v_conv_dw_b', 'v_conv_ln_g', 'v_conv_ln_b', 'v_conv_w_out', 'v_conv_b_out', 'v_mix_ln_g', 'v_mix_ln_b', 'v_mlp_w1', 'v_mlp_b1', 'v_mlp_w2', 'v_mlp_b2', 'v_mlp_ln_g', 'v_mlp_ln_b']
TWIN_OUTPUTS = ['loss', 'grad_x', 'grad_pool_w', 'grad_pool_scale', 'grad_conv_w_in', 'grad_conv_b_in', 'grad_conv_dw', 'grad_conv_dw_b', 'grad_conv_ln_g', 'grad_conv_ln_b', 'grad_conv_w_out', 'grad_conv_b_out', 'grad_mix_ln_g', 'grad_mix_ln_b', 'grad_mlp_w1', 'grad_mlp_b1', 'grad_mlp_w2', 'grad_mlp_b2', 'grad_mlp_ln_g', 'grad_mlp_ln_b', 'delta_pool_w', 'delta_pool_scale', 'delta_conv_w_in', 'delta_conv_b_in', 'delta_conv_dw', 'delta_conv_dw_b', 'delta_conv_ln_g', 'delta_conv_ln_b', 'delta_conv_w_out', 'delta_conv_b_out', 'delta_mix_ln_g', 'delta_mix_ln_b', 'delta_mlp_w1', 'delta_mlp_b1', 'delta_mlp_w2', 'delta_mlp_b2', 'delta_mlp_ln_g', 'delta_mlp_ln_b', 'new_m_pool_w', 'new_m_pool_scale', 'new_m_conv_w_in', 'new_m_conv_b_in', 'new_m_conv_dw', 'new_m_conv_dw_b', 'new_m_conv_ln_g', 'new_m_conv_ln_b', 'new_m_conv_w_out', 'new_m_conv_b_out', 'new_m_mix_ln_g', 'new_m_mix_ln_b', 'new_m_mlp_w1', 'new_m_mlp_b1', 'new_m_mlp_w2', 'new_m_mlp_b2', 'new_m_mlp_ln_g', 'new_m_mlp_ln_b', 'new_v_pool_w', 'new_v_pool_scale', 'new_v_conv_w_in', 'new_v_conv_b_in', 'new_v_conv_dw', 'new_v_conv_dw_b', 'new_v_conv_ln_g', 'new_v_conv_ln_b', 'new_v_conv_w_out', 'new_v_conv_b_out', 'new_v_mix_ln_g', 'new_v_mix_ln_b', 'new_v_mlp_w1', 'new_v_mlp_b1', 'new_v_mlp_w2', 'new_v_mlp_b2', 'new_v_mlp_ln_g', 'new_v_mlp_ln_b']
TWIN_LEAF_KINDS = {'loss': 'loss', 'grad_x': 'grad_x', 'grad_pool_w': 'grad_w', 'grad_pool_scale': 'grad_w', 'grad_conv_w_in': 'grad_w', 'grad_conv_b_in': 'grad_w', 'grad_conv_dw': 'grad_w', 'grad_conv_dw_b': 'grad_w', 'grad_conv_ln_g': 'grad_w', 'grad_conv_ln_b': 'grad_w', 'grad_conv_w_out': 'grad_w', 'grad_conv_b_out': 'grad_w', 'grad_mix_ln_g': 'grad_w', 'grad_mix_ln_b': 'grad_w', 'grad_mlp_w1': 'grad_w', 'grad_mlp_b1': 'grad_w', 'grad_mlp_w2': 'grad_w', 'grad_mlp_b2': 'grad_w', 'grad_mlp_ln_g': 'grad_w', 'grad_mlp_ln_b': 'grad_w', 'delta_pool_w': 'delta_w', 'delta_pool_scale': 'delta_w', 'delta_conv_w_in': 'delta_w', 'delta_conv_b_in': 'delta_w', 'delta_conv_dw': 'delta_w', 'delta_conv_dw_b': 'delta_w', 'delta_conv_ln_g': 'delta_w', 'delta_conv_ln_b': 'delta_w', 'delta_conv_w_out': 'delta_w', 'delta_conv_b_out': 'delta_w', 'delta_mix_ln_g': 'delta_w', 'delta_mix_ln_b': 'delta_w', 'delta_mlp_w1': 'delta_w', 'delta_mlp_b1': 'delta_w', 'delta_mlp_w2': 'delta_w', 'delta_mlp_b2': 'delta_w', 'delta_mlp_ln_g': 'delta_w', 'delta_mlp_ln_b': 'delta_w', 'new_m_pool_w': 'new_m', 'new_m_pool_scale': 'new_m', 'new_m_conv_w_in': 'new_m', 'new_m_conv_b_in': 'new_m', 'new_m_conv_dw': 'new_m', 'new_m_conv_dw_b': 'new_m', 'new_m_conv_ln_g': 'new_m', 'new_m_conv_ln_b': 'new_m', 'new_m_conv_w_out': 'new_m', 'new_m_conv_b_out': 'new_m', 'new_m_mix_ln_g': 'new_m', 'new_m_mix_ln_b': 'new_m', 'new_m_mlp_w1': 'new_m', 'new_m_mlp_b1': 'new_m', 'new_m_mlp_w2': 'new_m', 'new_m_mlp_b2': 'new_m', 'new_m_mlp_ln_g': 'new_m', 'new_m_mlp_ln_b': 'new_m', 'new_v_pool_w': 'new_v', 'new_v_pool_scale': 'new_v', 'new_v_conv_w_in': 'new_v', 'new_v_conv_b_in': 'new_v', 'new_v_conv_dw': 'new_v', 'new_v_conv_dw_b': 'new_v', 'new_v_conv_ln_g': 'new_v', 'new_v_conv_ln_b': 'new_v', 'new_v_conv_w_out': 'new_v', 'new_v_conv_b_out': 'new_v', 'new_v_mix_ln_g': 'new_v', 'new_v_mix_ln_b': 'new_v', 'new_v_mlp_w1': 'new_v', 'new_v_mlp_b1': 'new_v', 'new_v_mlp_w2': 'new_v', 'new_v_mlp_b2': 'new_v', 'new_v_mlp_ln_g': 'new_v', 'new_v_mlp_ln_b': 'new_v'}


def _forward(args):
    return _fwd_reference(*[args[k] for k in FWD_PARAMS])


def _output_shape():
    def fwd():
        inp = _fwd_setup_inputs(0)
        return _fwd_reference(*[inp[k] for k in FWD_PARAMS])
    out = _jax.eval_shape(fwd)
    return out.shape, out.dtype

N_MICROBATCH = 1
ADAM_LR = 0.001
ADAM_B1 = 0.9
ADAM_B2 = 0.999
ADAM_EPS = 1e-08
ADAM_WD = 0.01
ADAM_STEP = 10
PER_EXAMPLE_BATCH_AXIS = {'x': 0, 'loss_target': 0}
SHARED_INPUTS = []
_WEIGHT_DTYPES = {'pool_w': _jnp.float32, 'pool_scale': _jnp.float32, 'conv_w_in': _jnp.float32, 'conv_b_in': _jnp.float32, 'conv_dw': _jnp.float32, 'conv_dw_b': _jnp.float32, 'conv_ln_g': _jnp.float32, 'conv_ln_b': _jnp.float32, 'conv_w_out': _jnp.float32, 'conv_b_out': _jnp.float32, 'mix_ln_g': _jnp.float32, 'mix_ln_b': _jnp.float32, 'mlp_w1': _jnp.float32, 'mlp_b1': _jnp.float32, 'mlp_w2': _jnp.float32, 'mlp_b2': _jnp.float32, 'mlp_ln_g': _jnp.float32, 'mlp_ln_b': _jnp.float32}
MOMENT_SCALE = {'pool_w': 1.908989e-01, 'pool_scale': 4.868434e-01, 'conv_w_in': 4.131363e-02, 'conv_b_in': 1.484594e-01, 'conv_dw': 5.669224e-02, 'conv_dw_b': 3.288296e-01, 'conv_ln_g': 1.310730e-01, 'conv_ln_b': 2.008370e-01, 'conv_w_out': 1.634042e-01, 'conv_b_out': 8.221036e-01, 'mix_ln_g': 6.140858e+00, 'mix_ln_b': 1.182544e+00, 'mlp_w1': 3.273889e-02, 'mlp_b1': 7.702982e-02, 'mlp_w2': 9.809167e-02, 'mlp_b2': 8.360519e-01, 'mlp_ln_g': 4.609310e+01, 'mlp_ln_b': 4.641688e+00}


def _to_microbatches(a, axis):
    t = _jnp.moveaxis(a, axis, 0)
    t = t.reshape((N_MICROBATCH, t.shape[0] // N_MICROBATCH) + t.shape[1:])
    return _jnp.moveaxis(t, 1, axis + 1)


def setup_inputs(seed: int = 0) -> dict:
    inp = _fwd_setup_inputs(seed)
    key = _jax.random.fold_in(_jax.random.key(seed), 7919)
    shape, _ = _output_shape()
    out = dict(inp)
    out["loss_target"] = _jax.random.normal(_jax.random.fold_in(key, 0), shape, _jnp.float32)
    for i, name in enumerate(TWIN_WEIGHTS):
        w = inp[name].astype(_jnp.float32)
        if MOMENT_SCALE is None:
            s = _jnp.sqrt(_jnp.mean(_jnp.square(w)) + 1e-30)
        else:
            s = MOMENT_SCALE[name]
        km, kv = _jax.random.split(_jax.random.fold_in(key, i + 1))
        out[name] = w
        out["m_" + name] = s * _jax.random.normal(km, w.shape, _jnp.float32)
        out["v_" + name] = (s * s) * _jax.random.uniform(kv, w.shape, _jnp.float32, 0.5, 1.5)
    if N_MICROBATCH > 1:
        for name, axis in PER_EXAMPLE_BATCH_AXIS.items():
            out[name] = _to_microbatches(out[name], axis)
    return {'x': out['x'], 'pool_w': out['pool_w'], 'pool_scale': out['pool_scale'], 'conv_w_in': out['conv_w_in'], 'conv_b_in': out['conv_b_in'], 'conv_dw': out['conv_dw'], 'conv_dw_b': out['conv_dw_b'], 'conv_ln_g': out['conv_ln_g'], 'conv_ln_b': out['conv_ln_b'], 'conv_w_out': out['conv_w_out'], 'conv_b_out': out['conv_b_out'], 'mix_ln_g': out['mix_ln_g'], 'mix_ln_b': out['mix_ln_b'], 'mlp_w1': out['mlp_w1'], 'mlp_b1': out['mlp_b1'], 'mlp_w2': out['mlp_w2'], 'mlp_b2': out['mlp_b2'], 'mlp_ln_g': out['mlp_ln_g'], 'mlp_ln_b': out['mlp_ln_b'], 'loss_target': out['loss_target'], 'm_pool_w': out['m_pool_w'], 'm_pool_scale': out['m_pool_scale'], 'm_conv_w_in': out['m_conv_w_in'], 'm_conv_b_in': out['m_conv_b_in'], 'm_conv_dw': out['m_conv_dw'], 'm_conv_dw_b': out['m_conv_dw_b'], 'm_conv_ln_g': out['m_conv_ln_g'], 'm_conv_ln_b': out['m_conv_ln_b'], 'm_conv_w_out': out['m_conv_w_out'], 'm_conv_b_out': out['m_conv_b_out'], 'm_mix_ln_g': out['m_mix_ln_g'], 'm_mix_ln_b': out['m_mix_ln_b'], 'm_mlp_w1': out['m_mlp_w1'], 'm_mlp_b1': out['m_mlp_b1'], 'm_mlp_w2': out['m_mlp_w2'], 'm_mlp_b2': out['m_mlp_b2'], 'm_mlp_ln_g': out['m_mlp_ln_g'], 'm_mlp_ln_b': out['m_mlp_ln_b'], 'v_pool_w': out['v_pool_w'], 'v_pool_scale': out['v_pool_scale'], 'v_conv_w_in': out['v_conv_w_in'], 'v_conv_b_in': out['v_conv_b_in'], 'v_conv_dw': out['v_conv_dw'], 'v_conv_dw_b': out['v_conv_dw_b'], 'v_conv_ln_g': out['v_conv_ln_g'], 'v_conv_ln_b': out['v_conv_ln_b'], 'v_conv_w_out': out['v_conv_w_out'], 'v_conv_b_out': out['v_conv_b_out'], 'v_mix_ln_g': out['v_mix_ln_g'], 'v_mix_ln_b': out['v_mix_ln_b'], 'v_mlp_w1': out['v_mlp_w1'], 'v_mlp_b1': out['v_mlp_b1'], 'v_mlp_w2': out['v_mlp_w2'], 'v_mlp_b2': out['v_mlp_b2'], 'v_mlp_ln_g': out['v_mlp_ln_g'], 'v_mlp_ln_b': out['v_mlp_ln_b']}


def _loss(weights, diff, rest, loss_target):
    with _jax.named_scope("forward"):
        args = {**rest, TWIN_DIFF_INPUT: diff, **{k: w.astype(_WEIGHT_DTYPES[k]) for k, w in weights.items()}}
        y = _forward(args)
    with _jax.named_scope("loss_head"):
        err = _jnp.square(y.astype(_jnp.float32) - loss_target)
        return 0.5 * _jnp.sum(_jnp.mean(err, axis=-1)) if err.ndim else 0.5 * err


def _adamw(w, g, m, v):
    m = ADAM_B1 * m + (1.0 - ADAM_B1) * g
    v = ADAM_B2 * v + (1.0 - ADAM_B2) * _jnp.square(g)
    m_hat = m / (1.0 - ADAM_B1 ** ADAM_STEP)
    v_hat = v / (1.0 - ADAM_B2 ** ADAM_STEP)
    delta = -ADAM_LR * (m_hat / (_jnp.sqrt(v_hat) + ADAM_EPS) + ADAM_WD * w)
    return delta, m, v


def reference(x, pool_w, pool_scale, conv_w_in, conv_b_in, conv_dw, conv_dw_b, conv_ln_g, conv_ln_b, conv_w_out, conv_b_out, mix_ln_g, mix_ln_b, mlp_w1, mlp_b1, mlp_w2, mlp_b2, mlp_ln_g, mlp_ln_b, loss_target, m_pool_w, m_pool_scale, m_conv_w_in, m_conv_b_in, m_conv_dw, m_conv_dw_b, m_conv_ln_g, m_conv_ln_b, m_conv_w_out, m_conv_b_out, m_mix_ln_g, m_mix_ln_b, m_mlp_w1, m_mlp_b1, m_mlp_w2, m_mlp_b2, m_mlp_ln_g, m_mlp_ln_b, v_pool_w, v_pool_scale, v_conv_w_in, v_conv_b_in, v_conv_dw, v_conv_dw_b, v_conv_ln_g, v_conv_ln_b, v_conv_w_out, v_conv_b_out, v_mix_ln_g, v_mix_ln_b, v_mlp_w1, v_mlp_b1, v_mlp_w2, v_mlp_b2, v_mlp_ln_g, v_mlp_ln_b):
    given = dict(x=x, pool_w=pool_w, pool_scale=pool_scale, conv_w_in=conv_w_in, conv_b_in=conv_b_in, conv_dw=conv_dw, conv_dw_b=conv_dw_b, conv_ln_g=conv_ln_g, conv_ln_b=conv_ln_b, conv_w_out=conv_w_out, conv_b_out=conv_b_out, mix_ln_g=mix_ln_g, mix_ln_b=mix_ln_b, mlp_w1=mlp_w1, mlp_b1=mlp_b1, mlp_w2=mlp_w2, mlp_b2=mlp_b2, mlp_ln_g=mlp_ln_g, mlp_ln_b=mlp_ln_b, loss_target=loss_target, m_pool_w=m_pool_w, m_pool_scale=m_pool_scale, m_conv_w_in=m_conv_w_in, m_conv_b_in=m_conv_b_in, m_conv_dw=m_conv_dw, m_conv_dw_b=m_conv_dw_b, m_conv_ln_g=m_conv_ln_g, m_conv_ln_b=m_conv_ln_b, m_conv_w_out=m_conv_w_out, m_conv_b_out=m_conv_b_out, m_mix_ln_g=m_mix_ln_g, m_mix_ln_b=m_mix_ln_b, m_mlp_w1=m_mlp_w1, m_mlp_b1=m_mlp_b1, m_mlp_w2=m_mlp_w2, m_mlp_b2=m_mlp_b2, m_mlp_ln_g=m_mlp_ln_g, m_mlp_ln_b=m_mlp_ln_b, v_pool_w=v_pool_w, v_pool_scale=v_pool_scale, v_conv_w_in=v_conv_w_in, v_conv_b_in=v_conv_b_in, v_conv_dw=v_conv_dw, v_conv_dw_b=v_conv_dw_b, v_conv_ln_g=v_conv_ln_g, v_conv_ln_b=v_conv_ln_b, v_conv_w_out=v_conv_w_out, v_conv_b_out=v_conv_b_out, v_mix_ln_g=v_mix_ln_g, v_mix_ln_b=v_mix_ln_b, v_mlp_w1=v_mlp_w1, v_mlp_b1=v_mlp_b1, v_mlp_w2=v_mlp_w2, v_mlp_b2=v_mlp_b2, v_mlp_ln_g=v_mlp_ln_g, v_mlp_ln_b=v_mlp_ln_b)
    weights = {n: given[n] for n in TWIN_WEIGHTS}
    shared = {n: given[n] for n in SHARED_INPUTS}
    per_example = {n: given[n] for n in ['x']}
    grad_fn = _jax.value_and_grad(_loss, argnums=(0, 1))

    def one_microbatch(ex, loss_target):
        ex = dict(ex)
        diff = ex.pop(TWIN_DIFF_INPUT)
        return grad_fn(weights, diff, {**shared, **ex}, loss_target)

    if N_MICROBATCH == 1:
        loss, (grad_w, grad_x) = one_microbatch(per_example, given["loss_target"])
    else:
        def body(carry, xs):
            loss_sum, grad_sum = carry
            l_k, (gw_k, gx_k) = one_microbatch(xs[0], xs[1])
            with _jax.named_scope("update"):
                return (loss_sum + l_k, _jax.tree.map(_jnp.add, grad_sum, gw_k)), gx_k

        init = (_jnp.zeros((), _jnp.float32), _jax.tree.map(_jnp.zeros_like, weights))
        (loss, grad_w), grad_x = _jax.lax.scan(body, init, (per_example, given["loss_target"]))
    with _jax.named_scope("update"):
        delta_w, new_m, new_v = {}, {}, {}
        for n in TWIN_WEIGHTS:
            delta_w[n], new_m[n], new_v[n] = _adamw(weights[n], grad_w[n], given["m_" + n], given["v_" + n])
    return (loss, grad_x, *[grad_w[n] for n in TWIN_WEIGHTS], *[delta_w[n] for n in TWIN_WEIGHTS],
            *[new_m[n] for n in TWIN_WEIGHTS], *[new_v[n] for n in TWIN_WEIGHTS])
```

```python
import jax
import jax.numpy as jnp
from jax import lax
from jax.experimental import pallas as pl
from jax.experimental.pallas import tpu as pltpu

N_DEV = 8
MESH_AXES = ("x", "y", "c")
POOL_WINDOWS = (2, 4, 8, 16)
POOL_HALO = 16
CONV_HALO = 32
LN_EPS = 1e-5
ADAM_LR = 0.001
ADAM_B1 = 0.9
ADAM_B2 = 0.999
ADAM_EPS = 1e-08
ADAM_WD = 0.01
ADAM_STEP = 10
VMEM_LIMIT = 56 * 1024 * 1024

F32 = jnp.float32
BF16 = jnp.bfloat16


def _mm(a, b):
    return lax.dot_general(a, b, (((1,), (0,)), ((), ())), preferred_element_type=F32)


def _mm_nt(a, b):
    return lax.dot_general(a, b, (((1,), (1,)), ((), ())), preferred_element_type=F32)


def _mm_tn(a, b):
    return lax.dot_general(a, b, (((0,), (0,)), ((), ())), preferred_element_type=F32)


def _tile(n, pref):
    t = min(n, pref)
    assert n % t == 0, (n, pref)
    return t


def _params(*sem):
    return pltpu.CompilerParams(dimension_semantics=sem, vmem_limit_bytes=VMEM_LIMIT)


def _full(shape):
    nd = len(shape)
    return pl.BlockSpec(shape, lambda *_: (0,) * nd)


def _rows(tm, d):
    return pl.BlockSpec((tm, d), lambda i: (i, 0))


def _ln_stats(z):
    mu = jnp.mean(z, axis=-1, keepdims=True)
    zc = z - mu
    var = jnp.mean(zc * zc, axis=-1, keepdims=True)
    rstd = lax.rsqrt(var + LN_EPS)
    return zc * rstd, rstd


def _ln_fwd(z, g, b):
    xhat, _ = _ln_stats(z)
    return xhat * g + b


def _ln_bwd(dy, z, g):
    xhat, rstd = _ln_stats(z)
    dxh = dy * g
    m1 = jnp.mean(dxh, axis=-1, keepdims=True)
    m2 = jnp.mean(dxh * xhat, axis=-1, keepdims=True)
    return rstd * (dxh - m1 - xhat * m2), xhat


def _colsum(v):
    return jnp.sum(v, axis=0, keepdims=True)


def _sigmoid(v):
    return 1.0 / (1.0 + jnp.exp(-v))


def _acc(ref, first, val):
    @pl.when(first)
    def _():
        ref[...] = val

    @pl.when(jnp.logical_not(first))
    def _():
        ref[...] += val


def _exchange(name, srcs, scatter, dsts):
    n = len(srcs)

    def body(*refs):
        src_refs, dst_refs = refs[:n], refs[n:2 * n]
        send_sems, recv_sems, local_sems = refs[2 * n:]
        x, y, c = (lax.axis_index(a) for a in MESH_AXES)
        me = 4 * x + 2 * y + c
        remote = []
        for d in range(1, N_DEV):
            px = (x + ((d >> 2) & 1)) % 2
            py = (y + ((d >> 1) & 1)) % 2
            pc = (c + (d & 1)) % 2
            pid = 4 * px + 2 * py + pc
            for k in range(n):
                src = src_refs[k].at[pid] if scatter[k] else src_refs[k]
                cp = pltpu.make_async_remote_copy(
                    src_ref=src, dst_ref=dst_refs[k].at[me],
                    send_sem=send_sems.at[k, d - 1], recv_sem=recv_sems.at[k, d - 1],
                    device_id=(px, py, pc), device_id_type=pl.DeviceIdType.MESH)
                cp.start()
                remote.append(cp)
        local = []
        for k in range(n):
            src = src_refs[k].at[me] if scatter[k] else src_refs[k]
            cp = pltpu.make_async_copy(src, dst_refs[k].at[me], local_sems.at[k])
            cp.start()
            local.append(cp)
        for cp in remote:
            cp.wait_recv()
        for cp in remote:
            cp.wait_send()
        for cp in local:
            cp.wait()

    any_spec = pl.BlockSpec(memory_space=pl.ANY)
    return pl.pallas_call(
        body, name=name,
        out_shape=[jax.ShapeDtypeStruct(s, dt) for s, dt in dsts],
        in_specs=[any_spec] * n, out_specs=[any_spec] * n,
        scratch_shapes=[pltpu.SemaphoreType.DMA((n, N_DEV - 1)), pltpu.SemaphoreType.DMA((n, N_DEV - 1)),
                        pltpu.SemaphoreType.DMA((n,))],
    )(*srcs)


def _pool_fwd(x, pw, scale, g, b, alpha, tm):
    T, D = x.shape
    G = len(POOL_WINDOWS)
    Dg = D // G
    H = POOL_HALO
    r = tm // H

    def body(x_ref, xh_ref, pw_ref, sc_ref, g_ref, b_ref, y_ref, z_ref, ext_ref):
        i = pl.program_id(0)
        ext_ref[0:H, :] = jnp.where(i > 0, xh_ref[...], 0.0)
        ext_ref[H:, :] = x_ref[...]
        row = i * tm + lax.broadcasted_iota(jnp.int32, (tm, 1), 0)
        rowf = (row + 1).astype(F32)
        for gi, w in enumerate(POOL_WINDOWS):
            sl = slice(gi * Dg, (gi + 1) * Dg)
            s = ext_ref[:, sl]
            k = 1
            while k < w:
                s = s + pltpu.roll(s, k, 0)
                k *= 2
            inv = 1.0 / jnp.minimum(rowf, float(w))
            xg = x_ref[:, sl]
            d = s[H:, :] * inv - xg
            mix = _mm(d.astype(BF16), pw_ref[gi]) * sc_ref[:, sl]
            z_ref[:, sl] = alpha * xg + mix
        y_ref[...] = _ln_fwd(z_ref[...], g_ref[...], b_ref[...])

    return pl.pallas_call(
        body, name="pool_fwd", grid=(T // tm,),
        out_shape=[jax.ShapeDtypeStruct((T, D), F32)] * 2,
        in_specs=[_rows(tm, D), pl.BlockSpec((H, D), lambda i: (jnp.maximum(i * r - 1, 0), 0)),
                  _full(pw.shape), _full((1, D)), _full((1, D)), _full((1, D))],
        out_specs=[_rows(tm, D)] * 2,
        scratch_shapes=[pltpu.VMEM((tm + H, D), F32)],
        compiler_params=_params("parallel"),
    )(x, x, pw, scale, g, b)


def _mlp_up(name, h, w1b, b1, tm):
    T, D = h.shape
    nb, _, Fs = w1b.shape
    Fd = nb * Fs

    def body(h_ref, w_ref, b_ref, u_ref, a_ref):
        hb = h_ref[...].astype(BF16)
        for j in range(nb):
            sl = slice(j * Fs, (j + 1) * Fs)
            u = _mm(hb, w_ref[j]) + b_ref[:, sl]
            u_ref[:, sl] = u
            rl = jnp.maximum(u, 0.0)
            a_ref[:, sl] = (rl * rl).astype(BF16)

    return pl.pallas_call(
        body, name=name, grid=(T // tm,),
        out_shape=[jax.ShapeDtypeStruct((T, Fd), F32), jax.ShapeDtypeStruct((T, Fd), BF16)],
        in_specs=[_rows(tm, D), _full(w1b.shape), _full((1, Fd))],
        out_specs=[_rows(tm, Fd), _rows(tm, Fd)],
        compiler_params=_params("parallel"),
    )(h, w1b, b1)


def _proj_ln(name, a, w, bias, res, g, b, alpha, tm):
    T, K = a.shape
    D = w.shape[1]

    def body(a_ref, w_ref, bias_ref, res_ref, g_ref, b_ref, y_ref, z_ref):
        z = alpha * res_ref[...] + _mm(a_ref[...], w_ref[...]) + bias_ref[...]
        z_ref[...] = z
        y_ref[...] = _ln_fwd(z, g_ref[...], b_ref[...])

    return pl.pallas_call(
        body, name=name, grid=(T // tm,),
        out_shape=[jax.ShapeDtypeStruct((T, D), F32)] * 2,
        in_specs=[_rows(tm, K), _full((K, D)), _full((1, D)), _rows(tm, D), _full((1, D)), _full((1, D))],
        out_specs=[_rows(tm, D)] * 2,
        compiler_params=_params("parallel"),
    )(a, w, bias, res, g, b)


def _conv_in(x, wb, b_in, tm):
    T, D = x.shape
    nb, _, Ns = wb.shape
    half = nb // 2

    def body(x_ref, w_ref, b_ref, p_ref, glu_ref):
        xb = x_ref[...].astype(BF16)
        for j in range(half):
            sa = slice(j * Ns, (j + 1) * Ns)
            sg = slice(D + j * Ns, D + (j + 1) * Ns)
            a = _mm(xb, w_ref[j]) + b_ref[:, sa]
            gate = _mm(xb, w_ref[half + j]) + b_ref[:, sg]
            p_ref[:, sa] = a
            p_ref[:, sg] = gate
            glu_ref[:, sa] = a * _sigmoid(gate)

    return pl.pallas_call(
        body, name="conv_in", grid=(T // tm,),
        out_shape=[jax.ShapeDtypeStruct((T, 2 * D), F32), jax.ShapeDtypeStruct((T, D), F32)],
        in_specs=[_rows(tm, D), _full(wb.shape), _full((1, 2 * D))],
        out_specs=[_rows(tm, 2 * D), _rows(tm, D)],
        compiler_params=_params("parallel"),
    )(x, wb, b_in)


def _lane_chunk(d):
    return 128 if d % 128 == 0 else d


def _dwconv_fwd(glu, dw, dw_b, g, b, tm):
    T, D = glu.shape
    K = dw.shape[0]
    H = CONV_HALO
    off = H - (K - 1)
    r = tm // H
    cc = _lane_chunk(D)

    def body(x_ref, xh_ref, dw_ref, dwb_ref, g_ref, b_ref, cz_ref, s_ref, ext_ref):
        i = pl.program_id(0)
        ext_ref[0:H, :] = jnp.where(i > 0, xh_ref[...], 0.0)
        ext_ref[H:, :] = x_ref[...]
        for c0 in range(0, D, cc):
            cs = slice(c0, c0 + cc)
            acc = jnp.zeros((tm, cc), F32) + dwb_ref[:, cs]
            for k in range(K):
                acc = acc + ext_ref[off + k:off + k + tm, cs] * dw_ref[k:k + 1, cs]
            cz_ref[:, cs] = acc
        ln = _ln_fwd(cz_ref[...], g_ref[...], b_ref[...])
        s_ref[...] = (ln * _sigmoid(ln)).astype(BF16)

    return pl.pallas_call(
        body, name="dwconv_fwd", grid=(T // tm,),
        out_shape=[jax.ShapeDtypeStruct((T, D), F32), jax.ShapeDtypeStruct((T, D), BF16)],
        in_specs=[_rows(tm, D), pl.BlockSpec((H, D), lambda i: (jnp.maximum(i * r - 1, 0), 0)),
                  _full((K, D)), _full((1, D)), _full((1, D)), _full((1, D))],
        out_specs=[_rows(tm, D)] * 2,
        scratch_shapes=[pltpu.VMEM((tm + H, D), F32)],
        compiler_params=_params("parallel"),
    )(glu, glu, dw, dw_b, g, b)


def _ln_bwd_call(name, dy_or_y, target, z, g, tm):
    T, D = z.shape
    with_loss = target is not None

    def body(*refs):
        if with_loss:
            y_ref, t_ref, z_ref, g_ref, dz_ref, dzb_ref, gg_ref, gb_ref, sdz_ref, loss_ref = refs
            e = y_ref[...] - t_ref[...]
            dy = e * (1.0 / D)
        else:
            y_ref, z_ref, g_ref, dz_ref, dzb_ref, gg_ref, gb_ref, sdz_ref = refs
            dy = y_ref[...]
        first = pl.program_id(0) == 0
        dz, xhat = _ln_bwd(dy, z_ref[...], g_ref[...])
        dz_ref[...] = dz
        dzb_ref[...] = dz.astype(BF16)
        _acc(gg_ref, first, _colsum(dy * xhat))
        _acc(gb_ref, first, _colsum(dy))
        _acc(sdz_ref, first, _colsum(dz))
        if with_loss:
            _acc(loss_ref, first, _colsum(e * e))

    n_acc = 4 if with_loss else 3
    ins = [dy_or_y] + ([target] if with_loss else []) + [z, g]
    in_specs = [_rows(tm, D)] * (len(ins) - 1) + [_full((1, D))]
    return pl.pallas_call(
        body, name=name, grid=(T // tm,),
        out_shape=[jax.ShapeDtypeStruct((T, D), F32), jax.ShapeDtypeStruct((T, D), BF16)]
        + [jax.ShapeDtypeStruct((1, D), F32)] * n_acc,
        in_specs=in_specs,
        out_specs=[_rows(tm, D)] * 2 + [_full((1, D))] * n_acc,
        compiler_params=_params("arbitrary"),
    )(*ins)


def _mlp_bwd(name, dz, dzb, u, w2b, w1b, alpha, tm):
    T, D = dz.shape
    nb, Fs, _ = w2b.shape
    Fd = nb * Fs

    def body(dz_ref, dzb_ref, u_ref, w2_ref, w1_ref, du_ref, dx_ref, gb1_ref):
        first = pl.program_id(0) == 0
        dzv = dzb_ref[...]
        dh = alpha * dz_ref[...]
        for j in range(nb):
            sl = slice(j * Fs, (j + 1) * Fs)
            da = _mm_nt(dzv, w2_ref[j])
            du = da * (2.0 * jnp.maximum(u_ref[:, sl], 0.0))
            dub = du.astype(BF16)
            du_ref[:, sl] = dub
            _acc(gb1_ref.at[:, sl], first, _colsum(du))
            dh = dh + _mm_nt(dub, w1_ref[j])
        dx_ref[...] = dh

    return pl.pallas_call(
        body, name=name, grid=(T // tm,),
        out_shape=[jax.ShapeDtypeStruct((T, Fd), BF16), jax.ShapeDtypeStruct((T, D), F32),
                   jax.ShapeDtypeStruct((1, Fd), F32)],
        in_specs=[_rows(tm, D), _rows(tm, D), _rows(tm, Fd), _full(w2b.shape), _full(w1b.shape)],
        out_specs=[_rows(tm, Fd), _rows(tm, D), _full((1, Fd))],
        compiler_params=_params("arbitrary"),
    )(dz, dzb, u, w2b, w1b)


def _wgrad(name, xm, dy, col_blocks, nk, nj, tm):
    T, K = xm.shape
    N = dy.shape[1]
    Kb, Nb = K // nk, N // nj
    nt = T // tm
    if col_blocks:
        per, Ns = N_DEV // nj, N // N_DEV
        out_shape = (N_DEV, K, Ns)
        out_spec = pl.BlockSpec((per, Kb, Ns), lambda k, j, t: (j, k, 0))
    else:
        per, Ks = N_DEV // nk, K // N_DEV
        out_shape = (N_DEV, Ks, N)
        out_spec = pl.BlockSpec((per, Ks, Nb), lambda k, j, t: (k, 0, j))

    def body(x_ref, dy_ref, o_ref, acc_ref):
        t = pl.program_id(2)
        _acc(acc_ref, t == 0, _mm_tn(x_ref[...].astype(BF16), dy_ref[...]))

        @pl.when(t == nt - 1)
        def _():
            for q in range(per):
                if col_blocks:
                    o_ref[q] = acc_ref[:, q * Ns:(q + 1) * Ns].astype(BF16)
                else:
                    o_ref[q] = acc_ref[q * Ks:(q + 1) * Ks, :].astype(BF16)

    return pl.pallas_call(
        body, name=name, grid=(nk, nj, nt),
        out_shape=jax.ShapeDtypeStruct(out_shape, BF16),
        in_specs=[pl.BlockSpec((tm, Kb), lambda k, j, t: (t, k)), pl.BlockSpec((tm, Nb), lambda k, j, t: (t, j))],
        out_specs=out_spec,
        scratch_shapes=[pltpu.VMEM((Kb, Nb), F32)],
        compiler_params=_params("parallel", "parallel", "arbitrary"),
    )(xm, dy)


def _conv_out_bwd(dzb, w_out, cz, g, b, tm):
    T, D = cz.shape

    def body(dz_ref, w_ref, cz_ref, g_ref, b_ref, dc_ref, gg_ref, gb_ref, sdc_ref):
        first = pl.program_id(0) == 0
        ds = _mm_nt(dz_ref[...], w_ref[...])
        czv = cz_ref[...]
        gv = g_ref[...]
        ln = _ln_fwd(czv, gv, b_ref[...])
        sg = _sigmoid(ln)
        dln = ds * (sg * (1.0 + ln * (1.0 - sg)))
        dc, xhat = _ln_bwd(dln, czv, gv)
        dc_ref[...] = dc
        _acc(gg_ref, first, _colsum(dln * xhat))
        _acc(gb_ref, first, _colsum(dln))
        _acc(sdc_ref, first, _colsum(dc))

    return pl.pallas_call(
        body, name="conv_out_bwd", grid=(T // tm,),
        out_shape=[jax.ShapeDtypeStruct((T, D), F32)] + [jax.ShapeDtypeStruct((1, D), F32)] * 3,
        in_specs=[_rows(tm, D), _full((D, D)), _rows(tm, D), _full((1, D)), _full((1, D))],
        out_specs=[_rows(tm, D)] + [_full((1, D))] * 3,
        compiler_params=_params("arbitrary"),
    )(dzb, w_out, cz, g, b)


def _dwconv_bwd(dc, glu, p, dw, tm):
    T, D = dc.shape
    K = dw.shape[0]
    Kp = -(-K // 8) * 8
    H = CONV_HALO
    off = H - (K - 1)
    r = tm // H
    last = T // H - 1
    nt = T // tm
    cc = _lane_chunk(D)

    def body(dc_ref, dch_ref, x_ref, xh_ref, p_ref, dw_ref, dp_ref, gdw_ref, gbin_ref, edc_ref, ex_ref, dglu_ref):
        i = pl.program_id(0)
        first = i == 0
        edc_ref[0:tm, :] = dc_ref[...]
        edc_ref[tm:, :] = jnp.where(i < nt - 1, dch_ref[...], 0.0)
        ex_ref[0:H, :] = jnp.where(i > 0, xh_ref[...], 0.0)
        ex_ref[H:, :] = x_ref[...]

        @pl.when(first)
        def _():
            gdw_ref[...] = jnp.zeros_like(gdw_ref)

        for c0 in range(0, D, cc):
            cs = slice(c0, c0 + cc)
            dcv = dc_ref[:, cs]
            acc = jnp.zeros((tm, cc), F32)
            for k in range(K):
                acc = acc + edc_ref[K - 1 - k:K - 1 - k + tm, cs] * dw_ref[k:k + 1, cs]
                gdw_ref[k:k + 1, cs] += _colsum(dcv * ex_ref[off + k:off + k + tm, cs])
            dglu_ref[:, cs] = acc
        dglu = dglu_ref[...]
        a = p_ref[:, 0:D]
        sg = _sigmoid(p_ref[:, D:2 * D])
        da = dglu * sg
        dgate = dglu * a * (sg * (1.0 - sg))
        dp_ref[:, 0:D] = da.astype(BF16)
        dp_ref[:, D:2 * D] = dgate.astype(BF16)
        _acc(gbin_ref.at[:, 0:D], first, _colsum(da))
        _acc(gbin_ref.at[:, D:2 * D], first, _colsum(dgate))

    return pl.pallas_call(
        body, name="dwconv_bwd", grid=(nt,),
        out_shape=[jax.ShapeDtypeStruct((T, 2 * D), BF16), jax.ShapeDtypeStruct((Kp, D), F32),
                   jax.ShapeDtypeStruct((1, 2 * D), F32)],
        in_specs=[_rows(tm, D), pl.BlockSpec((H, D), lambda i: (jnp.minimum((i + 1) * r, last), 0)),
                  _rows(tm, D), pl.BlockSpec((H, D), lambda i: (jnp.maximum(i * r - 1, 0), 0)),
                  _rows(tm, 2 * D), _full((K, D))],
        out_specs=[_rows(tm, 2 * D), _full((Kp, D)), _full((1, 2 * D))],
        scratch_shapes=[pltpu.VMEM((tm + H, D), F32), pltpu.VMEM((tm + H, D), F32), pltpu.VMEM((tm, D), F32)],
        compiler_params=_params("arbitrary"),
    )(dc, dc, glu, glu, p, dw)


def _dx_proj(name, dz, dy, wb, alpha, tm):
    T, D = dz.shape
    nb, _, Ns = wb.shape
    N = nb * Ns

    def body(dz_ref, dy_ref, w_ref, dx_ref):
        dx = alpha * dz_ref[...]
        for j in range(nb):
            dx = dx + _mm_nt(dy_ref[:, j * Ns:(j + 1) * Ns], w_ref[j])
        dx_ref[...] = dx

    return pl.pallas_call(
        body, name=name, grid=(T // tm,),
        out_shape=jax.ShapeDtypeStruct((T, D), F32),
        in_specs=[_rows(tm, D), _rows(tm, N), _full(wb.shape)],
        out_specs=_rows(tm, D),
        compiler_params=_params("parallel"),
    )(dz, dy, wb)


def _pool_bwd(dz, x, pw, scale, alpha, tm):
    T, D = x.shape
    G = len(POOL_WINDOWS)
    Dg = D // G
    H = POOL_HALO
    r = tm // H
    last = T // H - 1
    nt = T // tm
    n_ext = tm + H

    def body(dz_ref, dzh_ref, x_ref, xh_ref, pw_ref, sc_ref, dx_ref, gpw_ref, gsc_ref, edz_ref, ex_ref):
        i = pl.program_id(0)
        first = i == 0
        edz_ref[0:tm, :] = dz_ref[...]
        edz_ref[tm:, :] = jnp.where(i < nt - 1, dzh_ref[...], 0.0)
        ex_ref[0:H, :] = jnp.where(i > 0, xh_ref[...], 0.0)
        ex_ref[H:, :] = x_ref[...]
        row = i * tm + lax.broadcasted_iota(jnp.int32, (tm, 1), 0)
        rowf = (row + 1).astype(F32)
        erow = i * tm + lax.broadcasted_iota(jnp.int32, (n_ext, 1), 0)
        erowf = (erow + 1).astype(F32)
        for gi, w in enumerate(POOL_WINDOWS):
            sl = slice(gi * Dg, (gi + 1) * Dg)
            s = ex_ref[:, sl]
            k = 1
            while k < w:
                s = s + pltpu.roll(s, k, 0)
                k *= 2
            xg = x_ref[:, sl]
            d = (s[H:, :] * (1.0 / jnp.minimum(rowf, float(w))) - xg).astype(BF16)
            wg = pw_ref[gi]
            premix = _mm(d, wg)
            dzg = dz_ref[:, sl]
            _acc(gsc_ref.at[:, sl], first, _colsum(dzg * premix))
            dpre = edz_ref[:, sl] * sc_ref[:, sl]
            dpre_b = dpre.astype(BF16)
            _acc(gpw_ref.at[gi], first, _mm_tn(d, dpre_b[0:tm, :]))
            dd = _mm_nt(dpre_b, wg)
            e = dd * (1.0 / jnp.minimum(erowf, float(w)))
            k = 1
            while k < w:
                e = e + pltpu.roll(e, n_ext - k, 0)
                k *= 2
            dx_ref[:, sl] = alpha * dzg + e[0:tm, :] - dd[0:tm, :]

    return pl.pallas_call(
        body, name="pool_bwd", grid=(nt,),
        out_shape=[jax.ShapeDtypeStruct((T, D), F32), jax.ShapeDtypeStruct((G, Dg, Dg), F32),
                   jax.ShapeDtypeStruct((1, D), F32)],
        in_specs=[_rows(tm, D), pl.BlockSpec((H, D), lambda i: (jnp.minimum((i + 1) * r, last), 0)),
                  _rows(tm, D), pl.BlockSpec((H, D), lambda i: (jnp.maximum(i * r - 1, 0), 0)),
                  _full(pw.shape), _full((1, D))],
        out_specs=[_rows(tm, D), _full((G, Dg, Dg)), _full((1, D))],
        scratch_shapes=[pltpu.VMEM((n_ext, D), F32), pltpu.VMEM((n_ext, D), F32)],
        compiler_params=_params("arbitrary"),
    )(dz, dz, x, x, pw, scale)


def _adamw(name, recv, w, m, v, tm):
    R, C = w.shape
    c1 = 1.0 - ADAM_B1 ** ADAM_STEP
    c2 = 1.0 - ADAM_B2 ** ADAM_STEP

    def body(r_ref, w_ref, m_ref, v_ref, g_ref, d_ref, nm_ref, nv_ref):
        g = r_ref[0].astype(F32)
        for s in range(1, N_DEV):
            g = g + r_ref[s].astype(F32)
        m1 = ADAM_B1 * m_ref[...] + (1.0 - ADAM_B1) * g
        v1 = ADAM_B2 * v_ref[...] + (1.0 - ADAM_B2) * (g * g)
        m_hat = m1 / c1
        v_hat = v1 / c2
        g_ref[...] = g
        d_ref[...] = -ADAM_LR * (m_hat / (jnp.sqrt(v_hat) + ADAM_EPS) + ADAM_WD * w_ref[...])
        nm_ref[...] = m1
        nv_ref[...] = v1

    return pl.pallas_call(
        body, name=name, grid=(R // tm,),
        out_shape=[jax.ShapeDtypeStruct((R, C), F32)] * 4,
        in_specs=[pl.BlockSpec((N_DEV, tm, C), lambda i: (0, i, 0))] + [_rows(tm, C)] * 3,
        out_specs=[_rows(tm, C)] * 4,
        compiler_params=_params("parallel"),
    )(recv, w, m, v)


def _pad_rows(a, rows):
    return jnp.pad(a, ((0, rows - a.shape[0]), (0, 0)))


def kernel(x, pool_w, pool_scale, conv_w_in, conv_b_in, conv_dw, conv_dw_b, conv_ln_g, conv_ln_b, conv_w_out, conv_b_out, mix_ln_g, mix_ln_b, mlp_w1, mlp_b1, mlp_w2, mlp_b2, mlp_ln_g, mlp_ln_b, loss_target, m_pool_w, m_pool_scale, m_conv_w_in, m_conv_b_in, m_conv_dw, m_conv_dw_b, m_conv_ln_g, m_conv_ln_b, m_conv_w_out, m_conv_b_out, m_mix_ln_g, m_mix_ln_b, m_mlp_w1, m_mlp_b1, m_mlp_w2, m_mlp_b2, m_mlp_ln_g, m_mlp_ln_b, v_pool_w, v_pool_scale, v_conv_w_in, v_conv_b_in, v_conv_dw, v_conv_dw_b, v_conv_ln_g, v_conv_ln_b, v_conv_w_out, v_conv_b_out, v_mix_ln_g, v_mix_ln_b, v_mlp_w1, v_mlp_b1, v_mlp_w2, v_mlp_b2, v_mlp_ln_g, v_mlp_ln_b):
    _, T, D = x.shape
    L = mlp_w1.shape[0]
    assert L == 2 and pool_w.shape[0] == 1 and conv_w_in.shape[0] == 1
    G = pool_w.shape[1]
    Dg = D // G
    Fd = mlp_b1.shape[1]
    Fs = Fd // N_DEV
    Kc = conv_dw.shape[1]
    Dc = D // N_DEV
    alpha = float((2.0 * L) ** 0.25)
    x2d, tgt = x[0], loss_target[0]

    tm = _tile(T, 512)
    tm_wide = _tile(T, 256)
    tm_conv = _tile(T, 256)
    tm_wg = _tile(T, 1024)

    def pack_sh(dw, dwb, lg, lb, bo, bi):
        rows = jnp.concatenate([dw[0], dwb, lg, lb, bo, bi.reshape(2, Dc)], axis=0)
        return _pad_rows(rows, SH_ROWS)

    SH_ROWS = -(-(Kc + 6) // 8) * 8
    def pack_rep(ps, mg, mb, b1, b2, lg, lb):
        rows = jnp.concatenate([ps, mg, mb, b1.reshape(L * Fd // D, D), b2, lg, lb], axis=0)
        return _pad_rows(rows, REP_ROWS)

    n_rep = 1 + 2 * L + L * Fd // D + 3 * L
    REP_ROWS = -(-n_rep // 8) * 8

    w_sh = pack_sh(conv_dw, conv_dw_b, conv_ln_g, conv_ln_b, conv_b_out, conv_b_in)
    m_sh = pack_sh(m_conv_dw, m_conv_dw_b, m_conv_ln_g, m_conv_ln_b, m_conv_b_out, m_conv_b_in)
    v_sh = pack_sh(v_conv_dw, v_conv_dw_b, v_conv_ln_g, v_conv_ln_b, v_conv_b_out, v_conv_b_in)
    w_rep = pack_rep(pool_scale, mix_ln_g, mix_ln_b, mlp_b1, mlp_b2, mlp_ln_g, mlp_ln_b)
    m_rep = pack_rep(m_pool_scale, m_mix_ln_g, m_mix_ln_b, m_mlp_b1, m_mlp_b2, m_mlp_ln_g, m_mlp_ln_b)
    v_rep = pack_rep(v_pool_scale, v_mix_ln_g, v_mix_ln_b, v_mlp_b1, v_mlp_b2, v_mlp_ln_g, v_mlp_ln_b)

    shards = [pool_w[0].astype(BF16), conv_w_in[0].astype(BF16), conv_w_out[0].astype(BF16),
              mlp_w1.astype(BF16), mlp_w2.astype(BF16), w_sh]
    full = _exchange("gather_weights", shards, [False] * len(shards),
                     [((N_DEV,) + s.shape, s.dtype) for s in shards])
    pw_all, win_b, wout_all, w1_all, w2_all, sh_all = full
    pw = pw_all.transpose(1, 0, 2, 3).reshape(G, Dg, Dg)
    w_out = wout_all.reshape(D, D)
    w1b = [w1_all[:, l] for l in range(L)]
    w2b = [w2_all[:, l] for l in range(L)]
    dw_full = sh_all[:, 0:Kc].transpose(1, 0, 2).reshape(Kc, D)

    def sh_row(i):
        return sh_all[:, i].reshape(1, D)

    dwb_full, cg_full, cb_full, bout_full = (sh_row(Kc + i) for i in range(4))
    bin_full = sh_all[:, Kc + 4:Kc + 6].reshape(1, 2 * D)

    h0, z_m0 = _pool_fwd(x2d, pw, pool_scale, mix_ln_g[0:1], mix_ln_b[0:1], alpha, tm)
    u0, a0 = _mlp_up("mlp_up0", h0, w1b[0], mlp_b1[0:1], tm_wide)
    x1, z_f0 = _proj_ln("mlp_down0", a0, w2b[0].reshape(Fd, D), mlp_b2[0:1], h0, mlp_ln_g[0:1], mlp_ln_b[0:1],
                        alpha, tm)
    p, glu = _conv_in(x1, win_b, bin_full, tm)
    cz, s = _dwconv_fwd(glu, dw_full, dwb_full, cg_full, cb_full, tm_conv)
    h1, z_m1 = _proj_ln("conv_out", s, w_out, bout_full, x1, mix_ln_g[1:2], mix_ln_b[1:2], alpha, tm)
    u1, a1 = _mlp_up("mlp_up1", h1, w1b[1], mlp_b1[1:2], tm_wide)
    x2, z_f1 = _proj_ln("mlp_down1", a1, w2b[1].reshape(Fd, D), mlp_b2[1:2], h1, mlp_ln_g[1:2], mlp_ln_b[1:2],
                        alpha, tm)

    dz, dzb, g_fg1, g_fb1, g_b2_1, loss_cols = _ln_bwd_call("ln_bwd_f1", x2, tgt, z_f1, mlp_ln_g[1:2], tm)
    du, dx, g_b1_1 = _mlp_bwd("mlp_bwd1", dz, dzb, u1, w2b[1], w1b[1], alpha, tm_wide)
    gw2_1 = _wgrad("gw2_1", a1, dzb, False, 4, 1, tm_wg)
    gw1_1 = _wgrad("gw1_1", h1, du, True, 1, 2, tm_wg)
    dz, dzb, g_mg1, g_mb1, g_bout, = _ln_bwd_call("ln_bwd_m1", dx, None, z_m1, mix_ln_g[1:2], tm)
    gwout = _wgrad("gw_out", s, dzb, False, 1, 1, tm_wg)
    dc, g_cg, g_cb, g_dwb = _conv_out_bwd(dzb, w_out, cz, cg_full, cb_full, tm)
    dp, g_dw, g_bin = _dwconv_bwd(dc, glu, p, dw_full, tm_conv)
    gwin = _wgrad("gw_in", x1, dp, True, 1, 2, tm_wg)
    dx = _dx_proj("conv_in_bwd", dz, dp, win_b, alpha, tm)
    dz, dzb, g_fg0, g_fb0, g_b2_0 = _ln_bwd_call("ln_bwd_f0", dx, None, z_f0, mlp_ln_g[0:1], tm)
    du, dx, g_b1_0 = _mlp_bwd("mlp_bwd0", dz, dzb, u0, w2b[0], w1b[0], alpha, tm_wide)
    gw2_0 = _wgrad("gw2_0", a0, dzb, False, 4, 1, tm_wg)
    gw1_0 = _wgrad("gw1_0", h0, du, True, 1, 2, tm_wg)
    dz, dzb, g_mg0, g_mb0, _ = _ln_bwd_call("ln_bwd_m0", dx, None, z_m0, mix_ln_g[0:1], tm)
    grad_x, g_pw, g_ps = _pool_bwd(dz, x2d, pw, pool_scale, alpha, tm)

    loss = lax.psum(0.5 / D * jnp.sum(loss_cols), MESH_AXES)

    gpw_b = g_pw.reshape(G, N_DEV, Dg // N_DEV, Dg).transpose(1, 0, 2, 3).astype(BF16)

    def to_dev(vec, rows):
        return vec.reshape(rows, N_DEV, Dc).transpose(1, 0, 2)

    g_sh = jnp.concatenate(
        [to_dev(g_dw[0:Kc], Kc), to_dev(g_dwb, 1), to_dev(g_cg, 1), to_dev(g_cb, 1), to_dev(g_bout, 1),
         g_bin.reshape(N_DEV, 2, Dc), jnp.zeros((N_DEV, SH_ROWS - Kc - 6, Dc), F32)], axis=1)
    g_rep = _pad_rows(jnp.concatenate(
        [g_ps, g_mg0, g_mg1, g_mb0, g_mb1, g_b1_0.reshape(Fd // D, D), g_b1_1.reshape(Fd // D, D),
         g_b2_0, g_b2_1, g_fg0, g_fg1, g_fb0, g_fb1], axis=0), REP_ROWS)

    srcs = [gpw_b, gwin, gwout, gw1_0, gw1_1, gw2_0, gw2_1, g_sh, g_rep]
    scatter = [True] * 8 + [False]
    dsts = [(s_.shape, s_.dtype) for s_ in srcs[:8]] + [((N_DEV,) + g_rep.shape, F32)]
    r_pw, r_win, r_wout, r_w1_0, r_w1_1, r_w2_0, r_w2_1, r_sh, r_rep = _exchange("exchange_grads", srcs, scatter, dsts)

    def upd(name, recv, w, m, v):
        shape = w.shape
        C = shape[-1]
        R = w.size // C
        outs = _adamw(name, recv.reshape(N_DEV, R, C), w.reshape(R, C), m.reshape(R, C), v.reshape(R, C), _tile(R, 256))
        return [o.reshape(shape) for o in outs]

    o_pw = upd("adam_pool_w", r_pw, pool_w, m_pool_w, v_pool_w)
    o_win = upd("adam_w_in", r_win, conv_w_in, m_conv_w_in, v_conv_w_in)
    o_wout = upd("adam_w_out", r_wout, conv_w_out, m_conv_w_out, v_conv_w_out)
    o_w1 = upd("adam_w1", jnp.stack([r_w1_0, r_w1_1], axis=1), mlp_w1, m_mlp_w1, v_mlp_w1)
    o_w2 = upd("adam_w2", jnp.stack([r_w2_0, r_w2_1], axis=1), mlp_w2, m_mlp_w2, v_mlp_w2)
    o_sh = upd("adam_conv_vec", r_sh, w_sh, m_sh, v_sh)
    o_rep = upd("adam_replicated", r_rep, w_rep, m_rep, v_rep)

    def unpack_sh(a):
        return (a[0:Kc][None], a[Kc:Kc + 1], a[Kc + 1:Kc + 2], a[Kc + 2:Kc + 3], a[Kc + 3:Kc + 4],
                a[Kc + 4:Kc + 6].reshape(1, 2 * Dc))

    def unpack_rep(a):
        o = 0
        out = []
        for rows, shape in ((1, (1, D)), (L, (L, D)), (L, (L, D)), (L * Fd // D, (L, Fd)), (L, (L, D)), (L, (L, D)),
                            (L, (L, D))):
            out.append(a[o:o + rows].reshape(shape))
            o += rows
        return out

    results = []
    for kind in range(4):
        dwv, dwb, lg, lb, bo, bi = unpack_sh(o_sh[kind])
        ps, mg, mb, b1, b2, fg, fb = unpack_rep(o_rep[kind])
        results.append([o_pw[kind], ps, o_win[kind], bi, dwv, dwb, lg, lb, o_wout[kind], bo, mg, mb,
                        o_w1[kind], b1, o_w2[kind], b2, fg, fb])
    return (loss, grad_x[None], *results[0], *results[1], *results[2], *results[3])
```

```python
import jax
import jax.numpy as jnp
from jax import lax
from jax.experimental import pallas as pl
from jax.experimental.pallas import tpu as pltpu

N_DEV = 8
MESH_AXES = ("x", "y", "c")
POOL_WINDOWS = (2, 4, 8, 16)
POOL_HALO = 16
CONV_HALO = 32
LN_EPS = 1e-5
ADAM_LR = 0.001
ADAM_B1 = 0.9
ADAM_B2 = 0.999
ADAM_EPS = 1e-08
ADAM_WD = 0.01
ADAM_STEP = 10
VMEM_LIMIT = 56 * 1024 * 1024

F32 = jnp.float32
BF16 = jnp.bfloat16


def _mm(a, b):
    return lax.dot_general(a, b, (((1,), (0,)), ((), ())), preferred_element_type=F32)


def _mm_nt(a, b):
    return lax.dot_general(a, b, (((1,), (1,)), ((), ())), preferred_element_type=F32)


def _mm_tn(a, b):
    return lax.dot_general(a, b, (((0,), (0,)), ((), ())), preferred_element_type=F32)


def _tile(n, pref):
    t = min(n, pref)
    assert n % t == 0, (n, pref)
    return t


def _params(*sem):
    return pltpu.CompilerParams(dimension_semantics=sem, vmem_limit_bytes=VMEM_LIMIT)


def _full(shape):
    nd = len(shape)
    return pl.BlockSpec(shape, lambda *_: (0,) * nd)


def _rows(tm, d):
    return pl.BlockSpec((tm, d), lambda i: (i, 0))


def _ln_stats(z):
    mu = jnp.mean(z, axis=-1, keepdims=True)
    zc = z - mu
    var = jnp.mean(zc * zc, axis=-1, keepdims=True)
    rstd = lax.rsqrt(var + LN_EPS)
    return zc * rstd, rstd


def _ln_fwd(z, g, b):
    xhat, _ = _ln_stats(z)
    return xhat * g + b


def _ln_bwd(dy, z, g):
    xhat, rstd = _ln_stats(z)
    dxh = dy * g
    m1 = jnp.mean(dxh, axis=-1, keepdims=True)
    m2 = jnp.mean(dxh * xhat, axis=-1, keepdims=True)
    return rstd * (dxh - m1 - xhat * m2), xhat


def _colsum(v):
    return jnp.sum(v, axis=0, keepdims=True)


def _sigmoid(v):
    return 1.0 / (1.0 + jnp.exp(-v))


def _acc(ref, first, val):
    @pl.when(first)
    def _():
        ref[...] = val

    @pl.when(jnp.logical_not(first))
    def _():
        ref[...] += val


_HBM = pl.BlockSpec(memory_space=pltpu.HBM)
_SEM = pl.BlockSpec(memory_space=pltpu.SEMAPHORE)
_EFFECT = pltpu.SideEffectType.DATAFLOW_SIDE_EFFECTING


def _peers():
    x, y, c = (lax.axis_index(a) for a in MESH_AXES)
    out = []
    for d in range(1, N_DEV):
        px = (x + ((d >> 2) & 1)) % 2
        py = (y + ((d >> 1) & 1)) % 2
        pc = (c + (d & 1)) % 2
        out.append((d - 1, (px, py, pc), 4 * px + 2 * py + pc))
    return 4 * x + 2 * y + c, out


def _remote_copies(src_refs, land_refs, scatter, send_sems, recv_sems):
    me, peers = _peers()
    copies = []
    for i, pos, pid in peers:
        for k, (src, land) in enumerate(zip(src_refs, land_refs)):
            copies.append(pltpu.make_async_remote_copy(
                src_ref=src.at[pid] if scatter[k] else src, dst_ref=land.at[me],
                send_sem=send_sems.at[k * (N_DEV - 1) + i], recv_sem=recv_sems.at[k * (N_DEV - 1) + i],
                device_id=pos, device_id_type=pl.DeviceIdType.MESH))
    return copies


def _xstart(name, srcs, scatter):
    n = len(srcs)
    me = 4 * lax.axis_index("x") + 2 * lax.axis_index("y") + lax.axis_index("c")
    lands = []
    for s, sc in zip(srcs, scatter):
        own = lax.dynamic_index_in_dim(s, me, 0, keepdims=True) if sc else s[None]
        shape = s.shape if sc else (N_DEV,) + s.shape
        lands.append(lax.dynamic_update_slice(lax.empty(shape, s.dtype), own, (me,) + (0,) * (len(shape) - 1)))

    def body(*refs):
        src_refs, land_refs = refs[:n], refs[n:2 * n]
        send_sems, recv_sems = refs[2 * n], refs[2 * n + 1]
        token = refs[-1]
        for cp in _remote_copies(src_refs, land_refs, scatter, send_sems, recv_sems):
            cp.start()
        token[...] = jnp.zeros_like(token)

    outs = pl.pallas_call(
        body, name=name,
        out_shape=(pltpu.SemaphoreType.DMA((n * (N_DEV - 1),)), pltpu.SemaphoreType.DMA((n * (N_DEV - 1),)),
                   *[pltpu.HBM(a.shape, a.dtype) for a in srcs + lands], jax.ShapeDtypeStruct((8, 128), F32)),
        in_specs=(_HBM,) * (2 * n),
        out_specs=(_SEM, _SEM) + (_HBM,) * (2 * n) + (pl.BlockSpec(memory_space=pltpu.VMEM),),
        input_output_aliases={i: 2 + i for i in range(2 * n)},
        compiler_params=pltpu.CompilerParams(has_side_effects=_EFFECT),
    )(*[pltpu.with_memory_space_constraint(a, pltpu.HBM) for a in srcs + lands])
    return (name, scatter, outs[0], outs[1], outs[2:2 + n], outs[2 + n:2 + 2 * n]), outs[-1]


def _xwait(handle, after):
    name, scatter, send_sems, recv_sems, srcs, lands = handle
    n = len(srcs)

    def body(*refs):
        src_refs, land_refs = refs[:n], refs[n:2 * n]
        send, recv = refs[2 * n], refs[2 * n + 1]
        copies = _remote_copies(src_refs, land_refs, scatter, send, recv)
        for cp in copies:
            cp.wait_send()
        for cp in copies:
            cp.wait_recv()

    outs = pl.pallas_call(
        body, name=name + "_wait",
        out_shape=tuple(pltpu.HBM(a.shape, a.dtype) for a in (*srcs, *lands)),
        in_specs=(_HBM,) * (2 * n) + (_SEM, _SEM, pl.BlockSpec(memory_space=pl.ANY)),
        out_specs=(_HBM,) * (2 * n),
        input_output_aliases={i: i for i in range(2 * n)},
        compiler_params=pltpu.CompilerParams(has_side_effects=_EFFECT),
    )(*srcs, *lands, send_sems, recv_sems, after)
    return outs[n:]


def _tied_call(body, deps, in_specs, **kw):
    nd = len(deps)

    def tied_body(*refs):
        body(*refs[nd:])

    call = pl.pallas_call(tied_body, in_specs=[pl.BlockSpec(memory_space=pl.ANY)] * nd + list(in_specs), **kw)
    return lambda *args: call(*deps, *args)


def _pool_fwd(x, pw, scale, g, b, alpha, tm, deps=()):
    T, D = x.shape
    G = len(POOL_WINDOWS)
    Dg = D // G
    H = POOL_HALO
    r = tm // H

    def body(x_ref, xh_ref, pw_ref, sc_ref, g_ref, b_ref, y_ref, z_ref, ext_ref):
        i = pl.program_id(0)
        ext_ref[0:H, :] = jnp.where(i > 0, xh_ref[...], 0.0)
        ext_ref[H:, :] = x_ref[...]
        row = i * tm + lax.broadcasted_iota(jnp.int32, (tm, 1), 0)
        rowf = (row + 1).astype(F32)
        for gi, w in enumerate(POOL_WINDOWS):
            sl = slice(gi * Dg, (gi + 1) * Dg)
            s = ext_ref[:, sl]
            k = 1
            while k < w:
                s = s + pltpu.roll(s, k, 0)
                k *= 2
            inv = 1.0 / jnp.minimum(rowf, float(w))
            xg = x_ref[:, sl]
            d = s[H:, :] * inv - xg
            mix = _mm(d.astype(BF16), pw_ref[gi]) * sc_ref[:, sl]
            z_ref[:, sl] = alpha * xg + mix
        y_ref[...] = _ln_fwd(z_ref[...], g_ref[...], b_ref[...])

    return _tied_call(
        body, deps, name="pool_fwd", grid=(T // tm,),
        out_shape=[jax.ShapeDtypeStruct((T, D), F32)] * 2,
        in_specs=[_rows(tm, D), pl.BlockSpec((H, D), lambda i: (jnp.maximum(i * r - 1, 0), 0)),
                  _full(pw.shape), _full((1, D)), _full((1, D)), _full((1, D))],
        out_specs=[_rows(tm, D)] * 2,
        scratch_shapes=[pltpu.VMEM((tm + H, D), F32)],
        compiler_params=_params("parallel"),
    )(x, x, pw, scale, g, b)


def _mlp_up(name, h, w1b, b1, tm):
    T, D = h.shape
    nb, _, Fs = w1b.shape
    Fd = nb * Fs

    def body(h_ref, w_ref, b_ref, u_ref, a_ref):
        hb = h_ref[...].astype(BF16)
        for j in range(nb):
            sl = slice(j * Fs, (j + 1) * Fs)
            u = _mm(hb, w_ref[j]) + b_ref[:, sl]
            u_ref[:, sl] = u
            rl = jnp.maximum(u, 0.0)
            a_ref[:, sl] = (rl * rl).astype(BF16)

    return pl.pallas_call(
        body, name=name, grid=(T // tm,),
        out_shape=[jax.ShapeDtypeStruct((T, Fd), F32), jax.ShapeDtypeStruct((T, Fd), BF16)],
        in_specs=[_rows(tm, D), _full(w1b.shape), _full((1, Fd))],
        out_specs=[_rows(tm, Fd), _rows(tm, Fd)],
        compiler_params=_params("parallel"),
    )(h, w1b, b1)


def _proj_ln(name, a, w, bias, res, g, b, alpha, tm):
    T, K = a.shape
    D = w.shape[1]

    def body(a_ref, w_ref, bias_ref, res_ref, g_ref, b_ref, y_ref, z_ref):
        z = alpha * res_ref[...] + _mm(a_ref[...], w_ref[...]) + bias_ref[...]
        z_ref[...] = z
        y_ref[...] = _ln_fwd(z, g_ref[...], b_ref[...])

    return pl.pallas_call(
        body, name=name, grid=(T // tm,),
        out_shape=[jax.ShapeDtypeStruct((T, D), F32)] * 2,
        in_specs=[_rows(tm, K), _full((K, D)), _full((1, D)), _rows(tm, D), _full((1, D)), _full((1, D))],
        out_specs=[_rows(tm, D)] * 2,
        compiler_params=_params("parallel"),
    )(a, w, bias, res, g, b)


def _conv_in(x, wb, b_in, tm):
    T, D = x.shape
    nb, _, Ns = wb.shape
    half = nb // 2

    def body(x_ref, w_ref, b_ref, p_ref, glu_ref):
        xb = x_ref[...].astype(BF16)
        for j in range(half):
            sa = slice(j * Ns, (j + 1) * Ns)
            sg = slice(D + j * Ns, D + (j + 1) * Ns)
            a = _mm(xb, w_ref[j]) + b_ref[:, sa]
            gate = _mm(xb, w_ref[half + j]) + b_ref[:, sg]
            p_ref[:, sa] = a
            p_ref[:, sg] = gate
            glu_ref[:, sa] = a * _sigmoid(gate)

    return pl.pallas_call(
        body, name="conv_in", grid=(T // tm,),
        out_shape=[jax.ShapeDtypeStruct((T, 2 * D), F32), jax.ShapeDtypeStruct((T, D), F32)],
        in_specs=[_rows(tm, D), _full(wb.shape), _full((1, 2 * D))],
        out_specs=[_rows(tm, 2 * D), _rows(tm, D)],
        compiler_params=_params("parallel"),
    )(x, wb, b_in)


def _lane_chunk(d):
    return 128 if d % 128 == 0 else d


def _dwconv_fwd(glu, dw, dw_b, g, b, tm):
    T, D = glu.shape
    K = dw.shape[0]
    H = CONV_HALO
    off = H - (K - 1)
    r = tm // H
    cc = _lane_chunk(D)

    def body(x_ref, xh_ref, dw_ref, dwb_ref, g_ref, b_ref, cz_ref, s_ref, ext_ref):
        i = pl.program_id(0)
        ext_ref[0:H, :] = jnp.where(i > 0, xh_ref[...], 0.0)
        ext_ref[H:, :] = x_ref[...]
        for c0 in range(0, D, cc):
            cs = slice(c0, c0 + cc)
            acc = jnp.zeros((tm, cc), F32) + dwb_ref[:, cs]
            for k in range(K):
                acc = acc + ext_ref[off + k:off + k + tm, cs] * dw_ref[k:k + 1, cs]
            cz_ref[:, cs] = acc
        ln = _ln_fwd(cz_ref[...], g_ref[...], b_ref[...])
        s_ref[...] = (ln * _sigmoid(ln)).astype(BF16)

    return pl.pallas_call(
        body, name="dwconv_fwd", grid=(T // tm,),
        out_shape=[jax.ShapeDtypeStruct((T, D), F32), jax.ShapeDtypeStruct((T, D), BF16)],
        in_specs=[_rows(tm, D), pl.BlockSpec((H, D), lambda i: (jnp.maximum(i * r - 1, 0), 0)),
                  _full((K, D)), _full((1, D)), _full((1, D)), _full((1, D))],
        out_specs=[_rows(tm, D)] * 2,
        scratch_shapes=[pltpu.VMEM((tm + H, D), F32)],
        compiler_params=_params("parallel"),
    )(glu, glu, dw, dw_b, g, b)


def _ln_bwd_call(name, dy_or_y, target, z, g, tm, deps=()):
    T, D = z.shape
    with_loss = target is not None

    def body(*refs):
        if with_loss:
            y_ref, t_ref, z_ref, g_ref, dz_ref, dzb_ref, gg_ref, gb_ref, sdz_ref, loss_ref = refs
            e = y_ref[...] - t_ref[...]
            dy = e * (1.0 / D)
        else:
            y_ref, z_ref, g_ref, dz_ref, dzb_ref, gg_ref, gb_ref, sdz_ref = refs
            dy = y_ref[...]
        first = pl.program_id(0) == 0
        dz, xhat = _ln_bwd(dy, z_ref[...], g_ref[...])
        dz_ref[...] = dz
        dzb_ref[...] = dz.astype(BF16)
        _acc(gg_ref, first, _colsum(dy * xhat))
        _acc(gb_ref, first, _colsum(dy))
        _acc(sdz_ref, first, _colsum(dz))
        if with_loss:
            _acc(loss_ref, first, _colsum(e * e))

    n_acc = 4 if with_loss else 3
    ins = [dy_or_y] + ([target] if with_loss else []) + [z, g]
    in_specs = [_rows(tm, D)] * (len(ins) - 1) + [_full((1, D))]
    return _tied_call(
        body, deps, name=name, grid=(T // tm,),
        out_shape=[jax.ShapeDtypeStruct((T, D), F32), jax.ShapeDtypeStruct((T, D), BF16)]
        + [jax.ShapeDtypeStruct((1, D), F32)] * n_acc,
        in_specs=in_specs,
        out_specs=[_rows(tm, D)] * 2 + [_full((1, D))] * n_acc,
        compiler_params=_params("arbitrary"),
    )(*ins)


def _mlp_bwd(name, dz, dzb, u, w2b, w1b, alpha, tm, deps=()):
    T, D = dz.shape
    nb, Fs, _ = w2b.shape
    Fd = nb * Fs

    def body(dz_ref, dzb_ref, u_ref, w2_ref, w1_ref, du_ref, dx_ref, gb1_ref):
        first = pl.program_id(0) == 0
        dzv = dzb_ref[...]
        dh = alpha * dz_ref[...]
        for j in range(nb):
            sl = slice(j * Fs, (j + 1) * Fs)
            da = _mm_nt(dzv, w2_ref[j])
            du = da * (2.0 * jnp.maximum(u_ref[:, sl], 0.0))
            dub = du.astype(BF16)
            du_ref[:, sl] = dub
            _acc(gb1_ref.at[:, sl], first, _colsum(du))
            dh = dh + _mm_nt(dub, w1_ref[j])
        dx_ref[...] = dh

    return _tied_call(
        body, deps, name=name, grid=(T // tm,),
        out_shape=[jax.ShapeDtypeStruct((T, Fd), BF16), jax.ShapeDtypeStruct((T, D), F32),
                   jax.ShapeDtypeStruct((1, Fd), F32)],
        in_specs=[_rows(tm, D), _rows(tm, D), _rows(tm, Fd), _full(w2b.shape), _full(w1b.shape)],
        out_specs=[_rows(tm, Fd), _rows(tm, D), _full((1, Fd))],
        compiler_params=_params("arbitrary"),
    )(dz, dzb, u, w2b, w1b)


def _wgrad(name, xm, dy, col_blocks, nk, nj, tm):
    T, K = xm.shape
    N = dy.shape[1]
    Kb, Nb = K // nk, N // nj
    nt = T // tm
    if col_blocks:
        per, Ns = N_DEV // nj, N // N_DEV
        out_shape = (N_DEV, K, Ns)
        out_spec = pl.BlockSpec((per, Kb, Ns), lambda k, j, t: (j, k, 0))
    else:
        per, Ks = N_DEV // nk, K // N_DEV
        out_shape = (N_DEV, Ks, N)
        out_spec = pl.BlockSpec((per, Ks, Nb), lambda k, j, t: (k, 0, j))

    def body(x_ref, dy_ref, o_ref, acc_ref):
        t = pl.program_id(2)
        _acc(acc_ref, t == 0, _mm_tn(x_ref[...].astype(BF16), dy_ref[...]))

        @pl.when(t == nt - 1)
        def _():
            for q in range(per):
                if col_blocks:
                    o_ref[q] = acc_ref[:, q * Ns:(q + 1) * Ns].astype(BF16)
                else:
                    o_ref[q] = acc_ref[q * Ks:(q + 1) * Ks, :].astype(BF16)

    return pl.pallas_call(
        body, name=name, grid=(nk, nj, nt),
        out_shape=jax.ShapeDtypeStruct(out_shape, BF16),
        in_specs=[pl.BlockSpec((tm, Kb), lambda k, j, t: (t, k)), pl.BlockSpec((tm, Nb), lambda k, j, t: (t, j))],
        out_specs=out_spec,
        scratch_shapes=[pltpu.VMEM((Kb, Nb), F32)],
        compiler_params=_params("parallel", "parallel", "arbitrary"),
    )(xm, dy)


def _conv_out_bwd(dzb, w_out, cz, g, b, tm):
    T, D = cz.shape

    def body(dz_ref, w_ref, cz_ref, g_ref, b_ref, dc_ref, gg_ref, gb_ref, sdc_ref):
        first = pl.program_id(0) == 0
        ds = _mm_nt(dz_ref[...], w_ref[...])
        czv = cz_ref[...]
        gv = g_ref[...]
        ln = _ln_fwd(czv, gv, b_ref[...])
        sg = _sigmoid(ln)
        dln = ds * (sg * (1.0 + ln * (1.0 - sg)))
        dc, xhat = _ln_bwd(dln, czv, gv)
        dc_ref[...] = dc
        _acc(gg_ref, first, _colsum(dln * xhat))
        _acc(gb_ref, first, _colsum(dln))
        _acc(sdc_ref, first, _colsum(dc))

    return pl.pallas_call(
        body, name="conv_out_bwd", grid=(T // tm,),
        out_shape=[jax.ShapeDtypeStruct((T, D), F32)] + [jax.ShapeDtypeStruct((1, D), F32)] * 3,
        in_specs=[_rows(tm, D), _full((D, D)), _rows(tm, D), _full((1, D)), _full((1, D))],
        out_specs=[_rows(tm, D)] + [_full((1, D))] * 3,
        compiler_params=_params("arbitrary"),
    )(dzb, w_out, cz, g, b)


def _dwconv_bwd(dc, glu, p, dw, tm):
    T, D = dc.shape
    K = dw.shape[0]
    Kp = -(-K // 8) * 8
    H = CONV_HALO
    off = H - (K - 1)
    r = tm // H
    last = T // H - 1
    nt = T // tm
    cc = _lane_chunk(D)

    def body(dc_ref, dch_ref, x_ref, xh_ref, p_ref, dw_ref, dp_ref, gdw_ref, gbin_ref, edc_ref, ex_ref, dglu_ref):
        i = pl.program_id(0)
        first = i == 0
        edc_ref[0:tm, :] = dc_ref[...]
        edc_ref[tm:, :] = jnp.where(i < nt - 1, dch_ref[...], 0.0)
        ex_ref[0:H, :] = jnp.where(i > 0, xh_ref[...], 0.0)
        ex_ref[H:, :] = x_ref[...]

        @pl.when(first)
        def _():
            gdw_ref[...] = jnp.zeros_like(gdw_ref)

        for c0 in range(0, D, cc):
            cs = slice(c0, c0 + cc)
            dcv = dc_ref[:, cs]
            acc = jnp.zeros((tm, cc), F32)
            for k in range(K):
                acc = acc + edc_ref[K - 1 - k:K - 1 - k + tm, cs] * dw_ref[k:k + 1, cs]
                gdw_ref[k:k + 1, cs] += _colsum(dcv * ex_ref[off + k:off + k + tm, cs])
            dglu_ref[:, cs] = acc
        dglu = dglu_ref[...]
        a = p_ref[:, 0:D]
        sg = _sigmoid(p_ref[:, D:2 * D])
        da = dglu * sg
        dgate = dglu * a * (sg * (1.0 - sg))
        dp_ref[:, 0:D] = da.astype(BF16)
        dp_ref[:, D:2 * D] = dgate.astype(BF16)
        _acc(gbin_ref.at[:, 0:D], first, _colsum(da))
        _acc(gbin_ref.at[:, D:2 * D], first, _colsum(dgate))

    return pl.pallas_call(
        body, name="dwconv_bwd", grid=(nt,),
        out_shape=[jax.ShapeDtypeStruct((T, 2 * D), BF16), jax.ShapeDtypeStruct((Kp, D), F32),
                   jax.ShapeDtypeStruct((1, 2 * D), F32)],
        in_specs=[_rows(tm, D), pl.BlockSpec((H, D), lambda i: (jnp.minimum((i + 1) * r, last), 0)),
                  _rows(tm, D), pl.BlockSpec((H, D), lambda i: (jnp.maximum(i * r - 1, 0), 0)),
                  _rows(tm, 2 * D), _full((K, D))],
        out_specs=[_rows(tm, 2 * D), _full((Kp, D)), _full((1, 2 * D))],
        scratch_shapes=[pltpu.VMEM((tm + H, D), F32), pltpu.VMEM((tm + H, D), F32), pltpu.VMEM((tm, D), F32)],
        compiler_params=_params("arbitrary"),
    )(dc, dc, glu, glu, p, dw)


def _dx_proj(name, dz, dy, wb, alpha, tm, deps=()):
    T, D = dz.shape
    nb, _, Ns = wb.shape
    N = nb * Ns

    def body(dz_ref, dy_ref, w_ref, dx_ref):
        dx = alpha * dz_ref[...]
        for j in range(nb):
            dx = dx + _mm_nt(dy_ref[:, j * Ns:(j + 1) * Ns], w_ref[j])
        dx_ref[...] = dx

    return _tied_call(
        body, deps, name=name, grid=(T // tm,),
        out_shape=jax.ShapeDtypeStruct((T, D), F32),
        in_specs=[_rows(tm, D), _rows(tm, N), _full(wb.shape)],
        out_specs=_rows(tm, D),
        compiler_params=_params("parallel"),
    )(dz, dy, wb)


def _pool_bwd(dz, x, pw, scale, alpha, tm):
    T, D = x.shape
    G = len(POOL_WINDOWS)
    Dg = D // G
    H = POOL_HALO
    r = tm // H
    last = T // H - 1
    nt = T // tm
    n_ext = tm + H

    def body(dz_ref, dzh_ref, x_ref, xh_ref, pw_ref, sc_ref, dx_ref, gpw_ref, gsc_ref, edz_ref, ex_ref):
        i = pl.program_id(0)
        first = i == 0
        edz_ref[0:tm, :] = dz_ref[...]
        edz_ref[tm:, :] = jnp.where(i < nt - 1, dzh_ref[...], 0.0)
        ex_ref[0:H, :] = jnp.where(i > 0, xh_ref[...], 0.0)
        ex_ref[H:, :] = x_ref[...]
        row = i * tm + lax.broadcasted_iota(jnp.int32, (tm, 1), 0)
        rowf = (row + 1).astype(F32)
        erow = i * tm + lax.broadcasted_iota(jnp.int32, (n_ext, 1), 0)
        erowf = (erow + 1).astype(F32)
        for gi, w in enumerate(POOL_WINDOWS):
            sl = slice(gi * Dg, (gi + 1) * Dg)
            s = ex_ref[:, sl]
            k = 1
            while k < w:
                s = s + pltpu.roll(s, k, 0)
                k *= 2
            xg = x_ref[:, sl]
            d = (s[H:, :] * (1.0 / jnp.minimum(rowf, float(w))) - xg).astype(BF16)
            wg = pw_ref[gi]
            premix = _mm(d, wg)
            dzg = dz_ref[:, sl]
            _acc(gsc_ref.at[:, sl], first, _colsum(dzg * premix))
            dpre = edz_ref[:, sl] * sc_ref[:, sl]
            dpre_b = dpre.astype(BF16)
            _acc(gpw_ref.at[gi], first, _mm_tn(d, dpre_b[0:tm, :]))
            dd = _mm_nt(dpre_b, wg)
            e = dd * (1.0 / jnp.minimum(erowf, float(w)))
            k = 1
            while k < w:
                e = e + pltpu.roll(e, n_ext - k, 0)
                k *= 2
            dx_ref[:, sl] = alpha * dzg + e[0:tm, :] - dd[0:tm, :]

    return pl.pallas_call(
        body, name="pool_bwd", grid=(nt,),
        out_shape=[jax.ShapeDtypeStruct((T, D), F32), jax.ShapeDtypeStruct((G, Dg, Dg), F32),
                   jax.ShapeDtypeStruct((1, D), F32)],
        in_specs=[_rows(tm, D), pl.BlockSpec((H, D), lambda i: (jnp.minimum((i + 1) * r, last), 0)),
                  _rows(tm, D), pl.BlockSpec((H, D), lambda i: (jnp.maximum(i * r - 1, 0), 0)),
                  _full(pw.shape), _full((1, D))],
        out_specs=[_rows(tm, D), _full((G, Dg, Dg)), _full((1, D))],
        scratch_shapes=[pltpu.VMEM((n_ext, D), F32), pltpu.VMEM((n_ext, D), F32)],
        compiler_params=_params("arbitrary"),
    )(dz, dz, x, x, pw, scale)


def _adamw(name, recv, w, m, v, tm, layer=None, prev=None):
    R, C = w.shape[-2:]
    c1 = 1.0 - ADAM_B1 ** ADAM_STEP
    c2 = 1.0 - ADAM_B2 ** ADAM_STEP
    if layer is None:
        spec = _rows(tm, C)
    else:
        spec = pl.BlockSpec((None, tm, C), lambda i: (layer, i, 0))
    prev = list(prev) if prev is not None else []

    def body(r_ref, w_ref, m_ref, v_ref, *rest):
        g_ref, d_ref, nm_ref, nv_ref = rest[len(prev):]
        g = r_ref[0].astype(F32)
        for s in range(1, N_DEV):
            g = g + r_ref[s].astype(F32)
        m1 = ADAM_B1 * m_ref[...] + (1.0 - ADAM_B1) * g
        v1 = ADAM_B2 * v_ref[...] + (1.0 - ADAM_B2) * (g * g)
        m_hat = m1 / c1
        v_hat = v1 / c2
        g_ref[...] = g
        d_ref[...] = -ADAM_LR * (m_hat / (jnp.sqrt(v_hat) + ADAM_EPS) + ADAM_WD * w_ref[...])
        nm_ref[...] = m1
        nv_ref[...] = v1

    return pl.pallas_call(
        body, name=name, grid=(R // tm,),
        out_shape=[jax.ShapeDtypeStruct(w.shape, F32)] * 4,
        in_specs=[pl.BlockSpec((N_DEV, tm, C), lambda i: (0, i, 0))] + [spec] * 3
        + [pl.BlockSpec(memory_space=pl.ANY)] * len(prev),
        out_specs=[spec] * 4,
        input_output_aliases={4 + j: j for j in range(len(prev))},
        compiler_params=_params("parallel"),
    )(recv, w, m, v, *prev)


def _pad_rows(a, rows):
    return jnp.pad(a, ((0, rows - a.shape[0]), (0, 0)))


def kernel(x, pool_w, pool_scale, conv_w_in, conv_b_in, conv_dw, conv_dw_b, conv_ln_g, conv_ln_b, conv_w_out, conv_b_out, mix_ln_g, mix_ln_b, mlp_w1, mlp_b1, mlp_w2, mlp_b2, mlp_ln_g, mlp_ln_b, loss_target, m_pool_w, m_pool_scale, m_conv_w_in, m_conv_b_in, m_conv_dw, m_conv_dw_b, m_conv_ln_g, m_conv_ln_b, m_conv_w_out, m_conv_b_out, m_mix_ln_g, m_mix_ln_b, m_mlp_w1, m_mlp_b1, m_mlp_w2, m_mlp_b2, m_mlp_ln_g, m_mlp_ln_b, v_pool_w, v_pool_scale, v_conv_w_in, v_conv_b_in, v_conv_dw, v_conv_dw_b, v_conv_ln_g, v_conv_ln_b, v_conv_w_out, v_conv_b_out, v_mix_ln_g, v_mix_ln_b, v_mlp_w1, v_mlp_b1, v_mlp_w2, v_mlp_b2, v_mlp_ln_g, v_mlp_ln_b):
    _, T, D = x.shape
    L = mlp_w1.shape[0]
    assert L == 2 and pool_w.shape[0] == 1 and conv_w_in.shape[0] == 1
    G = pool_w.shape[1]
    Dg = D // G
    Fd = mlp_b1.shape[1]
    Fs = Fd // N_DEV
    Kc = conv_dw.shape[1]
    Dc = D // N_DEV
    alpha = float((2.0 * L) ** 0.25)
    x2d, tgt = x[0], loss_target[0]

    tm = _tile(T, 512)
    tm_wide = _tile(T, 256)
    tm_conv = _tile(T, 256)
    tm_wg = _tile(T, 1024)

    def pack_sh(dw, dwb, lg, lb, bo, bi):
        rows = jnp.concatenate([dw[0], dwb, lg, lb, bo, bi.reshape(2, Dc)], axis=0)
        return _pad_rows(rows, SH_ROWS)

    SH_ROWS = -(-(Kc + 6) // 8) * 8
    def pack_rep(ps, mg, mb, b1, b2, lg, lb):
        rows = jnp.concatenate([ps, mg, mb, b1.reshape(L * Fd // D, D), b2, lg, lb], axis=0)
        return _pad_rows(rows, REP_ROWS)

    n_rep = 1 + 2 * L + L * Fd // D + 3 * L
    REP_ROWS = -(-n_rep // 8) * 8

    w_sh = pack_sh(conv_dw, conv_dw_b, conv_ln_g, conv_ln_b, conv_b_out, conv_b_in)
    m_sh = pack_sh(m_conv_dw, m_conv_dw_b, m_conv_ln_g, m_conv_ln_b, m_conv_b_out, m_conv_b_in)
    v_sh = pack_sh(v_conv_dw, v_conv_dw_b, v_conv_ln_g, v_conv_ln_b, v_conv_b_out, v_conv_b_in)
    w_rep = pack_rep(pool_scale, mix_ln_g, mix_ln_b, mlp_b1, mlp_b2, mlp_ln_g, mlp_ln_b)
    m_rep = pack_rep(m_pool_scale, m_mix_ln_g, m_mix_ln_b, m_mlp_b1, m_mlp_b2, m_mlp_ln_g, m_mlp_ln_b)
    v_rep = pack_rep(v_pool_scale, v_mix_ln_g, v_mix_ln_b, v_mlp_b1, v_mlp_b2, v_mlp_ln_g, v_mlp_ln_b)

    groups = [[pool_w[0]], [mlp_w1[0]], [mlp_w2[0]], [conv_w_in[0], conv_w_out[0]], [mlp_w1[1]], [mlp_w2[1]]]
    handles, tokens = [], []
    for i, grp in enumerate(groups):
        tie = tokens[-1][0, 0] if tokens else 0.0
        srcs = [(a + tie).astype(BF16) for a in grp] + ([w_sh] if i == 0 else [])
        h, tk = _xstart("gather_%d" % i, srcs, [False] * len(srcs))
        handles.append(h)
        tokens.append(tk)
    pw_all, sh_all = _xwait(handles[0], tokens[-1])
    pw = pw_all.transpose(1, 0, 2, 3).reshape(G, Dg, Dg)
    dw_full = sh_all[:, 0:Kc].transpose(1, 0, 2).reshape(Kc, D)

    def sh_row(i):
        return sh_all[:, i].reshape(1, D)

    dwb_full, cg_full, cb_full, bout_full = (sh_row(Kc + i) for i in range(4))
    bin_full = sh_all[:, Kc + 4:Kc + 6].reshape(1, 2 * D)

    h0, z_m0 = _pool_fwd(x2d, pw, pool_scale, mix_ln_g[0:1], mix_ln_b[0:1], alpha, tm, tuple(tokens))
    (w1b0,) = _xwait(handles[1], h0)
    u0, a0 = _mlp_up("mlp_up0", h0, w1b0, mlp_b1[0:1], tm_wide)
    (w2b0,) = _xwait(handles[2], a0)
    x1, z_f0 = _proj_ln("mlp_down0", a0, w2b0.reshape(Fd, D), mlp_b2[0:1], h0, mlp_ln_g[0:1], mlp_ln_b[0:1],
                        alpha, tm)
    win_b, wout_all = _xwait(handles[3], x1)
    w_out = wout_all.reshape(D, D)
    p, glu = _conv_in(x1, win_b, bin_full, tm)
    cz, s = _dwconv_fwd(glu, dw_full, dwb_full, cg_full, cb_full, tm_conv)
    h1, z_m1 = _proj_ln("conv_out", s, w_out, bout_full, x1, mix_ln_g[1:2], mix_ln_b[1:2], alpha, tm)
    (w1b1,) = _xwait(handles[4], h1)
    u1, a1 = _mlp_up("mlp_up1", h1, w1b1, mlp_b1[1:2], tm_wide)
    (w2b1,) = _xwait(handles[5], a1)
    x2, z_f1 = _proj_ln("mlp_down1", a1, w2b1.reshape(Fd, D), mlp_b2[1:2], h1, mlp_ln_g[1:2], mlp_ln_b[1:2],
                        alpha, tm)

    dz, dzb, g_fg1, g_fb1, g_b2_1, loss_cols = _ln_bwd_call("ln_bwd_f1", x2, tgt, z_f1, mlp_ln_g[1:2], tm)
    gw2_1 = _wgrad("gw2_1", a1, dzb, False, 4, 1, tm_wg)
    e_w2_1, tk = _xstart("grads_w2_1", [gw2_1], [True])
    du, dx, g_b1_1 = _mlp_bwd("mlp_bwd1", dz, dzb, u1, w2b1, w1b1, alpha, tm_wide, (tk,))
    gw1_1 = _wgrad("gw1_1", h1, du, True, 1, 2, tm_wg)
    e_w1_1, tk = _xstart("grads_w1_1", [gw1_1], [True])
    dz, dzb, g_mg1, g_mb1, g_bout, = _ln_bwd_call("ln_bwd_m1", dx, None, z_m1, mix_ln_g[1:2], tm, (tk,))
    gwout = _wgrad("gw_out", s, dzb, False, 1, 1, tm_wg)
    dc, g_cg, g_cb, g_dwb = _conv_out_bwd(dzb, w_out, cz, cg_full, cb_full, tm)
    dp, g_dw, g_bin = _dwconv_bwd(dc, glu, p, dw_full, tm_conv)
    gwin = _wgrad("gw_in", x1, dp, True, 1, 2, tm_wg)
    e_conv, tk = _xstart("grads_conv", [gwin, gwout], [True, True])
    dx = _dx_proj("conv_in_bwd", dz, dp, win_b, alpha, tm, (tk,))
    dz, dzb, g_fg0, g_fb0, g_b2_0 = _ln_bwd_call("ln_bwd_f0", dx, None, z_f0, mlp_ln_g[0:1], tm)
    gw2_0 = _wgrad("gw2_0", a0, dzb, False, 4, 1, tm_wg)
    e_w2_0, tk = _xstart("grads_w2_0", [gw2_0], [True])
    du, dx, g_b1_0 = _mlp_bwd("mlp_bwd0", dz, dzb, u0, w2b0, w1b0, alpha, tm_wide, (tk,))
    gw1_0 = _wgrad("gw1_0", h0, du, True, 1, 2, tm_wg)
    e_w1_0, tk = _xstart("grads_w1_0", [gw1_0], [True])
    dz, dzb, g_mg0, g_mb0, _ = _ln_bwd_call("ln_bwd_m0", dx, None, z_m0, mix_ln_g[0:1], tm, (tk,))
    grad_x, g_pw, g_ps = _pool_bwd(dz, x2d, pw, pool_scale, alpha, tm)

    loss = lax.psum(0.5 / D * jnp.sum(loss_cols), MESH_AXES)

    gpw_b = g_pw.reshape(G, N_DEV, Dg // N_DEV, Dg).transpose(1, 0, 2, 3).astype(BF16)

    def to_dev(vec, rows):
        return vec.reshape(rows, N_DEV, Dc).transpose(1, 0, 2)

    g_sh = jnp.concatenate(
        [to_dev(g_dw[0:Kc], Kc), to_dev(g_dwb, 1), to_dev(g_cg, 1), to_dev(g_cb, 1), to_dev(g_bout, 1),
         g_bin.reshape(N_DEV, 2, Dc), jnp.zeros((N_DEV, SH_ROWS - Kc - 6, Dc), F32)], axis=1)
    g_rep = _pad_rows(jnp.concatenate(
        [g_ps, g_mg0, g_mg1, g_mb0, g_mb1, g_b1_0.reshape(Fd // D, D), g_b1_1.reshape(Fd // D, D),
         g_b2_0, g_b2_1, g_fg0, g_fg1, g_fb0, g_fb1], axis=0), REP_ROWS)

    e_small, tk = _xstart("grads_small", [gpw_b, g_sh, g_rep], [True, True, False])

    def upd(name, recv, w, m, v):
        shape = w.shape
        C = shape[-1]
        R = w.size // C
        outs = _adamw(name, recv.reshape(N_DEV, R, C), w.reshape(R, C), m.reshape(R, C), v.reshape(R, C), _tile(R, 256))
        return [o.reshape(shape) for o in outs]

    def upd_layer(name, recv, w, m, v, layer, prev):
        return _adamw(name, recv, w, m, v, _tile(w.shape[1], 256), layer, prev)

    (r_w2_1,) = _xwait(e_w2_1, tk)
    o_w2 = upd_layer("adam_w2_1", r_w2_1, mlp_w2, m_mlp_w2, v_mlp_w2, 1, None)
    (r_w1_1,) = _xwait(e_w1_1, o_w2[0])
    o_w1 = upd_layer("adam_w1_1", r_w1_1, mlp_w1, m_mlp_w1, v_mlp_w1, 1, None)
    r_win, r_wout = _xwait(e_conv, o_w1[0])
    o_win = upd("adam_w_in", r_win, conv_w_in, m_conv_w_in, v_conv_w_in)
    o_wout = upd("adam_w_out", r_wout, conv_w_out, m_conv_w_out, v_conv_w_out)
    (r_w2_0,) = _xwait(e_w2_0, o_wout[0])
    o_w2 = upd_layer("adam_w2_0", r_w2_0, mlp_w2, m_mlp_w2, v_mlp_w2, 0, o_w2)
    (r_w1_0,) = _xwait(e_w1_0, o_w2[0])
    o_w1 = upd_layer("adam_w1_0", r_w1_0, mlp_w1, m_mlp_w1, v_mlp_w1, 0, o_w1)
    r_pw, r_sh, r_rep = _xwait(e_small, o_w1[0])
    o_pw = upd("adam_pool_w", r_pw, pool_w, m_pool_w, v_pool_w)
    o_sh = upd("adam_conv_vec", r_sh, w_sh, m_sh, v_sh)
    o_rep = upd("adam_replicated", r_rep, w_rep, m_rep, v_rep)

    def unpack_sh(a):
        return (a[0:Kc][None], a[Kc:Kc + 1], a[Kc + 1:Kc + 2], a[Kc + 2:Kc + 3], a[Kc + 3:Kc + 4],
                a[Kc + 4:Kc + 6].reshape(1, 2 * Dc))

    def unpack_rep(a):
        o = 0
        out = []
        for rows, shape in ((1, (1, D)), (L, (L, D)), (L, (L, D)), (L * Fd // D, (L, Fd)), (L, (L, D)), (L, (L, D)),
                            (L, (L, D))):
            out.append(a[o:o + rows].reshape(shape))
            o += rows
        return out

    results = []
    for kind in range(4):
        dwv, dwb, lg, lb, bo, bi = unpack_sh(o_sh[kind])
        ps, mg, mb, b1, b2, fg, fb = unpack_rep(o_rep[kind])
        results.append([o_pw[kind], ps, o_win[kind], bi, dwv, dwb, lg, lb, o_wout[kind], bo, mg, mb,
                        o_w1[kind], b1, o_w2[kind], b2, fg, fb])
    return (loss, grad_x[None], *results[0], *results[1], *results[2], *results[3])
```

```python
import jax
import jax.numpy as jnp
from jax import lax
from jax.experimental import pallas as pl
from jax.experimental.pallas import tpu as pltpu

N_DEV = 8
MESH_AXES = ("x", "y", "c")
POOL_WINDOWS = (2, 4, 8, 16)
POOL_HALO = 16
CONV_HALO = 32
LN_EPS = 1e-5
ADAM_LR = 0.001
ADAM_B1 = 0.9
ADAM_B2 = 0.999
ADAM_EPS = 1e-08
ADAM_WD = 0.01
ADAM_STEP = 10
VMEM_LIMIT = 56 * 1024 * 1024

F32 = jnp.float32
BF16 = jnp.bfloat16


def _mm(a, b):
    return lax.dot_general(a, b, (((1,), (0,)), ((), ())), preferred_element_type=F32)


def _mm_nt(a, b):
    return lax.dot_general(a, b, (((1,), (1,)), ((), ())), preferred_element_type=F32)


def _mm_tn(a, b):
    return lax.dot_general(a, b, (((0,), (0,)), ((), ())), preferred_element_type=F32)


def _tile(n, pref):
    t = min(n, pref)
    assert n % t == 0, (n, pref)
    return t


def _params(*sem):
    return pltpu.CompilerParams(dimension_semantics=sem, vmem_limit_bytes=VMEM_LIMIT)


def _full(shape):
    nd = len(shape)
    return pl.BlockSpec(shape, lambda *_: (0,) * nd)


def _rows(tm, d):
    return pl.BlockSpec((tm, d), lambda i: (i, 0))


def _ln_stats(z):
    mu = jnp.mean(z, axis=-1, keepdims=True)
    zc = z - mu
    var = jnp.mean(zc * zc, axis=-1, keepdims=True)
    rstd = lax.rsqrt(var + LN_EPS)
    return zc * rstd, rstd


def _ln_fwd(z, g, b):
    xhat, _ = _ln_stats(z)
    return xhat * g + b


def _ln_bwd(dy, z, g):
    xhat, rstd = _ln_stats(z)
    dxh = dy * g
    m1 = jnp.mean(dxh, axis=-1, keepdims=True)
    m2 = jnp.mean(dxh * xhat, axis=-1, keepdims=True)
    return rstd * (dxh - m1 - xhat * m2), xhat


def _colsum(v):
    return jnp.sum(v, axis=0, keepdims=True)


def _sigmoid(v):
    return 1.0 / (1.0 + jnp.exp(-v))


def _acc(ref, first, val):
    @pl.when(first)
    def _():
        ref[...] = val

    @pl.when(jnp.logical_not(first))
    def _():
        ref[...] += val


_HBM = pl.BlockSpec(memory_space=pltpu.HBM)
_SEM = pl.BlockSpec(memory_space=pltpu.SEMAPHORE)
_EFFECT = pltpu.SideEffectType.DATAFLOW_SIDE_EFFECTING


def _peers():
    x, y, c = (lax.axis_index(a) for a in MESH_AXES)
    out = []
    for d in range(1, N_DEV):
        px = (x + ((d >> 2) & 1)) % 2
        py = (y + ((d >> 1) & 1)) % 2
        pc = (c + (d & 1)) % 2
        out.append((d - 1, (px, py, pc), 4 * px + 2 * py + pc))
    return 4 * x + 2 * y + c, out


def _remote_copies(src_refs, land_refs, scatter, send_sems, recv_sems):
    me, peers = _peers()
    copies = []
    for i, pos, pid in peers:
        for k, (src, land) in enumerate(zip(src_refs, land_refs)):
            copies.append(pltpu.make_async_remote_copy(
                src_ref=src.at[pid] if scatter[k] else src, dst_ref=land.at[me],
                send_sem=send_sems.at[k * (N_DEV - 1) + i], recv_sem=recv_sems.at[k * (N_DEV - 1) + i],
                device_id=pos, device_id_type=pl.DeviceIdType.MESH))
    return copies


def _xstart(name, srcs, scatter):
    n = len(srcs)
    me = 4 * lax.axis_index("x") + 2 * lax.axis_index("y") + lax.axis_index("c")
    lands = []
    for s, sc in zip(srcs, scatter):
        own = lax.dynamic_index_in_dim(s, me, 0, keepdims=True) if sc else s[None]
        shape = s.shape if sc else (N_DEV,) + s.shape
        lands.append(lax.dynamic_update_slice(lax.empty(shape, s.dtype), own, (me,) + (0,) * (len(shape) - 1)))

    def body(*refs):
        src_refs, land_refs = refs[:n], refs[n:2 * n]
        send_sems, recv_sems = refs[2 * n], refs[2 * n + 1]
        token = refs[-1]
        for cp in _remote_copies(src_refs, land_refs, scatter, send_sems, recv_sems):
            cp.start()
        token[...] = jnp.zeros_like(token)

    outs = pl.pallas_call(
        body, name=name,
        out_shape=(pltpu.SemaphoreType.DMA((n * (N_DEV - 1),)), pltpu.SemaphoreType.DMA((n * (N_DEV - 1),)),
                   *[pltpu.HBM(a.shape, a.dtype) for a in srcs + lands], jax.ShapeDtypeStruct((8, 128), F32)),
        in_specs=(_HBM,) * (2 * n),
        out_specs=(_SEM, _SEM) + (_HBM,) * (2 * n) + (pl.BlockSpec(memory_space=pltpu.VMEM),),
        input_output_aliases={i: 2 + i for i in range(2 * n)},
        compiler_params=pltpu.CompilerParams(has_side_effects=_EFFECT),
    )(*[pltpu.with_memory_space_constraint(a, pltpu.HBM) for a in srcs + lands])
    return (name, scatter, outs[0], outs[1], outs[2:2 + n], outs[2 + n:2 + 2 * n]), outs[-1]


def _xwait(handle, after):
    name, scatter, send_sems, recv_sems, srcs, lands = handle
    n = len(srcs)

    def body(*refs):
        src_refs, land_refs = refs[:n], refs[n:2 * n]
        send, recv = refs[2 * n], refs[2 * n + 1]
        copies = _remote_copies(src_refs, land_refs, scatter, send, recv)
        for cp in copies:
            cp.wait_send()
        for cp in copies:
            cp.wait_recv()

    outs = pl.pallas_call(
        body, name=name + "_wait",
        out_shape=tuple(pltpu.HBM(a.shape, a.dtype) for a in (*srcs, *lands)),
        in_specs=(_HBM,) * (2 * n) + (_SEM, _SEM, pl.BlockSpec(memory_space=pl.ANY)),
        out_specs=(_HBM,) * (2 * n),
        input_output_aliases={i: i for i in range(2 * n)},
        compiler_params=pltpu.CompilerParams(has_side_effects=_EFFECT),
    )(*srcs, *lands, send_sems, recv_sems, after)
    return outs[n:]


def _tied_call(body, deps, in_specs, **kw):
    nd = len(deps)

    def tied_body(*refs):
        body(*refs[nd:])

    call = pl.pallas_call(tied_body, in_specs=[pl.BlockSpec(memory_space=pl.ANY)] * nd + list(in_specs), **kw)
    return lambda *args: call(*deps, *args)


def _pool_fwd(x, pw, scale, g, b, alpha, tm, deps=()):
    T, D = x.shape
    G = len(POOL_WINDOWS)
    Dg = D // G
    H = POOL_HALO
    r = tm // H

    def body(x_ref, xh_ref, pw_ref, sc_ref, g_ref, b_ref, y_ref, z_ref, ext_ref):
        i = pl.program_id(0)
        ext_ref[0:H, :] = jnp.where(i > 0, xh_ref[...], 0.0)
        ext_ref[H:, :] = x_ref[...]
        row = i * tm + lax.broadcasted_iota(jnp.int32, (tm, 1), 0)
        rowf = (row + 1).astype(F32)
        for gi, w in enumerate(POOL_WINDOWS):
            sl = slice(gi * Dg, (gi + 1) * Dg)
            s = ext_ref[:, sl]
            k = 1
            while k < w:
                s = s + pltpu.roll(s, k, 0)
                k *= 2
            inv = 1.0 / jnp.minimum(rowf, float(w))
            xg = x_ref[:, sl]
            d = s[H:, :] * inv - xg
            mix = _mm(d.astype(BF16), pw_ref[gi]) * sc_ref[:, sl]
            z_ref[:, sl] = alpha * xg + mix
        y_ref[...] = _ln_fwd(z_ref[...], g_ref[...], b_ref[...])

    return _tied_call(
        body, deps, name="pool_fwd", grid=(T // tm,),
        out_shape=[jax.ShapeDtypeStruct((T, D), F32)] * 2,
        in_specs=[_rows(tm, D), pl.BlockSpec((H, D), lambda i: (jnp.maximum(i * r - 1, 0), 0)),
                  _full(pw.shape), _full((1, D)), _full((1, D)), _full((1, D))],
        out_specs=[_rows(tm, D)] * 2,
        scratch_shapes=[pltpu.VMEM((tm + H, D), F32)],
        compiler_params=_params("parallel"),
    )(x, x, pw, scale, g, b)


def _mlp_up(name, h, w1b, b1, tm):
    T, D = h.shape
    nb, _, Fs = w1b.shape
    Fd = nb * Fs

    def body(h_ref, w_ref, b_ref, u_ref, a_ref, hb_ref):
        @pl.when(pl.program_id(1) == 0)
        def _():
            hb_ref[...] = h_ref[...].astype(BF16)

        u = _mm(hb_ref[...], w_ref[...]) + b_ref[...]
        u_ref[...] = u
        rl = jnp.maximum(u, 0.0)
        a_ref[...] = (rl * rl).astype(BF16)

    return pl.pallas_call(
        body, name=name, grid=(T // tm, nb),
        out_shape=[jax.ShapeDtypeStruct((T, Fd), F32), jax.ShapeDtypeStruct((T, Fd), BF16)],
        in_specs=[pl.BlockSpec((tm, D), lambda i, j: (i, 0)), pl.BlockSpec((None, D, Fs), lambda i, j: (j, 0, 0)),
                  pl.BlockSpec((1, Fs), lambda i, j: (0, j))],
        out_specs=[pl.BlockSpec((tm, Fs), lambda i, j: (i, j))] * 2,
        scratch_shapes=[pltpu.VMEM((tm, D), BF16)],
        compiler_params=_params("parallel", "arbitrary"),
    )(h, w1b, b1)


def _proj_ln(name, a, w, bias, res, g, b, alpha, tm):
    T, K = a.shape
    D = w.shape[1]

    def body(a_ref, w_ref, bias_ref, res_ref, g_ref, b_ref, y_ref, z_ref):
        z = alpha * res_ref[...] + _mm(a_ref[...], w_ref[...]) + bias_ref[...]
        z_ref[...] = z
        y_ref[...] = _ln_fwd(z, g_ref[...], b_ref[...])

    return pl.pallas_call(
        body, name=name, grid=(T // tm,),
        out_shape=[jax.ShapeDtypeStruct((T, D), F32)] * 2,
        in_specs=[_rows(tm, K), _full((K, D)), _full((1, D)), _rows(tm, D), _full((1, D)), _full((1, D))],
        out_specs=[_rows(tm, D)] * 2,
        compiler_params=_params("parallel"),
    )(a, w, bias, res, g, b)


def _conv_in(x, wb, b_in, tm):
    T, D = x.shape
    nb, _, Ns = wb.shape
    half = nb // 2

    def body(x_ref, w_ref, b_ref, p_ref, glu_ref):
        xb = x_ref[...].astype(BF16)
        for j in range(half):
            sa = slice(j * Ns, (j + 1) * Ns)
            sg = slice(D + j * Ns, D + (j + 1) * Ns)
            a = _mm(xb, w_ref[j]) + b_ref[:, sa]
            gate = _mm(xb, w_ref[half + j]) + b_ref[:, sg]
            p_ref[:, sa] = a
            p_ref[:, sg] = gate
            glu_ref[:, sa] = a * _sigmoid(gate)

    return pl.pallas_call(
        body, name="conv_in", grid=(T // tm,),
        out_shape=[jax.ShapeDtypeStruct((T, 2 * D), F32), jax.ShapeDtypeStruct((T, D), F32)],
        in_specs=[_rows(tm, D), _full(wb.shape), _full((1, 2 * D))],
        out_specs=[_rows(tm, 2 * D), _rows(tm, D)],
        compiler_params=_params("parallel"),
    )(x, wb, b_in)


def _lane_chunk(d):
    return 128 if d % 128 == 0 else d


SUBLANES = 8


def _row_chunk(tm, pref):
    return pref if tm % pref == 0 else tm


def _slabs(ref, base, n_taps, r0, rows, cs):
    out = []
    for r in range(min(SUBLANES, n_taps)):
        nq = (n_taps - 1 - r) // SUBLANES + 1
        lo = base + r + r0
        slab = ref[lo:lo + rows + SUBLANES * (nq - 1), cs]
        out.append((slab, [(q, SUBLANES * q + r) for q in range(nq)]))
    return out


def _dwconv_fwd(glu, dw, dw_b, g, b, tm):
    T, D = glu.shape
    K = dw.shape[0]
    H = CONV_HALO
    off = H - (K - 1)
    r = tm // H
    cc = _lane_chunk(D)

    def body(x_ref, xh_ref, dw_ref, dwb_ref, g_ref, b_ref, cz_ref, s_ref, ext_ref, sh_ref):
        i = pl.program_id(0)
        ext_ref[0:H, :] = jnp.where(i > 0, xh_ref[...], 0.0)
        ext_ref[H:, :] = x_ref[...]
        for ph in range(min(SUBLANES, K)):
            n = tm + SUBLANES * ((K - 1 - ph) // SUBLANES)
            sh_ref[ph, 0:n, :] = ext_ref[off + ph:off + ph + n, :]
        rc = _row_chunk(tm, 128)
        for c0 in range(0, D, cc):
            cs = slice(c0, c0 + cc)
            for r0 in range(0, tm, rc):
                acc = jnp.zeros((rc, cc), F32) + dwb_ref[:, cs]
                for k in range(K):
                    q, ph = divmod(k, SUBLANES)
                    acc = acc + sh_ref[ph, SUBLANES * q + r0:SUBLANES * q + r0 + rc, cs] * dw_ref[k:k + 1, cs]
                cz_ref[r0:r0 + rc, cs] = acc
        ln = _ln_fwd(cz_ref[...], g_ref[...], b_ref[...])
        s_ref[...] = (ln * _sigmoid(ln)).astype(BF16)

    return pl.pallas_call(
        body, name="dwconv_fwd", grid=(T // tm,),
        out_shape=[jax.ShapeDtypeStruct((T, D), F32), jax.ShapeDtypeStruct((T, D), BF16)],
        in_specs=[_rows(tm, D), pl.BlockSpec((H, D), lambda i: (jnp.maximum(i * r - 1, 0), 0)),
                  _full((K, D)), _full((1, D)), _full((1, D)), _full((1, D))],
        out_specs=[_rows(tm, D)] * 2,
        scratch_shapes=[pltpu.VMEM((tm + H, D), F32),
                        pltpu.VMEM((SUBLANES, tm + SUBLANES * ((K - 1) // SUBLANES), D), F32)],
        compiler_params=_params("parallel"),
    )(glu, glu, dw, dw_b, g, b)


def _ln_bwd_call(name, dy_or_y, target, z, g, tm, deps=()):
    T, D = z.shape
    with_loss = target is not None

    def body(*refs):
        if with_loss:
            y_ref, t_ref, z_ref, g_ref, dz_ref, dzb_ref, gg_ref, gb_ref, sdz_ref, loss_ref = refs
            e = y_ref[...] - t_ref[...]
            dy = e * (1.0 / D)
        else:
            y_ref, z_ref, g_ref, dz_ref, dzb_ref, gg_ref, gb_ref, sdz_ref = refs
            dy = y_ref[...]
        first = pl.program_id(0) == 0
        dz, xhat = _ln_bwd(dy, z_ref[...], g_ref[...])
        dz_ref[...] = dz
        dzb_ref[...] = dz.astype(BF16)
        _acc(gg_ref, first, _colsum(dy * xhat))
        _acc(gb_ref, first, _colsum(dy))
        _acc(sdz_ref, first, _colsum(dz))
        if with_loss:
            _acc(loss_ref, first, _colsum(e * e))

    n_acc = 4 if with_loss else 3
    ins = [dy_or_y] + ([target] if with_loss else []) + [z, g]
    in_specs = [_rows(tm, D)] * (len(ins) - 1) + [_full((1, D))]
    return _tied_call(
        body, deps, name=name, grid=(T // tm,),
        out_shape=[jax.ShapeDtypeStruct((T, D), F32), jax.ShapeDtypeStruct((T, D), BF16)]
        + [jax.ShapeDtypeStruct((1, D), F32)] * n_acc,
        in_specs=in_specs,
        out_specs=[_rows(tm, D)] * 2 + [_full((1, D))] * n_acc,
        compiler_params=_params("arbitrary"),
    )(*ins)


def _mlp_bwd(name, dz, dzb, u, w2b, w1b, alpha, tm, deps=()):
    T, D = dz.shape
    nb, Fs, _ = w2b.shape
    Fd = nb * Fs

    def body(dz_ref, dzb_ref, u_ref, w2_ref, w1_ref, du_ref, dx_ref, gb1_ref):
        i, j = pl.program_id(0), pl.program_id(1)
        da = _mm_nt(dzb_ref[...], w2_ref[...])
        du = da * (2.0 * jnp.maximum(u_ref[...], 0.0))
        dub = du.astype(BF16)
        du_ref[...] = dub
        _acc(gb1_ref.at[j], i == 0, _colsum(du))
        dh = _mm_nt(dub, w1_ref[...])

        @pl.when(j == 0)
        def _():
            dx_ref[...] = alpha * dz_ref[...] + dh

        @pl.when(j > 0)
        def _():
            dx_ref[...] += dh

    du, dx, gb1 = _tied_call(
        body, deps, name=name, grid=(T // tm, nb),
        out_shape=[jax.ShapeDtypeStruct((T, Fd), BF16), jax.ShapeDtypeStruct((T, D), F32),
                   jax.ShapeDtypeStruct((nb, 1, Fs), F32)],
        in_specs=[pl.BlockSpec((tm, D), lambda i, j: (i, 0)), pl.BlockSpec((tm, D), lambda i, j: (i, 0)),
                  pl.BlockSpec((tm, Fs), lambda i, j: (i, j)), pl.BlockSpec((None, Fs, D), lambda i, j: (j, 0, 0)),
                  pl.BlockSpec((None, D, Fs), lambda i, j: (j, 0, 0))],
        out_specs=[pl.BlockSpec((tm, Fs), lambda i, j: (i, j)), pl.BlockSpec((tm, D), lambda i, j: (i, 0)),
                   pl.BlockSpec((nb, 1, Fs), lambda i, j: (0, 0, 0))],
        compiler_params=_params("arbitrary", "arbitrary"),
    )(dz, dzb, u, w2b, w1b)
    return du, dx, gb1.reshape(1, Fd)


def _wgrad(name, xm, dy, col_blocks, nk, nj, tm):
    T, K = xm.shape
    N = dy.shape[1]
    Kb, Nb = K // nk, N // nj
    nt = T // tm
    if col_blocks:
        per, Ns = N_DEV // nj, N // N_DEV
        out_shape = (N_DEV, K, Ns)
        out_spec = pl.BlockSpec((per, Kb, Ns), lambda k, j, t: (j, k, 0))
    else:
        per, Ks = N_DEV // nk, K // N_DEV
        out_shape = (N_DEV, Ks, N)
        out_spec = pl.BlockSpec((per, Ks, Nb), lambda k, j, t: (k, 0, j))

    def body(x_ref, dy_ref, o_ref, acc_ref):
        t = pl.program_id(2)
        _acc(acc_ref, t == 0, _mm_tn(x_ref[...].astype(BF16), dy_ref[...]))

        @pl.when(t == nt - 1)
        def _():
            for q in range(per):
                if col_blocks:
                    o_ref[q] = acc_ref[:, q * Ns:(q + 1) * Ns].astype(BF16)
                else:
                    o_ref[q] = acc_ref[q * Ks:(q + 1) * Ks, :].astype(BF16)

    return pl.pallas_call(
        body, name=name, grid=(nk, nj, nt),
        out_shape=jax.ShapeDtypeStruct(out_shape, BF16),
        in_specs=[pl.BlockSpec((tm, Kb), lambda k, j, t: (t, k)), pl.BlockSpec((tm, Nb), lambda k, j, t: (t, j))],
        out_specs=out_spec,
        scratch_shapes=[pltpu.VMEM((Kb, Nb), F32)],
        compiler_params=_params("parallel", "parallel", "arbitrary"),
    )(xm, dy)


def _conv_out_bwd(dzb, w_out, cz, g, b, tm):
    T, D = cz.shape

    def body(dz_ref, w_ref, cz_ref, g_ref, b_ref, dc_ref, gg_ref, gb_ref, sdc_ref):
        first = pl.program_id(0) == 0
        ds = _mm_nt(dz_ref[...], w_ref[...])
        czv = cz_ref[...]
        gv = g_ref[...]
        ln = _ln_fwd(czv, gv, b_ref[...])
        sg = _sigmoid(ln)
        dln = ds * (sg * (1.0 + ln * (1.0 - sg)))
        dc, xhat = _ln_bwd(dln, czv, gv)
        dc_ref[...] = dc
        _acc(gg_ref, first, _colsum(dln * xhat))
        _acc(gb_ref, first, _colsum(dln))
        _acc(sdc_ref, first, _colsum(dc))

    return pl.pallas_call(
        body, name="conv_out_bwd", grid=(T // tm,),
        out_shape=[jax.ShapeDtypeStruct((T, D), F32)] + [jax.ShapeDtypeStruct((1, D), F32)] * 3,
        in_specs=[_rows(tm, D), _full((D, D)), _rows(tm, D), _full((1, D)), _full((1, D))],
        out_specs=[_rows(tm, D)] + [_full((1, D))] * 3,
        compiler_params=_params("arbitrary"),
    )(dzb, w_out, cz, g, b)


def _dwconv_bwd(dc, glu, p, dw, tm):
    T, D = dc.shape
    K = dw.shape[0]
    Kp = -(-K // 8) * 8
    H = CONV_HALO
    off = H - (K - 1)
    r = tm // H
    last = T // H - 1
    nt = T // tm
    cc = _lane_chunk(D)

    def body(dc_ref, dch_ref, x_ref, xh_ref, p_ref, dw_ref, dp_ref, gdw_ref, gbin_ref, edc_ref, ex_ref, dglu_ref,
             gacc_ref):
        i = pl.program_id(0)
        first = i == 0
        edc_ref[0:tm, :] = dc_ref[...]
        edc_ref[tm:, :] = jnp.where(i < nt - 1, dch_ref[...], 0.0)
        ex_ref[0:H, :] = jnp.where(i > 0, xh_ref[...], 0.0)
        ex_ref[H:, :] = x_ref[...]

        @pl.when(first)
        def _():
            gacc_ref[...] = jnp.zeros_like(gacc_ref)

        rc = _row_chunk(tm, 64)
        for c0 in range(0, D, cc):
            cs = slice(c0, c0 + cc)
            for r0 in range(0, tm, rc):
                dcv = dc_ref[r0:r0 + rc, cs]
                acc = jnp.zeros((rc, cc), F32)
                for slab, taps in _slabs(edc_ref, 0, K, r0, rc, cs):
                    for q, m in taps:
                        acc = acc + slab[SUBLANES * q:SUBLANES * q + rc] * dw_ref[K - 1 - m:K - m, cs]
                dglu_ref[r0:r0 + rc, cs] = acc
                for slab, taps in _slabs(ex_ref, off, K, r0, rc, cs):
                    for q, k in taps:
                        part = (dcv * slab[SUBLANES * q:SUBLANES * q + rc]).reshape(rc // SUBLANES, SUBLANES, cc)
                        gacc_ref[k, :, cs] += jnp.sum(part, axis=0)

        @pl.when(i == nt - 1)
        def _():
            gdw_ref[...] = jnp.zeros_like(gdw_ref)
            gdw_ref[0:K, :] = jnp.sum(gacc_ref[...], axis=1)
        dglu = dglu_ref[...]
        a = p_ref[:, 0:D]
        sg = _sigmoid(p_ref[:, D:2 * D])
        da = dglu * sg
        dgate = dglu * a * (sg * (1.0 - sg))
        dp_ref[:, 0:D] = da.astype(BF16)
        dp_ref[:, D:2 * D] = dgate.astype(BF16)
        _acc(gbin_ref.at[:, 0:D], first, _colsum(da))
        _acc(gbin_ref.at[:, D:2 * D], first, _colsum(dgate))

    return pl.pallas_call(
        body, name="dwconv_bwd", grid=(nt,),
        out_shape=[jax.ShapeDtypeStruct((T, 2 * D), BF16), jax.ShapeDtypeStruct((Kp, D), F32),
                   jax.ShapeDtypeStruct((1, 2 * D), F32)],
        in_specs=[_rows(tm, D), pl.BlockSpec((H, D), lambda i: (jnp.minimum((i + 1) * r, last), 0)),
                  _rows(tm, D), pl.BlockSpec((H, D), lambda i: (jnp.maximum(i * r - 1, 0), 0)),
                  _rows(tm, 2 * D), _full((K, D))],
        out_specs=[_rows(tm, 2 * D), _full((Kp, D)), _full((1, 2 * D))],
        scratch_shapes=[pltpu.VMEM((tm + H, D), F32), pltpu.VMEM((tm + H, D), F32), pltpu.VMEM((tm, D), F32),
                        pltpu.VMEM((K, SUBLANES, D), F32)],
        compiler_params=_params("arbitrary"),
    )(dc, dc, glu, glu, p, dw)


def _dx_proj(name, dz, dy, wb, alpha, tm, deps=()):
    T, D = dz.shape
    nb, _, Ns = wb.shape
    N = nb * Ns

    def body(dz_ref, dy_ref, w_ref, dx_ref):
        dx = alpha * dz_ref[...]
        for j in range(nb):
            dx = dx + _mm_nt(dy_ref[:, j * Ns:(j + 1) * Ns], w_ref[j])
        dx_ref[...] = dx

    return _tied_call(
        body, deps, name=name, grid=(T // tm,),
        out_shape=jax.ShapeDtypeStruct((T, D), F32),
        in_specs=[_rows(tm, D), _rows(tm, N), _full(wb.shape)],
        out_specs=_rows(tm, D),
        compiler_params=_params("parallel"),
    )(dz, dy, wb)


def _pool_bwd(dz, x, pw, scale, alpha, tm):
    T, D = x.shape
    G = len(POOL_WINDOWS)
    Dg = D // G
    H = POOL_HALO
    r = tm // H
    last = T // H - 1
    nt = T // tm
    n_ext = tm + H

    def body(dz_ref, dzh_ref, x_ref, xh_ref, pw_ref, sc_ref, dx_ref, gpw_ref, gsc_ref, edz_ref, ex_ref):
        i = pl.program_id(0)
        first = i == 0
        edz_ref[0:tm, :] = dz_ref[...]
        edz_ref[tm:, :] = jnp.where(i < nt - 1, dzh_ref[...], 0.0)
        ex_ref[0:H, :] = jnp.where(i > 0, xh_ref[...], 0.0)
        ex_ref[H:, :] = x_ref[...]
        row = i * tm + lax.broadcasted_iota(jnp.int32, (tm, 1), 0)
        rowf = (row + 1).astype(F32)
        erow = i * tm + lax.broadcasted_iota(jnp.int32, (n_ext, 1), 0)
        erowf = (erow + 1).astype(F32)
        for gi, w in enumerate(POOL_WINDOWS):
            sl = slice(gi * Dg, (gi + 1) * Dg)
            s = ex_ref[:, sl]
            k = 1
            while k < w:
                s = s + pltpu.roll(s, k, 0)
                k *= 2
            xg = x_ref[:, sl]
            d = (s[H:, :] * (1.0 / jnp.minimum(rowf, float(w))) - xg).astype(BF16)
            wg = pw_ref[gi]
            premix = _mm(d, wg)
            dzg = dz_ref[:, sl]
            _acc(gsc_ref.at[:, sl], first, _colsum(dzg * premix))
            dpre = edz_ref[:, sl] * sc_ref[:, sl]
            dpre_b = dpre.astype(BF16)
            _acc(gpw_ref.at[gi], first, _mm_tn(d, dpre_b[0:tm, :]))
            dd = _mm_nt(dpre_b, wg)
            e = dd * (1.0 / jnp.minimum(erowf, float(w)))
            k = 1
            while k < w:
                e = e + pltpu.roll(e, n_ext - k, 0)
                k *= 2
            dx_ref[:, sl] = alpha * dzg + e[0:tm, :] - dd[0:tm, :]

    return pl.pallas_call(
        body, name="pool_bwd", grid=(nt,),
        out_shape=[jax.ShapeDtypeStruct((T, D), F32), jax.ShapeDtypeStruct((G, Dg, Dg), F32),
                   jax.ShapeDtypeStruct((1, D), F32)],
        in_specs=[_rows(tm, D), pl.BlockSpec((H, D), lambda i: (jnp.minimum((i + 1) * r, last), 0)),
                  _rows(tm, D), pl.BlockSpec((H, D), lambda i: (jnp.maximum(i * r - 1, 0), 0)),
                  _full(pw.shape), _full((1, D))],
        out_specs=[_rows(tm, D), _full((G, Dg, Dg)), _full((1, D))],
        scratch_shapes=[pltpu.VMEM((n_ext, D), F32), pltpu.VMEM((n_ext, D), F32)],
        compiler_params=_params("arbitrary"),
    )(dz, dz, x, x, pw, scale)


def _adamw(name, recv, w, m, v, tm, layer=None, prev=None):
    R, C = w.shape[-2:]
    c1 = 1.0 - ADAM_B1 ** ADAM_STEP
    c2 = 1.0 - ADAM_B2 ** ADAM_STEP
    if layer is None:
        spec = _rows(tm, C)
    else:
        spec = pl.BlockSpec((None, tm, C), lambda i: (layer, i, 0))
    prev = list(prev) if prev is not None else []

    def body(r_ref, w_ref, m_ref, v_ref, *rest):
        g_ref, d_ref, nm_ref, nv_ref = rest[len(prev):]
        g = r_ref[0].astype(F32)
        for s in range(1, N_DEV):
            g = g + r_ref[s].astype(F32)
        m1 = ADAM_B1 * m_ref[...] + (1.0 - ADAM_B1) * g
        v1 = ADAM_B2 * v_ref[...] + (1.0 - ADAM_B2) * (g * g)
        m_hat = m1 / c1
        v_hat = v1 / c2
        g_ref[...] = g
        d_ref[...] = -ADAM_LR * (m_hat / (jnp.sqrt(v_hat) + ADAM_EPS) + ADAM_WD * w_ref[...])
        nm_ref[...] = m1
        nv_ref[...] = v1

    return pl.pallas_call(
        body, name=name, grid=(R // tm,),
        out_shape=[jax.ShapeDtypeStruct(w.shape, F32)] * 4,
        in_specs=[pl.BlockSpec((N_DEV, tm, C), lambda i: (0, i, 0))] + [spec] * 3
        + [pl.BlockSpec(memory_space=pl.ANY)] * len(prev),
        out_specs=[spec] * 4,
        input_output_aliases={4 + j: j for j in range(len(prev))},
        compiler_params=_params("parallel"),
    )(recv, w, m, v, *prev)


def _pad_rows(a, rows):
    return jnp.pad(a, ((0, rows - a.shape[0]), (0, 0)))


def kernel(x, pool_w, pool_scale, conv_w_in, conv_b_in, conv_dw, conv_dw_b, conv_ln_g, conv_ln_b, conv_w_out, conv_b_out, mix_ln_g, mix_ln_b, mlp_w1, mlp_b1, mlp_w2, mlp_b2, mlp_ln_g, mlp_ln_b, loss_target, m_pool_w, m_pool_scale, m_conv_w_in, m_conv_b_in, m_conv_dw, m_conv_dw_b, m_conv_ln_g, m_conv_ln_b, m_conv_w_out, m_conv_b_out, m_mix_ln_g, m_mix_ln_b, m_mlp_w1, m_mlp_b1, m_mlp_w2, m_mlp_b2, m_mlp_ln_g, m_mlp_ln_b, v_pool_w, v_pool_scale, v_conv_w_in, v_conv_b_in, v_conv_dw, v_conv_dw_b, v_conv_ln_g, v_conv_ln_b, v_conv_w_out, v_conv_b_out, v_mix_ln_g, v_mix_ln_b, v_mlp_w1, v_mlp_b1, v_mlp_w2, v_mlp_b2, v_mlp_ln_g, v_mlp_ln_b):
    _, T, D = x.shape
    L = mlp_w1.shape[0]
    assert L == 2 and pool_w.shape[0] == 1 and conv_w_in.shape[0] == 1
    G = pool_w.shape[1]
    Dg = D // G
    Fd = mlp_b1.shape[1]
    Fs = Fd // N_DEV
    Kc = conv_dw.shape[1]
    Dc = D // N_DEV
    alpha = float((2.0 * L) ** 0.25)
    x2d, tgt = x[0], loss_target[0]

    tm = _tile(T, 512)
    tm_wide = _tile(T, 1024)
    tm_conv = _tile(T, 256)
    tm_wg = _tile(T, 1024)

    def pack_sh(dw, dwb, lg, lb, bo, bi):
        rows = jnp.concatenate([dw[0], dwb, lg, lb, bo, bi.reshape(2, Dc)], axis=0)
        return _pad_rows(rows, SH_ROWS)

    SH_ROWS = -(-(Kc + 6) // 8) * 8
    def pack_rep(ps, mg, mb, b1, b2, lg, lb):
        rows = jnp.concatenate([ps, mg, mb, b1.reshape(L * Fd // D, D), b2, lg, lb], axis=0)
        return _pad_rows(rows, REP_ROWS)

    n_rep = 1 + 2 * L + L * Fd // D + 3 * L
    REP_ROWS = -(-n_rep // 8) * 8

    w_sh = pack_sh(conv_dw, conv_dw_b, conv_ln_g, conv_ln_b, conv_b_out, conv_b_in)
    m_sh = pack_sh(m_conv_dw, m_conv_dw_b, m_conv_ln_g, m_conv_ln_b, m_conv_b_out, m_conv_b_in)
    v_sh = pack_sh(v_conv_dw, v_conv_dw_b, v_conv_ln_g, v_conv_ln_b, v_conv_b_out, v_conv_b_in)
    w_rep = pack_rep(pool_scale, mix_ln_g, mix_ln_b, mlp_b1, mlp_b2, mlp_ln_g, mlp_ln_b)
    m_rep = pack_rep(m_pool_scale, m_mix_ln_g, m_mix_ln_b, m_mlp_b1, m_mlp_b2, m_mlp_ln_g, m_mlp_ln_b)
    v_rep = pack_rep(v_pool_scale, v_mix_ln_g, v_mix_ln_b, v_mlp_b1, v_mlp_b2, v_mlp_ln_g, v_mlp_ln_b)

    groups = [[pool_w[0]], [mlp_w1[0]], [mlp_w2[0]], [conv_w_in[0], conv_w_out[0]], [mlp_w1[1]], [mlp_w2[1]]]
    handles, tokens = [], []
    for i, grp in enumerate(groups):
        tie = tokens[-1][0, 0] if tokens else 0.0
        srcs = [(a + tie).astype(BF16) for a in grp] + ([w_sh] if i == 0 else [])
        h, tk = _xstart("gather_%d" % i, srcs, [False] * len(srcs))
        handles.append(h)
        tokens.append(tk)
    pw_all, sh_all = _xwait(handles[0], tokens[-1])
    pw = pw_all.transpose(1, 0, 2, 3).reshape(G, Dg, Dg)
    dw_full = sh_all[:, 0:Kc].transpose(1, 0, 2).reshape(Kc, D)

    def sh_row(i):
        return sh_all[:, i].reshape(1, D)

    dwb_full, cg_full, cb_full, bout_full = (sh_row(Kc + i) for i in range(4))
    bin_full = sh_all[:, Kc + 4:Kc + 6].reshape(1, 2 * D)

    h0, z_m0 = _pool_fwd(x2d, pw, pool_scale, mix_ln_g[0:1], mix_ln_b[0:1], alpha, tm, tuple(tokens))
    (w1b0,) = _xwait(handles[1], h0)
    u0, a0 = _mlp_up("mlp_up0", h0, w1b0, mlp_b1[0:1], tm_wide)
    (w2b0,) = _xwait(handles[2], a0)
    x1, z_f0 = _proj_ln("mlp_down0", a0, w2b0.reshape(Fd, D), mlp_b2[0:1], h0, mlp_ln_g[0:1], mlp_ln_b[0:1],
                        alpha, tm)
    win_b, wout_all = _xwait(handles[3], x1)
    w_out = wout_all.reshape(D, D)
    p, glu = _conv_in(x1, win_b, bin_full, tm)
    cz, s = _dwconv_fwd(glu, dw_full, dwb_full, cg_full, cb_full, tm_conv)
    h1, z_m1 = _proj_ln("conv_out", s, w_out, bout_full, x1, mix_ln_g[1:2], mix_ln_b[1:2], alpha, tm)
    (w1b1,) = _xwait(handles[4], h1)
    u1, a1 = _mlp_up("mlp_up1", h1, w1b1, mlp_b1[1:2], tm_wide)
    (w2b1,) = _xwait(handles[5], a1)
    x2, z_f1 = _proj_ln("mlp_down1", a1, w2b1.reshape(Fd, D), mlp_b2[1:2], h1, mlp_ln_g[1:2], mlp_ln_b[1:2],
                        alpha, tm)

    dz, dzb, g_fg1, g_fb1, g_b2_1, loss_cols = _ln_bwd_call("ln_bwd_f1", x2, tgt, z_f1, mlp_ln_g[1:2], tm)
    gw2_1 = _wgrad("gw2_1", a1, dzb, False, 4, 1, tm_wg)
    e_w2_1, tk = _xstart("grads_w2_1", [gw2_1], [True])
    du, dx, g_b1_1 = _mlp_bwd("mlp_bwd1", dz, dzb, u1, w2b1, w1b1, alpha, tm_wide, (tk,))
    gw1_1 = _wgrad("gw1_1", h1, du, True, 1, 2, tm_wg)
    e_w1_1, tk = _xstart("grads_w1_1", [gw1_1], [True])
    dz, dzb, g_mg1, g_mb1, g_bout, = _ln_bwd_call("ln_bwd_m1", dx, None, z_m1, mix_ln_g[1:2], tm, (tk,))
    gwout = _wgrad("gw_out", s, dzb, False, 1, 1, tm_wg)
    dc, g_cg, g_cb, g_dwb = _conv_out_bwd(dzb, w_out, cz, cg_full, cb_full, tm)
    dp, g_dw, g_bin = _dwconv_bwd(dc, glu, p, dw_full, tm_conv)
    gwin = _wgrad("gw_in", x1, dp, True, 1, 2, tm_wg)
    e_conv, tk = _xstart("grads_conv", [gwin, gwout], [True, True])
    dx = _dx_proj("conv_in_bwd", dz, dp, win_b, alpha, tm, (tk,))
    dz, dzb, g_fg0, g_fb0, g_b2_0 = _ln_bwd_call("ln_bwd_f0", dx, None, z_f0, mlp_ln_g[0:1], tm)
    gw2_0 = _wgrad("gw2_0", a0, dzb, False, 4, 1, tm_wg)
    e_w2_0, tk = _xstart("grads_w2_0", [gw2_0], [True])
    du, dx, g_b1_0 = _mlp_bwd("mlp_bwd0", dz, dzb, u0, w2b0, w1b0, alpha, tm_wide, (tk,))
    gw1_0 = _wgrad("gw1_0", h0, du, True, 1, 2, tm_wg)
    e_w1_0, tk = _xstart("grads_w1_0", [gw1_0], [True])
    dz, dzb, g_mg0, g_mb0, _ = _ln_bwd_call("ln_bwd_m0", dx, None, z_m0, mix_ln_g[0:1], tm, (tk,))
    grad_x, g_pw, g_ps = _pool_bwd(dz, x2d, pw, pool_scale, alpha, tm)

    loss = lax.psum(0.5 / D * jnp.sum(loss_cols), MESH_AXES)

    gpw_b = g_pw.reshape(G, N_DEV, Dg // N_DEV, Dg).transpose(1, 0, 2, 3).astype(BF16)

    def to_dev(vec, rows):
        return vec.reshape(rows, N_DEV, Dc).transpose(1, 0, 2)

    g_sh = jnp.concatenate(
        [to_dev(g_dw[0:Kc], Kc), to_dev(g_dwb, 1), to_dev(g_cg, 1), to_dev(g_cb, 1), to_dev(g_bout, 1),
         g_bin.reshape(N_DEV, 2, Dc), jnp.zeros((N_DEV, SH_ROWS - Kc - 6, Dc), F32)], axis=1)
    g_rep = _pad_rows(jnp.concatenate(
        [g_ps, g_mg0, g_mg1, g_mb0, g_mb1, g_b1_0.reshape(Fd // D, D), g_b1_1.reshape(Fd // D, D),
         g_b2_0, g_b2_1, g_fg0, g_fg1, g_fb0, g_fb1], axis=0), REP_ROWS)

    e_small, tk = _xstart("grads_small", [gpw_b, g_sh, g_rep], [True, True, False])

    def upd(name, recv, w, m, v):
        shape = w.shape
        C = shape[-1]
        R = w.size // C
        outs = _adamw(name, recv.reshape(N_DEV, R, C), w.reshape(R, C), m.reshape(R, C), v.reshape(R, C), _tile(R, 256))
        return [o.reshape(shape) for o in outs]

    def upd_layer(name, recv, w, m, v, layer, prev):
        return _adamw(name, recv, w, m, v, _tile(w.shape[1], 256), layer, prev)

    (r_w2_1,) = _xwait(e_w2_1, tk)
    o_w2 = upd_layer("adam_w2_1", r_w2_1, mlp_w2, m_mlp_w2, v_mlp_w2, 1, None)
    (r_w1_1,) = _xwait(e_w1_1, o_w2[0])
    o_w1 = upd_layer("adam_w1_1", r_w1_1, mlp_w1, m_mlp_w1, v_mlp_w1, 1, None)
    r_win, r_wout = _xwait(e_conv, o_w1[0])
    o_win = upd("adam_w_in", r_win, conv_w_in, m_conv_w_in, v_conv_w_in)
    o_wout = upd("adam_w_out", r_wout, conv_w_out, m_conv_w_out, v_conv_w_out)
    (r_w2_0,) = _xwait(e_w2_0, o_wout[0])
    o_w2 = upd_layer("adam_w2_0", r_w2_0, mlp_w2, m_mlp_w2, v_mlp_w2, 0, o_w2)
    (r_w1_0,) = _xwait(e_w1_0, o_w2[0])
    o_w1 = upd_layer("adam_w1_0", r_w1_0, mlp_w1, m_mlp_w1, v_mlp_w1, 0, o_w1)
    r_pw, r_sh, r_rep = _xwait(e_small, o_w1[0])
    o_pw = upd("adam_pool_w", r_pw, pool_w, m_pool_w, v_pool_w)
    o_sh = upd("adam_conv_vec", r_sh, w_sh, m_sh, v_sh)
    o_rep = upd("adam_replicated", r_rep, w_rep, m_rep, v_rep)

    def unpack_sh(a):
        return (a[0:Kc][None], a[Kc:Kc + 1], a[Kc + 1:Kc + 2], a[Kc + 2:Kc + 3], a[Kc + 3:Kc + 4],
                a[Kc + 4:Kc + 6].reshape(1, 2 * Dc))

    def unpack_rep(a):
        o = 0
        out = []
        for rows, shape in ((1, (1, D)), (L, (L, D)), (L, (L, D)), (L * Fd // D, (L, Fd)), (L, (L, D)), (L, (L, D)),
                            (L, (L, D))):
            out.append(a[o:o + rows].reshape(shape))
            o += rows
        return out

    results = []
    for kind in range(4):
        dwv, dwb, lg, lb, bo, bi = unpack_sh(o_sh[kind])
        ps, mg, mb, b1, b2, fg, fb = unpack_rep(o_rep[kind])
        results.append([o_pw[kind], ps, o_win[kind], bi, dwv, dwb, lg, lb, o_wout[kind], bo, mg, mb,
                        o_w1[kind], b1, o_w2[kind], b2, fg, fb])
    return (loss, grad_x[None], *results[0], *results[1], *results[2], *results[3])
```

```python
import jax
import jax.numpy as jnp
from jax import lax
from jax.experimental import pallas as pl
from jax.experimental.pallas import tpu as pltpu

N_DEV = 8
MESH_AXES = ("x", "y", "c")
POOL_WINDOWS = (2, 4, 8, 16)
POOL_HALO = 16
CONV_HALO = 32
LN_EPS = 1e-5
ADAM_LR = 0.001
ADAM_B1 = 0.9
ADAM_B2 = 0.999
ADAM_EPS = 1e-08
ADAM_WD = 0.01
ADAM_STEP = 10
VMEM_LIMIT = 56 * 1024 * 1024

F32 = jnp.float32
BF16 = jnp.bfloat16


def _mm(a, b):
    return lax.dot_general(a, b, (((1,), (0,)), ((), ())), preferred_element_type=F32)


def _mm_nt(a, b):
    return lax.dot_general(a, b, (((1,), (1,)), ((), ())), preferred_element_type=F32)


def _mm_tn(a, b):
    return lax.dot_general(a, b, (((0,), (0,)), ((), ())), preferred_element_type=F32)


def _tile(n, pref):
    t = min(n, pref)
    assert n % t == 0, (n, pref)
    return t


def _params(*sem):
    return pltpu.CompilerParams(dimension_semantics=sem, vmem_limit_bytes=VMEM_LIMIT)


def _full(shape):
    nd = len(shape)
    return pl.BlockSpec(shape, lambda *_: (0,) * nd)


def _rows(tm, d):
    return pl.BlockSpec((tm, d), lambda i: (i, 0))


def _ln_stats(z):
    mu = jnp.mean(z, axis=-1, keepdims=True)
    zc = z - mu
    var = jnp.mean(zc * zc, axis=-1, keepdims=True)
    rstd = lax.rsqrt(var + LN_EPS)
    return zc * rstd, rstd


def _ln_fwd(z, g, b):
    xhat, _ = _ln_stats(z)
    return xhat * g + b


def _ln_bwd(dy, z, g):
    xhat, rstd = _ln_stats(z)
    dxh = dy * g
    m1 = jnp.mean(dxh, axis=-1, keepdims=True)
    m2 = jnp.mean(dxh * xhat, axis=-1, keepdims=True)
    return rstd * (dxh - m1 - xhat * m2), xhat


def _colsum(v):
    return jnp.sum(v, axis=0, keepdims=True)


def _sigmoid(v):
    return 1.0 / (1.0 + jnp.exp(-v))


def _acc(ref, first, val):
    @pl.when(first)
    def _():
        ref[...] = val

    @pl.when(jnp.logical_not(first))
    def _():
        ref[...] += val


_HBM = pl.BlockSpec(memory_space=pltpu.HBM)
_SEM = pl.BlockSpec(memory_space=pltpu.SEMAPHORE)
_EFFECT = pltpu.SideEffectType.DATAFLOW_SIDE_EFFECTING


def _peers():
    x, y, c = (lax.axis_index(a) for a in MESH_AXES)
    out = []
    for d in range(1, N_DEV):
        px = (x + ((d >> 2) & 1)) % 2
        py = (y + ((d >> 1) & 1)) % 2
        pc = (c + (d & 1)) % 2
        out.append((d - 1, (px, py, pc), 4 * px + 2 * py + pc))
    return 4 * x + 2 * y + c, out


def _remote_copies(src_refs, land_refs, scatter, send_sems, recv_sems):
    me, peers = _peers()
    copies = []
    for i, pos, pid in peers:
        for k, (src, land) in enumerate(zip(src_refs, land_refs)):
            copies.append(pltpu.make_async_remote_copy(
                src_ref=src.at[pid] if scatter[k] else src, dst_ref=land.at[me],
                send_sem=send_sems.at[k * (N_DEV - 1) + i], recv_sem=recv_sems.at[k * (N_DEV - 1) + i],
                device_id=pos, device_id_type=pl.DeviceIdType.MESH))
    return copies


def _xstart(name, srcs, scatter):
    n = len(srcs)
    me = 4 * lax.axis_index("x") + 2 * lax.axis_index("y") + lax.axis_index("c")
    lands = []
    for s, sc in zip(srcs, scatter):
        own = lax.dynamic_index_in_dim(s, me, 0, keepdims=True) if sc else s[None]
        shape = s.shape if sc else (N_DEV,) + s.shape
        lands.append(lax.dynamic_update_slice(lax.empty(shape, s.dtype), own, (me,) + (0,) * (len(shape) - 1)))

    def body(*refs):
        src_refs, land_refs = refs[:n], refs[n:2 * n]
        send_sems, recv_sems = refs[2 * n], refs[2 * n + 1]
        token = refs[-1]
        for cp in _remote_copies(src_refs, land_refs, scatter, send_sems, recv_sems):
            cp.start()
        token[...] = jnp.zeros_like(token)

    outs = pl.pallas_call(
        body, name=name,
        out_shape=(pltpu.SemaphoreType.DMA((n * (N_DEV - 1),)), pltpu.SemaphoreType.DMA((n * (N_DEV - 1),)),
                   *[pltpu.HBM(a.shape, a.dtype) for a in srcs + lands], jax.ShapeDtypeStruct((8, 128), F32)),
        in_specs=(_HBM,) * (2 * n),
        out_specs=(_SEM, _SEM) + (_HBM,) * (2 * n) + (pl.BlockSpec(memory_space=pltpu.VMEM),),
        input_output_aliases={i: 2 + i for i in range(2 * n)},
        compiler_params=pltpu.CompilerParams(has_side_effects=_EFFECT),
    )(*[pltpu.with_memory_space_constraint(a, pltpu.HBM) for a in srcs + lands])
    return (name, scatter, outs[0], outs[1], outs[2:2 + n], outs[2 + n:2 + 2 * n]), outs[-1]


def _xwait(handle, after):
    name, scatter, send_sems, recv_sems, srcs, lands = handle
    n = len(srcs)

    def body(*refs):
        src_refs, land_refs = refs[:n], refs[n:2 * n]
        send, recv = refs[2 * n], refs[2 * n + 1]
        copies = _remote_copies(src_refs, land_refs, scatter, send, recv)
        for cp in copies:
            cp.wait_send()
        for cp in copies:
            cp.wait_recv()

    outs = pl.pallas_call(
        body, name=name + "_wait",
        out_shape=tuple(pltpu.HBM(a.shape, a.dtype) for a in (*srcs, *lands)),
        in_specs=(_HBM,) * (2 * n) + (_SEM, _SEM, pl.BlockSpec(memory_space=pl.ANY)),
        out_specs=(_HBM,) * (2 * n),
        input_output_aliases={i: i for i in range(2 * n)},
        compiler_params=pltpu.CompilerParams(has_side_effects=_EFFECT),
    )(*srcs, *lands, send_sems, recv_sems, after)
    return outs[n:]


def _tied_call(body, deps, in_specs, **kw):
    nd = len(deps)

    def tied_body(*refs):
        body(*refs[nd:])

    call = pl.pallas_call(tied_body, in_specs=[pl.BlockSpec(memory_space=pl.ANY)] * nd + list(in_specs), **kw)
    return lambda *args: call(*deps, *args)


def _pool_fwd(x, pw, scale, g, b, alpha, tm, deps=()):
    T, D = x.shape
    G = len(POOL_WINDOWS)
    Dg = D // G
    H = POOL_HALO
    r = tm // H

    def body(x_ref, xh_ref, pw_ref, sc_ref, g_ref, b_ref, y_ref, z_ref, ext_ref):
        i = pl.program_id(0)
        ext_ref[0:H, :] = jnp.where(i > 0, xh_ref[...], 0.0)
        ext_ref[H:, :] = x_ref[...]
        row = i * tm + lax.broadcasted_iota(jnp.int32, (tm, 1), 0)
        rowf = (row + 1).astype(F32)
        for gi, w in enumerate(POOL_WINDOWS):
            sl = slice(gi * Dg, (gi + 1) * Dg)
            s = ext_ref[:, sl]
            k = 1
            while k < w:
                s = s + pltpu.roll(s, k, 0)
                k *= 2
            inv = 1.0 / jnp.minimum(rowf, float(w))
            xg = x_ref[:, sl]
            d = s[H:, :] * inv - xg
            mix = _mm(d.astype(BF16), pw_ref[gi]) * sc_ref[:, sl]
            z_ref[:, sl] = alpha * xg + mix
        y_ref[...] = _ln_fwd(z_ref[...], g_ref[...], b_ref[...])

    return _tied_call(
        body, deps, name="pool_fwd", grid=(T // tm,),
        out_shape=[jax.ShapeDtypeStruct((T, D), F32)] * 2,
        in_specs=[_rows(tm, D), pl.BlockSpec((H, D), lambda i: (jnp.maximum(i * r - 1, 0), 0)),
                  _full(pw.shape), _full((1, D)), _full((1, D)), _full((1, D))],
        out_specs=[_rows(tm, D)] * 2,
        scratch_shapes=[pltpu.VMEM((tm + H, D), F32)],
        compiler_params=_params("parallel"),
    )(x, x, pw, scale, g, b)


def _mlp_up(name, h, w1b, b1, tm):
    T, D = h.shape
    nb, _, Fs = w1b.shape
    Fd = nb * Fs

    def body(h_ref, w_ref, b_ref, u_ref, a_ref):
        hb = h_ref[...].astype(BF16)
        for j in range(nb):
            sl = slice(j * Fs, (j + 1) * Fs)
            u = _mm(hb, w_ref[j]) + b_ref[:, sl]
            u_ref[:, sl] = u.astype(BF16)
            rl = jnp.maximum(u, 0.0)
            a_ref[:, sl] = (rl * rl).astype(BF16)

    return pl.pallas_call(
        body, name=name, grid=(T // tm,),
        out_shape=[jax.ShapeDtypeStruct((T, Fd), BF16)] * 2,
        in_specs=[_rows(tm, D), _full(w1b.shape), _full((1, Fd))],
        out_specs=[_rows(tm, Fd), _rows(tm, Fd)],
        compiler_params=_params("parallel"),
    )(h, w1b, b1)


def _proj_ln(name, a, w, bias, res, g, b, alpha, tm):
    T, K = a.shape
    D = w.shape[1]

    def body(a_ref, w_ref, bias_ref, res_ref, g_ref, b_ref, y_ref, z_ref):
        z = alpha * res_ref[...] + _mm(a_ref[...], w_ref[...]) + bias_ref[...]
        z_ref[...] = z
        y_ref[...] = _ln_fwd(z, g_ref[...], b_ref[...])

    return pl.pallas_call(
        body, name=name, grid=(T // tm,),
        out_shape=[jax.ShapeDtypeStruct((T, D), F32)] * 2,
        in_specs=[_rows(tm, K), _full((K, D)), _full((1, D)), _rows(tm, D), _full((1, D)), _full((1, D))],
        out_specs=[_rows(tm, D)] * 2,
        compiler_params=_params("parallel"),
    )(a, w, bias, res, g, b)


def _proj_ln_loss(name, a, w, bias, res, g, b, target, alpha, tm):
    T, K = a.shape
    D = w.shape[1]

    def body(a_ref, w_ref, bias_ref, res_ref, g_ref, b_ref, t_ref, dz_ref, dzb_ref, gg_ref, gb_ref, sdz_ref, loss_ref):
        first = pl.program_id(0) == 0
        z = alpha * res_ref[...] + _mm(a_ref[...], w_ref[...]) + bias_ref[...]
        gv = g_ref[...]
        e = _ln_fwd(z, gv, b_ref[...]) - t_ref[...]
        dy = e * (1.0 / D)
        dz, xhat = _ln_bwd(dy, z, gv)
        dz_ref[...] = dz
        dzb_ref[...] = dz.astype(BF16)
        _acc(gg_ref, first, _colsum(dy * xhat))
        _acc(gb_ref, first, _colsum(dy))
        _acc(sdz_ref, first, _colsum(dz))
        _acc(loss_ref, first, _colsum(e * e))

    return pl.pallas_call(
        body, name=name, grid=(T // tm,),
        out_shape=[jax.ShapeDtypeStruct((T, D), F32), jax.ShapeDtypeStruct((T, D), BF16)]
        + [jax.ShapeDtypeStruct((1, D), F32)] * 4,
        in_specs=[_rows(tm, K), _full((K, D)), _full((1, D)), _rows(tm, D), _full((1, D)), _full((1, D)), _rows(tm, D)],
        out_specs=[_rows(tm, D)] * 2 + [_full((1, D))] * 4,
        compiler_params=_params("arbitrary"),
    )(a, w, bias, res, g, b, target)


def _conv_in(x, wb, b_in, tm):
    T, D = x.shape
    nb, _, Ns = wb.shape
    half = nb // 2

    def body(x_ref, w_ref, b_ref, p_ref, glu_ref):
        xb = x_ref[...].astype(BF16)
        for j in range(half):
            sa = slice(j * Ns, (j + 1) * Ns)
            sg = slice(D + j * Ns, D + (j + 1) * Ns)
            a = _mm(xb, w_ref[j]) + b_ref[:, sa]
            gate = _mm(xb, w_ref[half + j]) + b_ref[:, sg]
            p_ref[:, sa] = a
            p_ref[:, sg] = gate
            glu_ref[:, sa] = a * _sigmoid(gate)

    return pl.pallas_call(
        body, name="conv_in", grid=(T // tm,),
        out_shape=[jax.ShapeDtypeStruct((T, 2 * D), F32), jax.ShapeDtypeStruct((T, D), F32)],
        in_specs=[_rows(tm, D), _full(wb.shape), _full((1, 2 * D))],
        out_specs=[_rows(tm, 2 * D), _rows(tm, D)],
        compiler_params=_params("parallel"),
    )(x, wb, b_in)


def _lane_chunk(d):
    return 128 if d % 128 == 0 else d


SUBLANES = 8


def _row_chunk(tm, pref):
    return pref if tm % pref == 0 else tm


def _slabs(ref, base, n_taps, r0, rows, cs):
    out = []
    for r in range(min(SUBLANES, n_taps)):
        nq = (n_taps - 1 - r) // SUBLANES + 1
        lo = base + r + r0
        slab = ref[lo:lo + rows + SUBLANES * (nq - 1), cs]
        out.append((slab, [(q, SUBLANES * q + r) for q in range(nq)]))
    return out


def _dwconv_fwd(glu, dw, dw_b, g, b, tm):
    T, D = glu.shape
    K = dw.shape[0]
    H = CONV_HALO
    off = H - (K - 1)
    r = tm // H
    cc = _lane_chunk(D)

    def body(x_ref, xh_ref, dw_ref, dwb_ref, g_ref, b_ref, cz_ref, s_ref, ext_ref, sh_ref):
        i = pl.program_id(0)
        ext_ref[0:H, :] = jnp.where(i > 0, xh_ref[...], 0.0)
        ext_ref[H:, :] = x_ref[...]
        for ph in range(min(SUBLANES, K)):
            n = tm + SUBLANES * ((K - 1 - ph) // SUBLANES)
            sh_ref[ph, 0:n, :] = ext_ref[off + ph:off + ph + n, :]
        rc = _row_chunk(tm, 128)
        for c0 in range(0, D, cc):
            cs = slice(c0, c0 + cc)
            for r0 in range(0, tm, rc):
                acc = jnp.zeros((rc, cc), F32) + dwb_ref[:, cs]
                for k in range(K):
                    q, ph = divmod(k, SUBLANES)
                    acc = acc + sh_ref[ph, SUBLANES * q + r0:SUBLANES * q + r0 + rc, cs] * dw_ref[k:k + 1, cs]
                cz_ref[r0:r0 + rc, cs] = acc
        ln = _ln_fwd(cz_ref[...], g_ref[...], b_ref[...])
        s_ref[...] = (ln * _sigmoid(ln)).astype(BF16)

    return pl.pallas_call(
        body, name="dwconv_fwd", grid=(T // tm,),
        out_shape=[jax.ShapeDtypeStruct((T, D), F32), jax.ShapeDtypeStruct((T, D), BF16)],
        in_specs=[_rows(tm, D), pl.BlockSpec((H, D), lambda i: (jnp.maximum(i * r - 1, 0), 0)),
                  _full((K, D)), _full((1, D)), _full((1, D)), _full((1, D))],
        out_specs=[_rows(tm, D)] * 2,
        scratch_shapes=[pltpu.VMEM((tm + H, D), F32),
                        pltpu.VMEM((SUBLANES, tm + SUBLANES * ((K - 1) // SUBLANES), D), F32)],
        compiler_params=_params("parallel"),
    )(glu, glu, dw, dw_b, g, b)


def _ln_bwd_store(dy, z_ref, g_ref, first, dz_ref, dzb_ref, gg_ref, gb_ref, sdz_ref):
    dz, xhat = _ln_bwd(dy, z_ref[...], g_ref[...])
    dz_ref[...] = dz
    dzb_ref[...] = dz.astype(BF16)
    _acc(gg_ref, first, _colsum(dy * xhat))
    _acc(gb_ref, first, _colsum(dy))
    _acc(sdz_ref, first, _colsum(dz))


def _ln_bwd_outs(T, D):
    shapes = [jax.ShapeDtypeStruct((T, D), F32), jax.ShapeDtypeStruct((T, D), BF16)] + [jax.ShapeDtypeStruct((1, D), F32)] * 3
    return shapes


def _mlp_bwd(name, dz, dzb, u, w2b, w1b, z_in, g_in, alpha, tm, deps=()):
    T, D = dz.shape
    nb, Fs, _ = w2b.shape
    Fd = nb * Fs

    def body(dz_ref, dzb_ref, u_ref, w2_ref, w1_ref, zin_ref, gin_ref,
             du_ref, gb1_ref, dzo_ref, dzob_ref, gg_ref, gb_ref, sdz_ref, dx_ref):
        i, j = pl.program_id(0), pl.program_id(1)
        da = _mm_nt(dzb_ref[...], w2_ref[...])
        du = da * (2.0 * jnp.maximum(u_ref[...].astype(F32), 0.0))
        dub = du.astype(BF16)
        du_ref[...] = dub
        _acc(gb1_ref.at[j], i == 0, _colsum(du))
        dh = _mm_nt(dub, w1_ref[...])

        @pl.when(j == 0)
        def _():
            dx_ref[...] = alpha * dz_ref[...] + dh

        @pl.when(j > 0)
        def _():
            dx_ref[...] += dh

        @pl.when(j == nb - 1)
        def _():
            _ln_bwd_store(dx_ref[...], zin_ref, gin_ref, i == 0, dzo_ref, dzob_ref, gg_ref, gb_ref, sdz_ref)

    tile = pl.BlockSpec((tm, D), lambda i, j: (i, 0))
    vec = pl.BlockSpec((1, D), lambda i, j: (0, 0))
    du, gb1, *rest = _tied_call(
        body, deps, name=name, grid=(T // tm, nb),
        out_shape=[jax.ShapeDtypeStruct((T, Fd), BF16), jax.ShapeDtypeStruct((nb, 1, Fs), F32)] + _ln_bwd_outs(T, D),
        in_specs=[tile, tile, pl.BlockSpec((tm, Fs), lambda i, j: (i, j)),
                  pl.BlockSpec((None, Fs, D), lambda i, j: (j, 0, 0)), pl.BlockSpec((None, D, Fs), lambda i, j: (j, 0, 0)),
                  tile, vec],
        out_specs=[pl.BlockSpec((tm, Fs), lambda i, j: (i, j)), pl.BlockSpec((nb, 1, Fs), lambda i, j: (0, 0, 0)),
                   tile, tile, vec, vec, vec],
        scratch_shapes=[pltpu.VMEM((tm, D), F32)],
        compiler_params=_params("arbitrary", "arbitrary"),
    )(dz, dzb, u, w2b, w1b, z_in, g_in)
    return (du, gb1.reshape(1, Fd), *rest)


def _wgrad(name, xm, dy, col_blocks, nk, nj, tm):
    T, K = xm.shape
    N = dy.shape[1]
    Kb, Nb = K // nk, N // nj
    nt = T // tm
    if col_blocks:
        per, Ns = N_DEV // nj, N // N_DEV
        out_shape = (N_DEV, K, Ns)
        out_spec = pl.BlockSpec((per, Kb, Ns), lambda k, j, t: (j, k, 0))
    else:
        per, Ks = N_DEV // nk, K // N_DEV
        out_shape = (N_DEV, Ks, N)
        out_spec = pl.BlockSpec((per, Ks, Nb), lambda k, j, t: (k, 0, j))

    def body(x_ref, dy_ref, o_ref, acc_ref):
        t = pl.program_id(2)
        _acc(acc_ref, t == 0, _mm_tn(x_ref[...].astype(BF16), dy_ref[...]))

        @pl.when(t == nt - 1)
        def _():
            for q in range(per):
                if col_blocks:
                    o_ref[q] = acc_ref[:, q * Ns:(q + 1) * Ns].astype(BF16)
                else:
                    o_ref[q] = acc_ref[q * Ks:(q + 1) * Ks, :].astype(BF16)

    return pl.pallas_call(
        body, name=name, grid=(nk, nj, nt),
        out_shape=jax.ShapeDtypeStruct(out_shape, BF16),
        in_specs=[pl.BlockSpec((tm, Kb), lambda k, j, t: (t, k)), pl.BlockSpec((tm, Nb), lambda k, j, t: (t, j))],
        out_specs=out_spec,
        scratch_shapes=[pltpu.VMEM((Kb, Nb), F32)],
        compiler_params=_params("parallel", "parallel", "arbitrary"),
    )(xm, dy)


def _conv_out_bwd(dzb, w_out, cz, g, b, tm, deps=()):
    T, D = cz.shape

    def body(dz_ref, w_ref, cz_ref, g_ref, b_ref, dc_ref, gg_ref, gb_ref, sdc_ref):
        first = pl.program_id(0) == 0
        ds = _mm_nt(dz_ref[...], w_ref[...])
        czv = cz_ref[...]
        gv = g_ref[...]
        ln = _ln_fwd(czv, gv, b_ref[...])
        sg = _sigmoid(ln)
        dln = ds * (sg * (1.0 + ln * (1.0 - sg)))
        dc, xhat = _ln_bwd(dln, czv, gv)
        dc_ref[...] = dc
        _acc(gg_ref, first, _colsum(dln * xhat))
        _acc(gb_ref, first, _colsum(dln))
        _acc(sdc_ref, first, _colsum(dc))

    return _tied_call(
        body, deps, name="conv_out_bwd", grid=(T // tm,),
        out_shape=[jax.ShapeDtypeStruct((T, D), F32)] + [jax.ShapeDtypeStruct((1, D), F32)] * 3,
        in_specs=[_rows(tm, D), _full((D, D)), _rows(tm, D), _full((1, D)), _full((1, D))],
        out_specs=[_rows(tm, D)] + [_full((1, D))] * 3,
        compiler_params=_params("arbitrary"),
    )(dzb, w_out, cz, g, b)


def _dwconv_bwd(dc, glu, p, dw, tm):
    T, D = dc.shape
    K = dw.shape[0]
    Kp = -(-K // 8) * 8
    H = CONV_HALO
    off = H - (K - 1)
    r = tm // H
    last = T // H - 1
    nt = T // tm
    cc = _lane_chunk(D)

    def body(dc_ref, dch_ref, x_ref, xh_ref, p_ref, dw_ref, dp_ref, gdw_ref, gbin_ref, edc_ref, ex_ref, dglu_ref,
             gacc_ref):
        i = pl.program_id(0)
        first = i == 0
        edc_ref[0:tm, :] = dc_ref[...]
        edc_ref[tm:, :] = jnp.where(i < nt - 1, dch_ref[...], 0.0)
        ex_ref[0:H, :] = jnp.where(i > 0, xh_ref[...], 0.0)
        ex_ref[H:, :] = x_ref[...]

        @pl.when(first)
        def _():
            gacc_ref[...] = jnp.zeros_like(gacc_ref)

        rc = _row_chunk(tm, 64)
        for c0 in range(0, D, cc):
            cs = slice(c0, c0 + cc)
            for r0 in range(0, tm, rc):
                dcv = dc_ref[r0:r0 + rc, cs]
                acc = jnp.zeros((rc, cc), F32)
                for slab, taps in _slabs(edc_ref, 0, K, r0, rc, cs):
                    for q, m in taps:
                        acc = acc + slab[SUBLANES * q:SUBLANES * q + rc] * dw_ref[K - 1 - m:K - m, cs]
                dglu_ref[r0:r0 + rc, cs] = acc
                for slab, taps in _slabs(ex_ref, off, K, r0, rc, cs):
                    for q, k in taps:
                        part = (dcv * slab[SUBLANES * q:SUBLANES * q + rc]).reshape(rc // SUBLANES, SUBLANES, cc)
                        gacc_ref[k, :, cs] += jnp.sum(part, axis=0)

        @pl.when(i == nt - 1)
        def _():
            gdw_ref[...] = jnp.zeros_like(gdw_ref)
            gdw_ref[0:K, :] = jnp.sum(gacc_ref[...], axis=1)
        dglu = dglu_ref[...]
        a = p_ref[:, 0:D]
        sg = _sigmoid(p_ref[:, D:2 * D])
        da = dglu * sg
        dgate = dglu * a * (sg * (1.0 - sg))
        dp_ref[:, 0:D] = da.astype(BF16)
        dp_ref[:, D:2 * D] = dgate.astype(BF16)
        _acc(gbin_ref.at[:, 0:D], first, _colsum(da))
        _acc(gbin_ref.at[:, D:2 * D], first, _colsum(dgate))

    return pl.pallas_call(
        body, name="dwconv_bwd", grid=(nt,),
        out_shape=[jax.ShapeDtypeStruct((T, 2 * D), BF16), jax.ShapeDtypeStruct((Kp, D), F32),
                   jax.ShapeDtypeStruct((1, 2 * D), F32)],
        in_specs=[_rows(tm, D), pl.BlockSpec((H, D), lambda i: (jnp.minimum((i + 1) * r, last), 0)),
                  _rows(tm, D), pl.BlockSpec((H, D), lambda i: (jnp.maximum(i * r - 1, 0), 0)),
                  _rows(tm, 2 * D), _full((K, D))],
        out_specs=[_rows(tm, 2 * D), _full((Kp, D)), _full((1, 2 * D))],
        scratch_shapes=[pltpu.VMEM((tm + H, D), F32), pltpu.VMEM((tm + H, D), F32), pltpu.VMEM((tm, D), F32),
                        pltpu.VMEM((K, SUBLANES, D), F32)],
        compiler_params=_params("arbitrary"),
    )(dc, dc, glu, glu, p, dw)


def _dx_proj(name, dz, dy, wb, z_in, g_in, alpha, tm, deps=()):
    T, D = dz.shape
    nb, _, Ns = wb.shape
    N = nb * Ns

    def body(dz_ref, dy_ref, w_ref, zin_ref, gin_ref, dzo_ref, dzob_ref, gg_ref, gb_ref, sdz_ref):
        dx = alpha * dz_ref[...]
        for j in range(nb):
            dx = dx + _mm_nt(dy_ref[:, j * Ns:(j + 1) * Ns], w_ref[j])
        _ln_bwd_store(dx, zin_ref, gin_ref, pl.program_id(0) == 0, dzo_ref, dzob_ref, gg_ref, gb_ref, sdz_ref)

    return _tied_call(
        body, deps, name=name, grid=(T // tm,),
        out_shape=_ln_bwd_outs(T, D),
        in_specs=[_rows(tm, D), _rows(tm, N), _full(wb.shape), _rows(tm, D), _full((1, D))],
        out_specs=[_rows(tm, D)] * 2 + [_full((1, D))] * 3,
        compiler_params=_params("arbitrary"),
    )(dz, dy, wb, z_in, g_in)


def _pool_bwd(dz, x, pw, scale, alpha, tm, deps=()):
    T, D = x.shape
    G = len(POOL_WINDOWS)
    Dg = D // G
    H = POOL_HALO
    r = tm // H
    last = T // H - 1
    nt = T // tm
    n_ext = tm + H

    def body(dz_ref, dzh_ref, x_ref, xh_ref, pw_ref, sc_ref, dx_ref, gpw_ref, gsc_ref, edz_ref, ex_ref):
        i = pl.program_id(0)
        first = i == 0
        edz_ref[0:tm, :] = dz_ref[...]
        edz_ref[tm:, :] = jnp.where(i < nt - 1, dzh_ref[...], 0.0)
        ex_ref[0:H, :] = jnp.where(i > 0, xh_ref[...], 0.0)
        ex_ref[H:, :] = x_ref[...]
        row = i * tm + lax.broadcasted_iota(jnp.int32, (tm, 1), 0)
        rowf = (row + 1).astype(F32)
        erow = i * tm + lax.broadcasted_iota(jnp.int32, (n_ext, 1), 0)
        erowf = (erow + 1).astype(F32)
        for gi, w in enumerate(POOL_WINDOWS):
            sl = slice(gi * Dg, (gi + 1) * Dg)
            s = ex_ref[:, sl]
            k = 1
            while k < w:
                s = s + pltpu.roll(s, k, 0)
                k *= 2
            xg = x_ref[:, sl]
            d = (s[H:, :] * (1.0 / jnp.minimum(rowf, float(w))) - xg).astype(BF16)
            wg = pw_ref[gi]
            premix = _mm(d, wg)
            dzg = dz_ref[:, sl]
            _acc(gsc_ref.at[:, sl], first, _colsum(dzg * premix))
            dpre = edz_ref[:, sl] * sc_ref[:, sl]
            dpre_b = dpre.astype(BF16)
            _acc(gpw_ref.at[gi], first, _mm_tn(d, dpre_b[0:tm, :]))
            dd = _mm_nt(dpre_b, wg)
            e = dd * (1.0 / jnp.minimum(erowf, float(w)))
            k = 1
            while k < w:
                e = e + pltpu.roll(e, n_ext - k, 0)
                k *= 2
            dx_ref[:, sl] = alpha * dzg + e[0:tm, :] - dd[0:tm, :]

    return _tied_call(
        body, deps, name="pool_bwd", grid=(nt,),
        out_shape=[jax.ShapeDtypeStruct((T, D), F32), jax.ShapeDtypeStruct((G, Dg, Dg), F32),
                   jax.ShapeDtypeStruct((1, D), F32)],
        in_specs=[_rows(tm, D), pl.BlockSpec((H, D), lambda i: (jnp.minimum((i + 1) * r, last), 0)),
                  _rows(tm, D), pl.BlockSpec((H, D), lambda i: (jnp.maximum(i * r - 1, 0), 0)),
                  _full(pw.shape), _full((1, D))],
        out_specs=[_rows(tm, D), _full((G, Dg, Dg)), _full((1, D))],
        scratch_shapes=[pltpu.VMEM((n_ext, D), F32), pltpu.VMEM((n_ext, D), F32)],
        compiler_params=_params("arbitrary"),
    )(dz, dz, x, x, pw, scale)


def _adamw(name, recv, w, m, v, tm, layer=None, prev=None):
    R, C = w.shape[-2:]
    c1 = 1.0 - ADAM_B1 ** ADAM_STEP
    c2 = 1.0 - ADAM_B2 ** ADAM_STEP
    if layer is None:
        spec = _rows(tm, C)
    else:
        spec = pl.BlockSpec((None, tm, C), lambda i: (layer, i, 0))
    prev = list(prev) if prev is not None else []

    def body(r_ref, w_ref, m_ref, v_ref, *rest):
        g_ref, d_ref, nm_ref, nv_ref = rest[len(prev):]
        g = r_ref[0].astype(F32)
        for s in range(1, N_DEV):
            g = g + r_ref[s].astype(F32)
        m1 = ADAM_B1 * m_ref[...] + (1.0 - ADAM_B1) * g
        v1 = ADAM_B2 * v_ref[...] + (1.0 - ADAM_B2) * (g * g)
        m_hat = m1 / c1
        v_hat = v1 / c2
        g_ref[...] = g
        d_ref[...] = -ADAM_LR * (m_hat / (jnp.sqrt(v_hat) + ADAM_EPS) + ADAM_WD * w_ref[...])
        nm_ref[...] = m1
        nv_ref[...] = v1

    return pl.pallas_call(
        body, name=name, grid=(R // tm,),
        out_shape=[jax.ShapeDtypeStruct(w.shape, F32)] * 4,
        in_specs=[pl.BlockSpec((N_DEV, tm, C), lambda i: (0, i, 0))] + [spec] * 3
        + [pl.BlockSpec(memory_space=pl.ANY)] * len(prev),
        out_specs=[spec] * 4,
        input_output_aliases={4 + j: j for j in range(len(prev))},
        compiler_params=_params("parallel"),
    )(recv, w, m, v, *prev)


def _pad_rows(a, rows):
    return jnp.pad(a, ((0, rows - a.shape[0]), (0, 0)))


def kernel(x, pool_w, pool_scale, conv_w_in, conv_b_in, conv_dw, conv_dw_b, conv_ln_g, conv_ln_b, conv_w_out, conv_b_out, mix_ln_g, mix_ln_b, mlp_w1, mlp_b1, mlp_w2, mlp_b2, mlp_ln_g, mlp_ln_b, loss_target, m_pool_w, m_pool_scale, m_conv_w_in, m_conv_b_in, m_conv_dw, m_conv_dw_b, m_conv_ln_g, m_conv_ln_b, m_conv_w_out, m_conv_b_out, m_mix_ln_g, m_mix_ln_b, m_mlp_w1, m_mlp_b1, m_mlp_w2, m_mlp_b2, m_mlp_ln_g, m_mlp_ln_b, v_pool_w, v_pool_scale, v_conv_w_in, v_conv_b_in, v_conv_dw, v_conv_dw_b, v_conv_ln_g, v_conv_ln_b, v_conv_w_out, v_conv_b_out, v_mix_ln_g, v_mix_ln_b, v_mlp_w1, v_mlp_b1, v_mlp_w2, v_mlp_b2, v_mlp_ln_g, v_mlp_ln_b):
    _, T, D = x.shape
    L = mlp_w1.shape[0]
    assert L == 2 and pool_w.shape[0] == 1 and conv_w_in.shape[0] == 1
    G = pool_w.shape[1]
    Dg = D // G
    Fd = mlp_b1.shape[1]
    Fs = Fd // N_DEV
    Kc = conv_dw.shape[1]
    Dc = D // N_DEV
    alpha = float((2.0 * L) ** 0.25)
    x2d, tgt = x[0], loss_target[0]

    tm = _tile(T, 512)
    tm_wide = _tile(T, 512)
    tm_conv = _tile(T, 256)
    tm_wg = _tile(T, 1024)

    def pack_sh(dw, dwb, lg, lb, bo, bi):
        rows = jnp.concatenate([dw[0], dwb, lg, lb, bo, bi.reshape(2, Dc)], axis=0)
        return _pad_rows(rows, SH_ROWS)

    SH_ROWS = -(-(Kc + 6) // 8) * 8
    def pack_rep(ps, mg, mb, b1, b2, lg, lb):
        rows = jnp.concatenate([ps, mg, mb, b1.reshape(L * Fd // D, D), b2, lg, lb], axis=0)
        return _pad_rows(rows, REP_ROWS)

    n_rep = 1 + 2 * L + L * Fd // D + 3 * L
    REP_ROWS = -(-n_rep // 8) * 8

    w_sh = pack_sh(conv_dw, conv_dw_b, conv_ln_g, conv_ln_b, conv_b_out, conv_b_in)
    m_sh = pack_sh(m_conv_dw, m_conv_dw_b, m_conv_ln_g, m_conv_ln_b, m_conv_b_out, m_conv_b_in)
    v_sh = pack_sh(v_conv_dw, v_conv_dw_b, v_conv_ln_g, v_conv_ln_b, v_conv_b_out, v_conv_b_in)
    w_rep = pack_rep(pool_scale, mix_ln_g, mix_ln_b, mlp_b1, mlp_b2, mlp_ln_g, mlp_ln_b)
    m_rep = pack_rep(m_pool_scale, m_mix_ln_g, m_mix_ln_b, m_mlp_b1, m_mlp_b2, m_mlp_ln_g, m_mlp_ln_b)
    v_rep = pack_rep(v_pool_scale, v_mix_ln_g, v_mix_ln_b, v_mlp_b1, v_mlp_b2, v_mlp_ln_g, v_mlp_ln_b)

    groups = [[pool_w[0]], [mlp_w1[0]], [mlp_w2[0]], [conv_w_in[0], conv_w_out[0]], [mlp_w1[1]], [mlp_w2[1]]]
    handles, tokens = [], []
    for i, grp in enumerate(groups):
        tie = tokens[-1][0, 0] if tokens else 0.0
        srcs = [(a + tie).astype(BF16) for a in grp] + ([w_sh] if i == 0 else [])
        h, tk = _xstart("gather_%d" % i, srcs, [False] * len(srcs))
        handles.append(h)
        tokens.append(tk)
    pw_all, sh_all = _xwait(handles[0], tokens[-1])
    pw = pw_all.transpose(1, 0, 2, 3).reshape(G, Dg, Dg)
    dw_full = sh_all[:, 0:Kc].transpose(1, 0, 2).reshape(Kc, D)

    def sh_row(i):
        return sh_all[:, i].reshape(1, D)

    dwb_full, cg_full, cb_full, bout_full = (sh_row(Kc + i) for i in range(4))
    bin_full = sh_all[:, Kc + 4:Kc + 6].reshape(1, 2 * D)

    h0, z_m0 = _pool_fwd(x2d, pw, pool_scale, mix_ln_g[0:1], mix_ln_b[0:1], alpha, tm, tuple(tokens))
    (w1b0,) = _xwait(handles[1], h0)
    u0, a0 = _mlp_up("mlp_up0", h0, w1b0, mlp_b1[0:1], tm)
    (w2b0,) = _xwait(handles[2], a0)
    x1, z_f0 = _proj_ln("mlp_down0", a0, w2b0.reshape(Fd, D), mlp_b2[0:1], h0, mlp_ln_g[0:1], mlp_ln_b[0:1],
                        alpha, tm)
    win_b, wout_all = _xwait(handles[3], x1)
    w_out = wout_all.reshape(D, D)
    p, glu = _conv_in(x1, win_b, bin_full, tm)
    cz, s = _dwconv_fwd(glu, dw_full, dwb_full, cg_full, cb_full, tm_conv)
    h1, z_m1 = _proj_ln("conv_out", s, w_out, bout_full, x1, mix_ln_g[1:2], mix_ln_b[1:2], alpha, tm)
    (w1b1,) = _xwait(handles[4], h1)
    u1, a1 = _mlp_up("mlp_up1", h1, w1b1, mlp_b1[1:2], tm)
    (w2b1,) = _xwait(handles[5], a1)

    dz, dzb, g_fg1, g_fb1, g_b2_1, loss_cols = _proj_ln_loss(
        "mlp_down1", a1, w2b1.reshape(Fd, D), mlp_b2[1:2], h1, mlp_ln_g[1:2], mlp_ln_b[1:2], tgt, alpha, tm)
    gw2_1 = _wgrad("gw2_1", a1, dzb, False, 4, 1, tm_wg)
    e_w2_1, tk = _xstart("grads_w2_1", [gw2_1], [True])
    du, g_b1_1, dz, dzb, g_mg1, g_mb1, g_bout = _mlp_bwd(
        "mlp_bwd1", dz, dzb, u1, w2b1, w1b1, z_m1, mix_ln_g[1:2], alpha, tm_wide, (tk,))
    gw1_1 = _wgrad("gw1_1", h1, du, True, 1, 2, tm_wg)
    e_w1_1, tk = _xstart("grads_w1_1", [gw1_1], [True])
    gwout = _wgrad("gw_out", s, dzb, False, 1, 1, tm_wg)
    dc, g_cg, g_cb, g_dwb = _conv_out_bwd(dzb, w_out, cz, cg_full, cb_full, tm, (tk,))
    dp, g_dw, g_bin = _dwconv_bwd(dc, glu, p, dw_full, tm_conv)
    gwin = _wgrad("gw_in", x1, dp, True, 1, 2, tm_wg)
    e_conv, tk = _xstart("grads_conv", [gwin, gwout], [True, True])
    dz, dzb, g_fg0, g_fb0, g_b2_0 = _dx_proj("conv_in_bwd", dz, dp, win_b, z_f0, mlp_ln_g[0:1], alpha, tm, (tk,))
    gw2_0 = _wgrad("gw2_0", a0, dzb, False, 4, 1, tm_wg)
    e_w2_0, tk = _xstart("grads_w2_0", [gw2_0], [True])
    du, g_b1_0, dz, dzb, g_mg0, g_mb0, _ = _mlp_bwd(
        "mlp_bwd0", dz, dzb, u0, w2b0, w1b0, z_m0, mix_ln_g[0:1], alpha, tm_wide, (tk,))
    gw1_0 = _wgrad("gw1_0", h0, du, True, 1, 2, tm_wg)
    e_w1_0, tk = _xstart("grads_w1_0", [gw1_0], [True])
    grad_x, g_pw, g_ps = _pool_bwd(dz, x2d, pw, pool_scale, alpha, tm, (tk,))

    loss = lax.psum(0.5 / D * jnp.sum(loss_cols), MESH_AXES)

    gpw_b = g_pw.reshape(G, N_DEV, Dg // N_DEV, Dg).transpose(1, 0, 2, 3).astype(BF16)

    def to_dev(vec, rows):
        return vec.reshape(rows, N_DEV, Dc).transpose(1, 0, 2)

    g_sh = jnp.concatenate(
        [to_dev(g_dw[0:Kc], Kc), to_dev(g_dwb, 1), to_dev(g_cg, 1), to_dev(g_cb, 1), to_dev(g_bout, 1),
         g_bin.reshape(N_DEV, 2, Dc), jnp.zeros((N_DEV, SH_ROWS - Kc - 6, Dc), F32)], axis=1)
    g_rep = _pad_rows(jnp.concatenate(
        [g_ps, g_mg0, g_mg1, g_mb0, g_mb1, g_b1_0.reshape(Fd // D, D), g_b1_1.reshape(Fd // D, D),
         g_b2_0, g_b2_1, g_fg0, g_fg1, g_fb0, g_fb1], axis=0), REP_ROWS)

    e_small, tk = _xstart("grads_small", [gpw_b, g_sh, g_rep], [True, True, False])

    def upd(name, recv, w, m, v):
        shape = w.shape
        C = shape[-1]
        R = w.size // C
        outs = _adamw(name, recv.reshape(N_DEV, R, C), w.reshape(R, C), m.reshape(R, C), v.reshape(R, C), _tile(R, 256))
        return [o.reshape(shape) for o in outs]

    def upd_layer(name, recv, w, m, v, layer, prev):
        return _adamw(name, recv, w, m, v, _tile(w.shape[1], 256), layer, prev)

    (r_w2_1,) = _xwait(e_w2_1, tk)
    o_w2 = upd_layer("adam_w2_1", r_w2_1, mlp_w2, m_mlp_w2, v_mlp_w2, 1, None)
    (r_w1_1,) = _xwait(e_w1_1, o_w2[0])
    o_w1 = upd_layer("adam_w1_1", r_w1_1, mlp_w1, m_mlp_w1, v_mlp_w1, 1, None)
    r_win, r_wout = _xwait(e_conv, o_w1[0])
    o_win = upd("adam_w_in", r_win, conv_w_in, m_conv_w_in, v_conv_w_in)
    o_wout = upd("adam_w_out", r_wout, conv_w_out, m_conv_w_out, v_conv_w_out)
    (r_w2_0,) = _xwait(e_w2_0, o_wout[0])
    o_w2 = upd_layer("adam_w2_0", r_w2_0, mlp_w2, m_mlp_w2, v_mlp_w2, 0, o_w2)
    (r_w1_0,) = _xwait(e_w1_0, o_w2[0])
    o_w1 = upd_layer("adam_w1_0", r_w1_0, mlp_w1, m_mlp_w1, v_mlp_w1, 0, o_w1)
    r_pw, r_sh, r_rep = _xwait(e_small, o_w1[0])
    o_pw = upd("adam_pool_w", r_pw, pool_w, m_pool_w, v_pool_w)
    o_sh = upd("adam_conv_vec", r_sh, w_sh, m_sh, v_sh)
    o_rep = upd("adam_replicated", r_rep, w_rep, m_rep, v_rep)

    def unpack_sh(a):
        return (a[0:Kc][None], a[Kc:Kc + 1], a[Kc + 1:Kc + 2], a[Kc + 2:Kc + 3], a[Kc + 3:Kc + 4],
                a[Kc + 4:Kc + 6].reshape(1, 2 * Dc))

    def unpack_rep(a):
        o = 0
        out = []
        for rows, shape in ((1, (1, D)), (L, (L, D)), (L, (L, D)), (L * Fd // D, (L, Fd)), (L, (L, D)), (L, (L, D)),
                            (L, (L, D))):
            out.append(a[o:o + rows].reshape(shape))
            o += rows
        return out

    results = []
    for kind in range(4):
        dwv, dwb, lg, lb, bo, bi = unpack_sh(o_sh[kind])
        ps, mg, mb, b1, b2, fg, fb = unpack_rep(o_rep[kind])
        results.append([o_pw[kind], ps, o_win[kind], bi, dwv, dwb, lg, lb, o_wout[kind], bo, mg, mb,
                        o_w1[kind], b1, o_w2[kind], b2, fg, fb])
    return (loss, grad_x[None], *results[0], *results[1], *results[2], *results[3])
```

```python
import jax
import jax.numpy as jnp
from jax import lax
from jax.experimental import pallas as pl
from jax.experimental.pallas import tpu as pltpu

N_DEV = 8
MESH_AXES = ("x", "y", "c")
POOL_WINDOWS = (2, 4, 8, 16)
POOL_HALO = 16
CONV_HALO = 32
LN_EPS = 1e-5
ADAM_LR = 0.001
ADAM_B1 = 0.9
ADAM_B2 = 0.999
ADAM_EPS = 1e-08
ADAM_WD = 0.01
ADAM_STEP = 10
VMEM_LIMIT = 56 * 1024 * 1024

F32 = jnp.float32
BF16 = jnp.bfloat16


def _mm(a, b):
    return lax.dot_general(a, b, (((1,), (0,)), ((), ())), preferred_element_type=F32)


def _mm_nt(a, b):
    return lax.dot_general(a, b, (((1,), (1,)), ((), ())), preferred_element_type=F32)


def _mm_tn(a, b):
    return lax.dot_general(a, b, (((0,), (0,)), ((), ())), preferred_element_type=F32)


def _tile(n, pref):
    t = min(n, pref)
    assert n % t == 0, (n, pref)
    return t


def _params(*sem):
    return pltpu.CompilerParams(dimension_semantics=sem, vmem_limit_bytes=VMEM_LIMIT)


def _full(shape):
    nd = len(shape)
    return pl.BlockSpec(shape, lambda *_: (0,) * nd)


def _rows(tm, d):
    return pl.BlockSpec((tm, d), lambda i: (i, 0))


def _ln_stats(z):
    mu = jnp.mean(z, axis=-1, keepdims=True)
    zc = z - mu
    var = jnp.mean(zc * zc, axis=-1, keepdims=True)
    rstd = lax.rsqrt(var + LN_EPS)
    return zc * rstd, rstd


def _ln_fwd(z, g, b):
    xhat, _ = _ln_stats(z)
    return xhat * g + b


def _ln_bwd(dy, z, g):
    xhat, rstd = _ln_stats(z)
    dxh = dy * g
    m1 = jnp.mean(dxh, axis=-1, keepdims=True)
    m2 = jnp.mean(dxh * xhat, axis=-1, keepdims=True)
    return rstd * (dxh - m1 - xhat * m2), xhat


def _colsum(v):
    return jnp.sum(v, axis=0, keepdims=True)


def _sigmoid(v):
    return 1.0 / (1.0 + jnp.exp(-v))


def _acc(ref, first, val):
    @pl.when(first)
    def _():
        ref[...] = val

    @pl.when(jnp.logical_not(first))
    def _():
        ref[...] += val


_HBM = pl.BlockSpec(memory_space=pltpu.HBM)
_SEM = pl.BlockSpec(memory_space=pltpu.SEMAPHORE)
_EFFECT = pltpu.SideEffectType.DATAFLOW_SIDE_EFFECTING


def _peers():
    x, y, c = (lax.axis_index(a) for a in MESH_AXES)
    out = []
    for d in range(1, N_DEV):
        px = (x + ((d >> 2) & 1)) % 2
        py = (y + ((d >> 1) & 1)) % 2
        pc = (c + (d & 1)) % 2
        out.append((d - 1, (px, py, pc), 4 * px + 2 * py + pc))
    return 4 * x + 2 * y + c, out


def _remote_copies(src_refs, land_refs, scatter, send_sems, recv_sems):
    me, peers = _peers()
    copies = []
    for i, pos, pid in peers:
        for k, (src, land) in enumerate(zip(src_refs, land_refs)):
            copies.append(pltpu.make_async_remote_copy(
                src_ref=src.at[pid] if scatter[k] else src, dst_ref=land.at[me],
                send_sem=send_sems.at[k * (N_DEV - 1) + i], recv_sem=recv_sems.at[k * (N_DEV - 1) + i],
                device_id=pos, device_id_type=pl.DeviceIdType.MESH))
    return copies


def _xstart(name, srcs, scatter):
    n = len(srcs)
    me = 4 * lax.axis_index("x") + 2 * lax.axis_index("y") + lax.axis_index("c")
    lands = []
    for s, sc in zip(srcs, scatter):
        own = lax.dynamic_index_in_dim(s, me, 0, keepdims=True) if sc else s[None]
        shape = s.shape if sc else (N_DEV,) + s.shape
        lands.append(lax.dynamic_update_slice(lax.empty(shape, s.dtype), own, (me,) + (0,) * (len(shape) - 1)))

    def body(*refs):
        src_refs, land_refs = refs[:n], refs[n:2 * n]
        send_sems, recv_sems = refs[2 * n], refs[2 * n + 1]
        token = refs[-1]
        for cp in _remote_copies(src_refs, land_refs, scatter, send_sems, recv_sems):
            cp.start()
        token[...] = jnp.zeros_like(token)

    outs = pl.pallas_call(
        body, name=name,
        out_shape=(pltpu.SemaphoreType.DMA((n * (N_DEV - 1),)), pltpu.SemaphoreType.DMA((n * (N_DEV - 1),)),
                   *[pltpu.HBM(a.shape, a.dtype) for a in srcs + lands], jax.ShapeDtypeStruct((8, 128), F32)),
        in_specs=(_HBM,) * (2 * n),
        out_specs=(_SEM, _SEM) + (_HBM,) * (2 * n) + (pl.BlockSpec(memory_space=pltpu.VMEM),),
        input_output_aliases={i: 2 + i for i in range(2 * n)},
        compiler_params=pltpu.CompilerParams(has_side_effects=_EFFECT),
    )(*[pltpu.with_memory_space_constraint(a, pltpu.HBM) for a in srcs + lands])
    return (name, scatter, outs[0], outs[1], outs[2:2 + n], outs[2 + n:2 + 2 * n]), outs[-1]


def _xwait(handle, after):
    name, scatter, send_sems, recv_sems, srcs, lands = handle
    n = len(srcs)

    def body(*refs):
        src_refs, land_refs = refs[:n], refs[n:2 * n]
        send, recv = refs[2 * n], refs[2 * n + 1]
        copies = _remote_copies(src_refs, land_refs, scatter, send, recv)
        for cp in copies:
            cp.wait_send()
        for cp in copies:
            cp.wait_recv()

    outs = pl.pallas_call(
        body, name=name + "_wait",
        out_shape=tuple(pltpu.HBM(a.shape, a.dtype) for a in (*srcs, *lands)),
        in_specs=(_HBM,) * (2 * n) + (_SEM, _SEM, pl.BlockSpec(memory_space=pl.ANY)),
        out_specs=(_HBM,) * (2 * n),
        input_output_aliases={i: i for i in range(2 * n)},
        compiler_params=pltpu.CompilerParams(has_side_effects=_EFFECT),
    )(*srcs, *lands, send_sems, recv_sems, after)
    return outs[n:]


def _tied_call(body, deps, in_specs, **kw):
    nd = len(deps)

    def tied_body(*refs):
        body(*refs[nd:])

    call = pl.pallas_call(tied_body, in_specs=[pl.BlockSpec(memory_space=pl.ANY)] * nd + list(in_specs), **kw)
    return lambda *args: call(*deps, *args)


def _pool_fwd(x, pw, scale, g, b, alpha, tm, deps=()):
    T, D = x.shape
    G = len(POOL_WINDOWS)
    Dg = D // G
    H = POOL_HALO
    r = tm // H

    def body(x_ref, xh_ref, pw_ref, sc_ref, g_ref, b_ref, y_ref, z_ref, ext_ref):
        i = pl.program_id(0)
        ext_ref[0:H, :] = jnp.where(i > 0, xh_ref[...], 0.0)
        ext_ref[H:, :] = x_ref[...]
        row = i * tm + lax.broadcasted_iota(jnp.int32, (tm, 1), 0)
        rowf = (row + 1).astype(F32)
        for gi, w in enumerate(POOL_WINDOWS):
            sl = slice(gi * Dg, (gi + 1) * Dg)
            s = ext_ref[:, sl]
            k = 1
            while k < w:
                s = s + pltpu.roll(s, k, 0)
                k *= 2
            inv = 1.0 / jnp.minimum(rowf, float(w))
            xg = x_ref[:, sl]
            d = s[H:, :] * inv - xg
            mix = _mm(d.astype(BF16), pw_ref[gi]) * sc_ref[:, sl]
            z_ref[:, sl] = alpha * xg + mix
        y_ref[...] = _ln_fwd(z_ref[...], g_ref[...], b_ref[...])

    return _tied_call(
        body, deps, name="pool_fwd", grid=(T // tm,),
        out_shape=[jax.ShapeDtypeStruct((T, D), F32)] * 2,
        in_specs=[_rows(tm, D), pl.BlockSpec((H, D), lambda i: (jnp.maximum(i * r - 1, 0), 0)),
                  _full(pw.shape), _full((1, D)), _full((1, D)), _full((1, D))],
        out_specs=[_rows(tm, D)] * 2,
        scratch_shapes=[pltpu.VMEM((tm + H, D), F32)],
        compiler_params=_params("parallel"),
    )(x, x, pw, scale, g, b)


def _mlp_up(name, h, w1b, b1, tm):
    T, D = h.shape
    nb, _, Fs = w1b.shape
    Fd = nb * Fs

    def body(h_ref, w_ref, b_ref, u_ref, a_ref):
        hb = h_ref[...].astype(BF16)
        for j in range(nb):
            sl = slice(j * Fs, (j + 1) * Fs)
            u = _mm(hb, w_ref[j]) + b_ref[:, sl]
            u_ref[:, sl] = u.astype(BF16)
            rl = jnp.maximum(u, 0.0)
            a_ref[:, sl] = (rl * rl).astype(BF16)

    return pl.pallas_call(
        body, name=name, grid=(T // tm,),
        out_shape=[jax.ShapeDtypeStruct((T, Fd), BF16)] * 2,
        in_specs=[_rows(tm, D), _full(w1b.shape), _full((1, Fd))],
        out_specs=[_rows(tm, Fd), _rows(tm, Fd)],
        compiler_params=_params("parallel"),
    )(h, w1b, b1)


def _proj_ln(name, a, w, bias, res, g, b, alpha, tm):
    T, K = a.shape
    D = w.shape[1]

    def body(a_ref, w_ref, bias_ref, res_ref, g_ref, b_ref, y_ref, z_ref):
        z = alpha * res_ref[...] + _mm(a_ref[...], w_ref[...]) + bias_ref[...]
        z_ref[...] = z
        y_ref[...] = _ln_fwd(z, g_ref[...], b_ref[...])

    return pl.pallas_call(
        body, name=name, grid=(T // tm,),
        out_shape=[jax.ShapeDtypeStruct((T, D), F32)] * 2,
        in_specs=[_rows(tm, K), _full((K, D)), _full((1, D)), _rows(tm, D), _full((1, D)), _full((1, D))],
        out_specs=[_rows(tm, D)] * 2,
        compiler_params=_params("parallel"),
    )(a, w, bias, res, g, b)


def _proj_ln_loss(name, a, w, bias, res, g, b, target, alpha, tm):
    T, K = a.shape
    D = w.shape[1]

    def body(a_ref, w_ref, bias_ref, res_ref, g_ref, b_ref, t_ref, dz_ref, dzb_ref, gg_ref, gb_ref, sdz_ref, loss_ref):
        first = pl.program_id(0) == 0
        z = alpha * res_ref[...] + _mm(a_ref[...], w_ref[...]) + bias_ref[...]
        gv = g_ref[...]
        e = _ln_fwd(z, gv, b_ref[...]) - t_ref[...]
        dy = e * (1.0 / D)
        dz, xhat = _ln_bwd(dy, z, gv)
        dz_ref[...] = dz
        dzb_ref[...] = dz.astype(BF16)
        _acc(gg_ref, first, _colsum(dy * xhat))
        _acc(gb_ref, first, _colsum(dy))
        _acc(sdz_ref, first, _colsum(dz))
        _acc(loss_ref, first, _colsum(e * e))

    return pl.pallas_call(
        body, name=name, grid=(T // tm,),
        out_shape=[jax.ShapeDtypeStruct((T, D), F32), jax.ShapeDtypeStruct((T, D), BF16)]
        + [jax.ShapeDtypeStruct((1, D), F32)] * 4,
        in_specs=[_rows(tm, K), _full((K, D)), _full((1, D)), _rows(tm, D), _full((1, D)), _full((1, D)), _rows(tm, D)],
        out_specs=[_rows(tm, D)] * 2 + [_full((1, D))] * 4,
        compiler_params=_params("arbitrary"),
    )(a, w, bias, res, g, b, target)


def _conv_in(x, wb, b_in, tm):
    T, D = x.shape
    nb, _, Ns = wb.shape
    half = nb // 2

    def body(x_ref, w_ref, b_ref, p_ref, glu_ref):
        xb = x_ref[...].astype(BF16)
        for j in range(half):
            sa = slice(j * Ns, (j + 1) * Ns)
            sg = slice(D + j * Ns, D + (j + 1) * Ns)
            a = _mm(xb, w_ref[j]) + b_ref[:, sa]
            gate = _mm(xb, w_ref[half + j]) + b_ref[:, sg]
            p_ref[:, sa] = a
            p_ref[:, sg] = gate
            glu_ref[:, sa] = a * _sigmoid(gate)

    return pl.pallas_call(
        body, name="conv_in", grid=(T // tm,),
        out_shape=[jax.ShapeDtypeStruct((T, 2 * D), F32), jax.ShapeDtypeStruct((T, D), F32)],
        in_specs=[_rows(tm, D), _full(wb.shape), _full((1, 2 * D))],
        out_specs=[_rows(tm, 2 * D), _rows(tm, D)],
        compiler_params=_params("parallel"),
    )(x, wb, b_in)


def _lane_chunk(d):
    return 128 if d % 128 == 0 else d


SUBLANES = 8


def _row_chunk(tm, pref):
    return pref if tm % pref == 0 else tm


def _slabs(ref, base, n_taps, r0, rows, cs):
    out = []
    for r in range(min(SUBLANES, n_taps)):
        nq = (n_taps - 1 - r) // SUBLANES + 1
        lo = base + r + r0
        slab = ref[lo:lo + rows + SUBLANES * (nq - 1), cs]
        out.append((slab, [(q, SUBLANES * q + r) for q in range(nq)]))
    return out


def _dwconv_fwd(glu, dw, dw_b, g, b, tm):
    T, D = glu.shape
    K = dw.shape[0]
    H = CONV_HALO
    off = H - (K - 1)
    r = tm // H
    cc = _lane_chunk(D)

    def body(x_ref, xh_ref, dw_ref, dwb_ref, g_ref, b_ref, cz_ref, s_ref, ext_ref, sh_ref):
        i = pl.program_id(0)
        ext_ref[0:H, :] = jnp.where(i > 0, xh_ref[...], 0.0)
        ext_ref[H:, :] = x_ref[...]
        for ph in range(min(SUBLANES, K)):
            n = tm + SUBLANES * ((K - 1 - ph) // SUBLANES)
            sh_ref[ph, 0:n, :] = ext_ref[off + ph:off + ph + n, :]
        rc = _row_chunk(tm, 128)
        for c0 in range(0, D, cc):
            cs = slice(c0, c0 + cc)
            for r0 in range(0, tm, rc):
                acc = jnp.zeros((rc, cc), F32) + dwb_ref[:, cs]
                for k in range(K):
                    q, ph = divmod(k, SUBLANES)
                    acc = acc + sh_ref[ph, SUBLANES * q + r0:SUBLANES * q + r0 + rc, cs] * dw_ref[k:k + 1, cs]
                cz_ref[r0:r0 + rc, cs] = acc
        ln = _ln_fwd(cz_ref[...], g_ref[...], b_ref[...])
        s_ref[...] = (ln * _sigmoid(ln)).astype(BF16)

    return pl.pallas_call(
        body, name="dwconv_fwd", grid=(T // tm,),
        out_shape=[jax.ShapeDtypeStruct((T, D), F32), jax.ShapeDtypeStruct((T, D), BF16)],
        in_specs=[_rows(tm, D), pl.BlockSpec((H, D), lambda i: (jnp.maximum(i * r - 1, 0), 0)),
                  _full((K, D)), _full((1, D)), _full((1, D)), _full((1, D))],
        out_specs=[_rows(tm, D)] * 2,
        scratch_shapes=[pltpu.VMEM((tm + H, D), F32),
                        pltpu.VMEM((SUBLANES, tm + SUBLANES * ((K - 1) // SUBLANES), D), F32)],
        compiler_params=_params("parallel"),
    )(glu, glu, dw, dw_b, g, b)


def _ln_bwd_store(dy, z_ref, g_ref, first, dz_ref, dzb_ref, gg_ref, gb_ref, sdz_ref):
    dz, xhat = _ln_bwd(dy, z_ref[...], g_ref[...])
    dz_ref[...] = dz
    dzb_ref[...] = dz.astype(BF16)
    _acc(gg_ref, first, _colsum(dy * xhat))
    _acc(gb_ref, first, _colsum(dy))
    _acc(sdz_ref, first, _colsum(dz))


def _ln_bwd_outs(T, D):
    shapes = [jax.ShapeDtypeStruct((T, D), F32), jax.ShapeDtypeStruct((T, D), BF16)] + [jax.ShapeDtypeStruct((1, D), F32)] * 3
    return shapes


def _mlp_act_bwd(name, dzb, u, w2b, tm, deps=()):
    T, D = dzb.shape
    nb, Fs, _ = w2b.shape
    Fd = nb * Fs

    def body(dzb_ref, u_ref, w2_ref, du_ref, gb1_ref):
        first = pl.program_id(0) == 0
        dzv = dzb_ref[...]
        for j in range(nb):
            sl = slice(j * Fs, (j + 1) * Fs)
            du = _mm_nt(dzv, w2_ref[j]) * (2.0 * jnp.maximum(u_ref[:, sl].astype(F32), 0.0))
            du_ref[:, sl] = du.astype(BF16)
            _acc(gb1_ref.at[:, sl], first, _colsum(du))

    return _tied_call(
        body, deps, name=name, grid=(T // tm,),
        out_shape=[jax.ShapeDtypeStruct((T, Fd), BF16), jax.ShapeDtypeStruct((1, Fd), F32)],
        in_specs=[_rows(tm, D), _rows(tm, Fd), _full(w2b.shape)],
        out_specs=[_rows(tm, Fd), _full((1, Fd))],
        compiler_params=_params("arbitrary"),
    )(dzb, u, w2b)


def _wgrad(name, xm, dy, col_blocks, nk, nj, tm):
    T, K = xm.shape
    N = dy.shape[1]
    Kb, Nb = K // nk, N // nj
    nt = T // tm
    if col_blocks:
        per, Ns = N_DEV // nj, N // N_DEV
        out_shape = (N_DEV, K, Ns)
        out_spec = pl.BlockSpec((per, Kb, Ns), lambda k, j, t: (j, k, 0))
    else:
        per, Ks = N_DEV // nk, K // N_DEV
        out_shape = (N_DEV, Ks, N)
        out_spec = pl.BlockSpec((per, Ks, Nb), lambda k, j, t: (k, 0, j))

    def body(x_ref, dy_ref, o_ref, acc_ref):
        t = pl.program_id(2)
        _acc(acc_ref, t == 0, _mm_tn(x_ref[...].astype(BF16), dy_ref[...]))

        @pl.when(t == nt - 1)
        def _():
            for q in range(per):
                if col_blocks:
                    o_ref[q] = acc_ref[:, q * Ns:(q + 1) * Ns].astype(BF16)
                else:
                    o_ref[q] = acc_ref[q * Ks:(q + 1) * Ks, :].astype(BF16)

    return pl.pallas_call(
        body, name=name, grid=(nk, nj, nt),
        out_shape=jax.ShapeDtypeStruct(out_shape, BF16),
        in_specs=[pl.BlockSpec((tm, Kb), lambda k, j, t: (t, k)), pl.BlockSpec((tm, Nb), lambda k, j, t: (t, j))],
        out_specs=out_spec,
        scratch_shapes=[pltpu.VMEM((Kb, Nb), F32)],
        compiler_params=_params("parallel", "parallel", "arbitrary"),
    )(xm, dy)


def _conv_out_bwd(dzb, w_out, cz, g, b, tm, deps=()):
    T, D = cz.shape

    def body(dz_ref, w_ref, cz_ref, g_ref, b_ref, dc_ref, gg_ref, gb_ref, sdc_ref):
        first = pl.program_id(0) == 0
        ds = _mm_nt(dz_ref[...], w_ref[...])
        czv = cz_ref[...]
        gv = g_ref[...]
        ln = _ln_fwd(czv, gv, b_ref[...])
        sg = _sigmoid(ln)
        dln = ds * (sg * (1.0 + ln * (1.0 - sg)))
        dc, xhat = _ln_bwd(dln, czv, gv)
        dc_ref[...] = dc
        _acc(gg_ref, first, _colsum(dln * xhat))
        _acc(gb_ref, first, _colsum(dln))
        _acc(sdc_ref, first, _colsum(dc))

    return _tied_call(
        body, deps, name="conv_out_bwd", grid=(T // tm,),
        out_shape=[jax.ShapeDtypeStruct((T, D), F32)] + [jax.ShapeDtypeStruct((1, D), F32)] * 3,
        in_specs=[_rows(tm, D), _full((D, D)), _rows(tm, D), _full((1, D)), _full((1, D))],
        out_specs=[_rows(tm, D)] + [_full((1, D))] * 3,
        compiler_params=_params("arbitrary"),
    )(dzb, w_out, cz, g, b)


def _dwconv_bwd(dc, glu, p, dw, tm):
    T, D = dc.shape
    K = dw.shape[0]
    Kp = -(-K // 8) * 8
    H = CONV_HALO
    off = H - (K - 1)
    r = tm // H
    last = T // H - 1
    nt = T // tm
    cc = _lane_chunk(D)

    def body(dc_ref, dch_ref, x_ref, xh_ref, p_ref, dw_ref, dp_ref, gdw_ref, gbin_ref, edc_ref, ex_ref, dglu_ref,
             gacc_ref):
        i = pl.program_id(0)
        first = i == 0
        edc_ref[0:tm, :] = dc_ref[...]
        edc_ref[tm:, :] = jnp.where(i < nt - 1, dch_ref[...], 0.0)
        ex_ref[0:H, :] = jnp.where(i > 0, xh_ref[...], 0.0)
        ex_ref[H:, :] = x_ref[...]

        @pl.when(first)
        def _():
            gacc_ref[...] = jnp.zeros_like(gacc_ref)

        rc = _row_chunk(tm, 64)
        for c0 in range(0, D, cc):
            cs = slice(c0, c0 + cc)
            for r0 in range(0, tm, rc):
                dcv = dc_ref[r0:r0 + rc, cs]
                acc = jnp.zeros((rc, cc), F32)
                for slab, taps in _slabs(edc_ref, 0, K, r0, rc, cs):
                    for q, m in taps:
                        acc = acc + slab[SUBLANES * q:SUBLANES * q + rc] * dw_ref[K - 1 - m:K - m, cs]
                dglu_ref[r0:r0 + rc, cs] = acc
                for slab, taps in _slabs(ex_ref, off, K, r0, rc, cs):
                    for q, k in taps:
                        part = (dcv * slab[SUBLANES * q:SUBLANES * q + rc]).reshape(rc // SUBLANES, SUBLANES, cc)
                        gacc_ref[k, :, cs] += jnp.sum(part, axis=0)

        @pl.when(i == nt - 1)
        def _():
            gdw_ref[...] = jnp.zeros_like(gdw_ref)
            gdw_ref[0:K, :] = jnp.sum(gacc_ref[...], axis=1)
        dglu = dglu_ref[...]
        a = p_ref[:, 0:D]
        sg = _sigmoid(p_ref[:, D:2 * D])
        da = dglu * sg
        dgate = dglu * a * (sg * (1.0 - sg))
        dp_ref[:, 0:D] = da.astype(BF16)
        dp_ref[:, D:2 * D] = dgate.astype(BF16)
        _acc(gbin_ref.at[:, 0:D], first, _colsum(da))
        _acc(gbin_ref.at[:, D:2 * D], first, _colsum(dgate))

    return pl.pallas_call(
        body, name="dwconv_bwd", grid=(nt,),
        out_shape=[jax.ShapeDtypeStruct((T, 2 * D), BF16), jax.ShapeDtypeStruct((Kp, D), F32),
                   jax.ShapeDtypeStruct((1, 2 * D), F32)],
        in_specs=[_rows(tm, D), pl.BlockSpec((H, D), lambda i: (jnp.minimum((i + 1) * r, last), 0)),
                  _rows(tm, D), pl.BlockSpec((H, D), lambda i: (jnp.maximum(i * r - 1, 0), 0)),
                  _rows(tm, 2 * D), _full((K, D))],
        out_specs=[_rows(tm, 2 * D), _full((Kp, D)), _full((1, 2 * D))],
        scratch_shapes=[pltpu.VMEM((tm + H, D), F32), pltpu.VMEM((tm + H, D), F32), pltpu.VMEM((tm, D), F32),
                        pltpu.VMEM((K, SUBLANES, D), F32)],
        compiler_params=_params("arbitrary"),
    )(dc, dc, glu, glu, p, dw)


def _dx_proj(name, dz, dy, wb, z_in, g_in, alpha, tm, deps=()):
    T, D = dz.shape
    nb, _, Ns = wb.shape
    N = nb * Ns

    def body(dz_ref, dy_ref, w_ref, zin_ref, gin_ref, dzo_ref, dzob_ref, gg_ref, gb_ref, sdz_ref):
        dx = alpha * dz_ref[...]
        for j in range(nb):
            dx = dx + _mm_nt(dy_ref[:, j * Ns:(j + 1) * Ns], w_ref[j])
        _ln_bwd_store(dx, zin_ref, gin_ref, pl.program_id(0) == 0, dzo_ref, dzob_ref, gg_ref, gb_ref, sdz_ref)

    return _tied_call(
        body, deps, name=name, grid=(T // tm,),
        out_shape=_ln_bwd_outs(T, D),
        in_specs=[_rows(tm, D), _rows(tm, N), _full(wb.shape), _rows(tm, D), _full((1, D))],
        out_specs=[_rows(tm, D)] * 2 + [_full((1, D))] * 3,
        compiler_params=_params("arbitrary"),
    )(dz, dy, wb, z_in, g_in)


def _pool_bwd(dz, x, pw, scale, alpha, tm, deps=()):
    T, D = x.shape
    G = len(POOL_WINDOWS)
    Dg = D // G
    H = POOL_HALO
    r = tm // H
    last = T // H - 1
    nt = T // tm
    n_ext = tm + H

    def body(dz_ref, dzh_ref, x_ref, xh_ref, pw_ref, sc_ref, dx_ref, gpw_ref, gsc_ref, edz_ref, ex_ref):
        i = pl.program_id(0)
        first = i == 0
        edz_ref[0:tm, :] = dz_ref[...]
        edz_ref[tm:, :] = jnp.where(i < nt - 1, dzh_ref[...], 0.0)
        ex_ref[0:H, :] = jnp.where(i > 0, xh_ref[...], 0.0)
        ex_ref[H:, :] = x_ref[...]
        row = i * tm + lax.broadcasted_iota(jnp.int32, (tm, 1), 0)
        rowf = (row + 1).astype(F32)
        erow = i * tm + lax.broadcasted_iota(jnp.int32, (n_ext, 1), 0)
        erowf = (erow + 1).astype(F32)
        for gi, w in enumerate(POOL_WINDOWS):
            sl = slice(gi * Dg, (gi + 1) * Dg)
            s = ex_ref[:, sl]
            k = 1
            while k < w:
                s = s + pltpu.roll(s, k, 0)
                k *= 2
            xg = x_ref[:, sl]
            d = (s[H:, :] * (1.0 / jnp.minimum(rowf, float(w))) - xg).astype(BF16)
            wg = pw_ref[gi]
            premix = _mm(d, wg)
            dzg = dz_ref[:, sl]
            _acc(gsc_ref.at[:, sl], first, _colsum(dzg * premix))
            dpre = edz_ref[:, sl] * sc_ref[:, sl]
            dpre_b = dpre.astype(BF16)
            _acc(gpw_ref.at[gi], first, _mm_tn(d, dpre_b[0:tm, :]))
            dd = _mm_nt(dpre_b, wg)
            e = dd * (1.0 / jnp.minimum(erowf, float(w)))
            k = 1
            while k < w:
                e = e + pltpu.roll(e, n_ext - k, 0)
                k *= 2
            dx_ref[:, sl] = alpha * dzg + e[0:tm, :] - dd[0:tm, :]

    return _tied_call(
        body, deps, name="pool_bwd", grid=(nt,),
        out_shape=[jax.ShapeDtypeStruct((T, D), F32), jax.ShapeDtypeStruct((G, Dg, Dg), F32),
                   jax.ShapeDtypeStruct((1, D), F32)],
        in_specs=[_rows(tm, D), pl.BlockSpec((H, D), lambda i: (jnp.minimum((i + 1) * r, last), 0)),
                  _rows(tm, D), pl.BlockSpec((H, D), lambda i: (jnp.maximum(i * r - 1, 0), 0)),
                  _full(pw.shape), _full((1, D))],
        out_specs=[_rows(tm, D), _full((G, Dg, Dg)), _full((1, D))],
        scratch_shapes=[pltpu.VMEM((n_ext, D), F32), pltpu.VMEM((n_ext, D), F32)],
        compiler_params=_params("arbitrary"),
    )(dz, dz, x, x, pw, scale)


def _adamw(name, recv, w, m, v, tm, layer=None, prev=None):
    R, C = w.shape[-2:]
    c1 = 1.0 - ADAM_B1 ** ADAM_STEP
    c2 = 1.0 - ADAM_B2 ** ADAM_STEP
    if layer is None:
        spec = _rows(tm, C)
    else:
        spec = pl.BlockSpec((None, tm, C), lambda i: (layer, i, 0))
    prev = list(prev) if prev is not None else []

    def body(r_ref, w_ref, m_ref, v_ref, *rest):
        g_ref, d_ref, nm_ref, nv_ref = rest[len(prev):]
        g = r_ref[0].astype(F32)
        for s in range(1, N_DEV):
            g = g + r_ref[s].astype(F32)
        m1 = ADAM_B1 * m_ref[...] + (1.0 - ADAM_B1) * g
        v1 = ADAM_B2 * v_ref[...] + (1.0 - ADAM_B2) * (g * g)
        m_hat = m1 / c1
        v_hat = v1 / c2
        g_ref[...] = g
        d_ref[...] = -ADAM_LR * (m_hat / (jnp.sqrt(v_hat) + ADAM_EPS) + ADAM_WD * w_ref[...])
        nm_ref[...] = m1
        nv_ref[...] = v1

    return pl.pallas_call(
        body, name=name, grid=(R // tm,),
        out_shape=[jax.ShapeDtypeStruct(w.shape, F32)] * 4,
        in_specs=[pl.BlockSpec((N_DEV, tm, C), lambda i: (0, i, 0))] + [spec] * 3
        + [pl.BlockSpec(memory_space=pl.ANY)] * len(prev),
        out_specs=[spec] * 4,
        input_output_aliases={4 + j: j for j in range(len(prev))},
        compiler_params=_params("parallel"),
    )(recv, w, m, v, *prev)


def _adamw_small(name, recv, w, m, v, pieces):
    R, C = w.shape
    c1 = 1.0 - ADAM_B1 ** ADAM_STEP
    c2 = 1.0 - ADAM_B2 ** ADAM_STEP
    n = len(pieces)

    def body(r_ref, w_ref, m_ref, v_ref, *rest):
        outs, packed = rest[:4 * n], rest[4 * n]
        g = r_ref[0]
        for s in range(1, N_DEV):
            g = g + r_ref[s]
        m1 = ADAM_B1 * m_ref[...] + (1.0 - ADAM_B1) * g
        v1 = ADAM_B2 * v_ref[...] + (1.0 - ADAM_B2) * (g * g)
        packed[0] = g
        packed[1] = -ADAM_LR * ((m1 / c1) / (jnp.sqrt(v1 / c2) + ADAM_EPS) + ADAM_WD * w_ref[...])
        packed[2] = m1
        packed[3] = v1
        for kind in range(4):
            for p, (r0, rows, shape) in enumerate(pieces):
                o_ref = outs[kind * n + p]
                if shape[-1] == C:
                    o_ref[...] = packed[kind, r0:r0 + rows, :].reshape(shape)
                else:
                    per = shape[-1] // C
                    for idx in range(rows):
                        l, q = divmod(idx, per)
                        o_ref[l:l + 1, q * C:(q + 1) * C] = packed[kind, r0 + idx:r0 + idx + 1, :]

    outs = pl.pallas_call(
        body, name=name,
        out_shape=[jax.ShapeDtypeStruct(shape, F32) for _ in range(4) for _, _, shape in pieces],
        scratch_shapes=[pltpu.VMEM((4, R, C), F32)],
    )(recv, w, m, v)
    return [outs[k * n:(k + 1) * n] for k in range(4)]


def _pad_rows(a, rows):
    return jnp.pad(a, ((0, rows - a.shape[0]), (0, 0)))


def kernel(x, pool_w, pool_scale, conv_w_in, conv_b_in, conv_dw, conv_dw_b, conv_ln_g, conv_ln_b, conv_w_out, conv_b_out, mix_ln_g, mix_ln_b, mlp_w1, mlp_b1, mlp_w2, mlp_b2, mlp_ln_g, mlp_ln_b, loss_target, m_pool_w, m_pool_scale, m_conv_w_in, m_conv_b_in, m_conv_dw, m_conv_dw_b, m_conv_ln_g, m_conv_ln_b, m_conv_w_out, m_conv_b_out, m_mix_ln_g, m_mix_ln_b, m_mlp_w1, m_mlp_b1, m_mlp_w2, m_mlp_b2, m_mlp_ln_g, m_mlp_ln_b, v_pool_w, v_pool_scale, v_conv_w_in, v_conv_b_in, v_conv_dw, v_conv_dw_b, v_conv_ln_g, v_conv_ln_b, v_conv_w_out, v_conv_b_out, v_mix_ln_g, v_mix_ln_b, v_mlp_w1, v_mlp_b1, v_mlp_w2, v_mlp_b2, v_mlp_ln_g, v_mlp_ln_b):
    _, T, D = x.shape
    L = mlp_w1.shape[0]
    assert L == 2 and pool_w.shape[0] == 1 and conv_w_in.shape[0] == 1
    G = pool_w.shape[1]
    Dg = D // G
    Fd = mlp_b1.shape[1]
    Fs = Fd // N_DEV
    Kc = conv_dw.shape[1]
    Dc = D // N_DEV
    alpha = float((2.0 * L) ** 0.25)
    x2d, tgt = x[0], loss_target[0]

    tm = _tile(T, 512)
    tm_wide = _tile(T, 512)
    tm_conv = _tile(T, 256)
    tm_wg = _tile(T, 1024)

    def pack_sh(dw, dwb, lg, lb, bo, bi):
        rows = jnp.concatenate([dw[0], dwb, lg, lb, bo, bi.reshape(2, Dc)], axis=0)
        return _pad_rows(rows, SH_ROWS)

    SH_ROWS = -(-(Kc + 6) // 8) * 8
    def pack_rep(ps, mg, mb, b1, b2, lg, lb):
        rows = jnp.concatenate([ps, mg, mb, b1.reshape(L * Fd // D, D), b2, lg, lb], axis=0)
        return _pad_rows(rows, REP_ROWS)

    n_rep = 1 + 2 * L + L * Fd // D + 3 * L
    REP_ROWS = -(-n_rep // 8) * 8

    w_sh = pack_sh(conv_dw, conv_dw_b, conv_ln_g, conv_ln_b, conv_b_out, conv_b_in)
    m_sh = pack_sh(m_conv_dw, m_conv_dw_b, m_conv_ln_g, m_conv_ln_b, m_conv_b_out, m_conv_b_in)
    v_sh = pack_sh(v_conv_dw, v_conv_dw_b, v_conv_ln_g, v_conv_ln_b, v_conv_b_out, v_conv_b_in)
    w_rep = pack_rep(pool_scale, mix_ln_g, mix_ln_b, mlp_b1, mlp_b2, mlp_ln_g, mlp_ln_b)
    m_rep = pack_rep(m_pool_scale, m_mix_ln_g, m_mix_ln_b, m_mlp_b1, m_mlp_b2, m_mlp_ln_g, m_mlp_ln_b)
    v_rep = pack_rep(v_pool_scale, v_mix_ln_g, v_mix_ln_b, v_mlp_b1, v_mlp_b2, v_mlp_ln_g, v_mlp_ln_b)

    groups = [[pool_w[0]], [mlp_w1[0]], [mlp_w2[0]], [conv_w_in[0], conv_w_out[0]], [mlp_w1[1]], [mlp_w2[1]]]
    handles, tokens = [], []
    for i, grp in enumerate(groups):
        tie = tokens[-1][0, 0] if tokens else 0.0
        srcs = [(a + tie).astype(BF16) for a in grp] + ([w_sh] if i == 0 else [])
        h, tk = _xstart("gather_%d" % i, srcs, [False] * len(srcs))
        handles.append(h)
        tokens.append(tk)
    pw_all, sh_all = _xwait(handles[0], tokens[-1])
    pw = pw_all.transpose(1, 0, 2, 3).reshape(G, Dg, Dg)
    dw_full = sh_all[:, 0:Kc].transpose(1, 0, 2).reshape(Kc, D)

    def sh_row(i):
        return sh_all[:, i].reshape(1, D)

    dwb_full, cg_full, cb_full, bout_full = (sh_row(Kc + i) for i in range(4))
    bin_full = sh_all[:, Kc + 4:Kc + 6].reshape(1, 2 * D)

    h0, z_m0 = _pool_fwd(x2d, pw, pool_scale, mix_ln_g[0:1], mix_ln_b[0:1], alpha, tm, tuple(tokens))
    (w1b0,) = _xwait(handles[1], h0)
    u0, a0 = _mlp_up("mlp_up0", h0, w1b0, mlp_b1[0:1], tm)
    (w2b0,) = _xwait(handles[2], a0)
    x1, z_f0 = _proj_ln("mlp_down0", a0, w2b0.reshape(Fd, D), mlp_b2[0:1], h0, mlp_ln_g[0:1], mlp_ln_b[0:1],
                        alpha, tm)
    win_b, wout_all = _xwait(handles[3], x1)
    w_out = wout_all.reshape(D, D)
    p, glu = _conv_in(x1, win_b, bin_full, tm)
    cz, s = _dwconv_fwd(glu, dw_full, dwb_full, cg_full, cb_full, tm_conv)
    h1, z_m1 = _proj_ln("conv_out", s, w_out, bout_full, x1, mix_ln_g[1:2], mix_ln_b[1:2], alpha, tm)
    (w1b1,) = _xwait(handles[4], h1)
    u1, a1 = _mlp_up("mlp_up1", h1, w1b1, mlp_b1[1:2], tm)
    (w2b1,) = _xwait(handles[5], a1)

    dz, dzb, g_fg1, g_fb1, g_b2_1, loss_cols = _proj_ln_loss(
        "mlp_down1", a1, w2b1.reshape(Fd, D), mlp_b2[1:2], h1, mlp_ln_g[1:2], mlp_ln_b[1:2], tgt, alpha, tm)
    gw2_1 = _wgrad("gw2_1", a1, dzb, False, 4, 1, tm_wg)
    e_w2_1, tk = _xstart("grads_w2_1", [gw2_1], [True])
    du, g_b1_1 = _mlp_act_bwd("mlp_act_bwd1", dzb, u1, w2b1, tm, (tk,))
    dz, dzb, g_mg1, g_mb1, g_bout = _dx_proj("mlp_in_bwd1", dz, du, w1b1, z_m1, mix_ln_g[1:2], alpha, tm_wide)
    gw1_1 = _wgrad("gw1_1", h1, du, True, 1, 2, tm_wg)
    e_w1_1, tk = _xstart("grads_w1_1", [gw1_1], [True])
    gwout = _wgrad("gw_out", s, dzb, False, 1, 1, tm_wg)
    dc, g_cg, g_cb, g_dwb = _conv_out_bwd(dzb, w_out, cz, cg_full, cb_full, tm, (tk,))
    dp, g_dw, g_bin = _dwconv_bwd(dc, glu, p, dw_full, tm_conv)
    gwin = _wgrad("gw_in", x1, dp, True, 1, 2, tm_wg)
    e_conv, tk = _xstart("grads_conv", [gwin, gwout], [True, True])
    dz, dzb, g_fg0, g_fb0, g_b2_0 = _dx_proj("conv_in_bwd", dz, dp, win_b, z_f0, mlp_ln_g[0:1], alpha, tm, (tk,))
    gw2_0 = _wgrad("gw2_0", a0, dzb, False, 4, 1, tm_wg)
    e_w2_0, tk = _xstart("grads_w2_0", [gw2_0], [True])
    du, g_b1_0 = _mlp_act_bwd("mlp_act_bwd0", dzb, u0, w2b0, tm, (tk,))
    dz, dzb, g_mg0, g_mb0, _ = _dx_proj("mlp_in_bwd0", dz, du, w1b0, z_m0, mix_ln_g[0:1], alpha, tm_wide)
    gw1_0 = _wgrad("gw1_0", h0, du, True, 1, 2, tm_wg)
    e_w1_0, tk = _xstart("grads_w1_0", [gw1_0], [True])
    grad_x, g_pw, g_ps = _pool_bwd(dz, x2d, pw, pool_scale, alpha, tm, (tk,))

    loss = lax.psum(0.5 / D * jnp.sum(loss_cols), MESH_AXES)

    gpw_b = g_pw.reshape(G, N_DEV, Dg // N_DEV, Dg).transpose(1, 0, 2, 3).astype(BF16)

    def to_dev(vec, rows):
        return vec.reshape(rows, N_DEV, Dc).transpose(1, 0, 2)

    g_sh = jnp.concatenate(
        [to_dev(g_dw[0:Kc], Kc), to_dev(g_dwb, 1), to_dev(g_cg, 1), to_dev(g_cb, 1), to_dev(g_bout, 1),
         g_bin.reshape(N_DEV, 2, Dc), jnp.zeros((N_DEV, SH_ROWS - Kc - 6, Dc), F32)], axis=1)
    g_rep = _pad_rows(jnp.concatenate(
        [g_ps, g_mg0, g_mg1, g_mb0, g_mb1, g_b1_0.reshape(Fd // D, D), g_b1_1.reshape(Fd // D, D),
         g_b2_0, g_b2_1, g_fg0, g_fg1, g_fb0, g_fb1], axis=0), REP_ROWS)

    e_small, tk = _xstart("grads_small", [gpw_b, g_sh, g_rep], [True, True, False])

    def upd(name, recv, w, m, v):
        shape = w.shape
        C = shape[-1]
        R = w.size // C
        outs = _adamw(name, recv.reshape(N_DEV, R, C), w.reshape(R, C), m.reshape(R, C), v.reshape(R, C), _tile(R, 256))
        return [o.reshape(shape) for o in outs]

    def upd_layer(name, recv, w, m, v, layer, prev):
        return _adamw(name, recv, w, m, v, _tile(w.shape[1], 256), layer, prev)

    (r_w2_1,) = _xwait(e_w2_1, tk)
    o_w2 = upd_layer("adam_w2_1", r_w2_1, mlp_w2, m_mlp_w2, v_mlp_w2, 1, None)
    (r_w1_1,) = _xwait(e_w1_1, o_w2[0])
    o_w1 = upd_layer("adam_w1_1", r_w1_1, mlp_w1, m_mlp_w1, v_mlp_w1, 1, None)
    r_win, r_wout = _xwait(e_conv, o_w1[0])
    o_win = upd("adam_w_in", r_win, conv_w_in, m_conv_w_in, v_conv_w_in)
    o_wout = upd("adam_w_out", r_wout, conv_w_out, m_conv_w_out, v_conv_w_out)
    (r_w2_0,) = _xwait(e_w2_0, o_wout[0])
    o_w2 = upd_layer("adam_w2_0", r_w2_0, mlp_w2, m_mlp_w2, v_mlp_w2, 0, o_w2)
    (r_w1_0,) = _xwait(e_w1_0, o_w2[0])
    o_w1 = upd_layer("adam_w1_0", r_w1_0, mlp_w1, m_mlp_w1, v_mlp_w1, 0, o_w1)
    r_pw, r_sh, r_rep = _xwait(e_small, o_w1[0])
    o_pw = upd("adam_pool_w", r_pw, pool_w, m_pool_w, v_pool_w)
    sh_pieces = [(0, Kc, (1, Kc, Dc))] + [(Kc + i, 1, (1, Dc)) for i in range(4)] + [(Kc + 4, 2, (1, 2 * Dc))]
    rep_pieces, o = [], 0
    for rows, shape in ((1, (1, D)), (L, (L, D)), (L, (L, D)), (L * Fd // D, (L, Fd)), (L, (L, D)), (L, (L, D)),
                        (L, (L, D))):
        rep_pieces.append((o, rows, shape))
        o += rows
    o_sh = _adamw_small("adam_conv_vec", r_sh, w_sh, m_sh, v_sh, sh_pieces)
    o_rep = _adamw_small("adam_replicated", r_rep, w_rep, m_rep, v_rep, rep_pieces)

    results = []
    for kind in range(4):
        dwv, dwb, lg, lb, bo, bi = o_sh[kind]
        ps, mg, mb, b1, b2, fg, fb = o_rep[kind]
        results.append([o_pw[kind], ps, o_win[kind], bi, dwv, dwb, lg, lb, o_wout[kind], bo, mg, mb,
                        o_w1[kind], b1, o_w2[kind], b2, fg, fb])
    return (loss, grad_x[None], *results[0], *results[1], *results[2], *results[3])
```

```python
import jax
import jax.numpy as jnp
from jax import lax
from jax.experimental import pallas as pl
from jax.experimental.pallas import tpu as pltpu

N_DEV = 8
MESH_AXES = ("x", "y", "c")
POOL_WINDOWS = (2, 4, 8, 16)
POOL_HALO = 16
CONV_HALO = 32
LN_EPS = 1e-5
ADAM_LR = 0.001
ADAM_B1 = 0.9
ADAM_B2 = 0.999
ADAM_EPS = 1e-08
ADAM_WD = 0.01
ADAM_STEP = 10
VMEM_LIMIT = 56 * 1024 * 1024

F32 = jnp.float32
BF16 = jnp.bfloat16


def _mm(a, b):
    return lax.dot_general(a, b, (((1,), (0,)), ((), ())), preferred_element_type=F32)


def _mm_nt(a, b):
    return lax.dot_general(a, b, (((1,), (1,)), ((), ())), preferred_element_type=F32)


def _mm_tn(a, b):
    return lax.dot_general(a, b, (((0,), (0,)), ((), ())), preferred_element_type=F32)


def _tile(n, pref):
    t = min(n, pref)
    assert n % t == 0, (n, pref)
    return t


def _params(*sem):
    return pltpu.CompilerParams(dimension_semantics=sem, vmem_limit_bytes=VMEM_LIMIT)


def _full(shape):
    nd = len(shape)
    return pl.BlockSpec(shape, lambda *_: (0,) * nd)


def _rows(tm, d):
    return pl.BlockSpec((tm, d), lambda i: (i, 0))


def _ln_stats(z):
    mu = jnp.mean(z, axis=-1, keepdims=True)
    zc = z - mu
    var = jnp.mean(zc * zc, axis=-1, keepdims=True)
    rstd = lax.rsqrt(var + LN_EPS)
    return zc * rstd, rstd


def _ln_fwd(z, g, b):
    xhat, _ = _ln_stats(z)
    return xhat * g + b


def _ln_bwd(dy, z, g):
    xhat, rstd = _ln_stats(z)
    dxh = dy * g
    m1 = jnp.mean(dxh, axis=-1, keepdims=True)
    m2 = jnp.mean(dxh * xhat, axis=-1, keepdims=True)
    return rstd * (dxh - m1 - xhat * m2), xhat


def _colsum(v):
    return jnp.sum(v, axis=0, keepdims=True)


def _sigmoid(v):
    return 1.0 / (1.0 + jnp.exp(-v))


def _acc(ref, first, val):
    @pl.when(first)
    def _():
        ref[...] = val

    @pl.when(jnp.logical_not(first))
    def _():
        ref[...] += val


_HBM = pl.BlockSpec(memory_space=pltpu.HBM)
_SEM = pl.BlockSpec(memory_space=pltpu.SEMAPHORE)
_EFFECT = pltpu.SideEffectType.DATAFLOW_SIDE_EFFECTING


def _peers():
    x, y, c = (lax.axis_index(a) for a in MESH_AXES)
    out = []
    for d in range(1, N_DEV):
        px = (x + ((d >> 2) & 1)) % 2
        py = (y + ((d >> 1) & 1)) % 2
        pc = (c + (d & 1)) % 2
        out.append((d - 1, (px, py, pc), 4 * px + 2 * py + pc))
    return 4 * x + 2 * y + c, out


def _remote_copies(src_refs, land_refs, scatter, send_sems, recv_sems):
    me, peers = _peers()
    copies = []
    for i, pos, pid in peers:
        for k, (src, land) in enumerate(zip(src_refs, land_refs)):
            copies.append(pltpu.make_async_remote_copy(
                src_ref=src.at[pid] if scatter[k] else src, dst_ref=land.at[me],
                send_sem=send_sems.at[k * (N_DEV - 1) + i], recv_sem=recv_sems.at[k * (N_DEV - 1) + i],
                device_id=pos, device_id_type=pl.DeviceIdType.MESH))
    return copies


def _xstart(name, srcs, scatter):
    n = len(srcs)
    me = 4 * lax.axis_index("x") + 2 * lax.axis_index("y") + lax.axis_index("c")
    lands = []
    for s, sc in zip(srcs, scatter):
        own = lax.dynamic_index_in_dim(s, me, 0, keepdims=True) if sc else s[None]
        shape = s.shape if sc else (N_DEV,) + s.shape
        lands.append(lax.dynamic_update_slice(lax.empty(shape, s.dtype), own, (me,) + (0,) * (len(shape) - 1)))

    def body(*refs):
        src_refs, land_refs = refs[:n], refs[n:2 * n]
        send_sems, recv_sems = refs[2 * n], refs[2 * n + 1]
        token = refs[-1]
        for cp in _remote_copies(src_refs, land_refs, scatter, send_sems, recv_sems):
            cp.start()
        token[...] = jnp.zeros_like(token)

    outs = pl.pallas_call(
        body, name=name,
        out_shape=(pltpu.SemaphoreType.DMA((n * (N_DEV - 1),)), pltpu.SemaphoreType.DMA((n * (N_DEV - 1),)),
                   *[pltpu.HBM(a.shape, a.dtype) for a in srcs + lands], jax.ShapeDtypeStruct((8, 128), F32)),
        in_specs=(_HBM,) * (2 * n),
        out_specs=(_SEM, _SEM) + (_HBM,) * (2 * n) + (pl.BlockSpec(memory_space=pltpu.VMEM),),
        input_output_aliases={i: 2 + i for i in range(2 * n)},
        compiler_params=pltpu.CompilerParams(has_side_effects=_EFFECT),
    )(*[pltpu.with_memory_space_constraint(a, pltpu.HBM) for a in srcs + lands])
    return (name, scatter, outs[0], outs[1], outs[2:2 + n], outs[2 + n:2 + 2 * n]), outs[-1]


def _xwait(handle, after):
    name, scatter, send_sems, recv_sems, srcs, lands = handle
    n = len(srcs)

    def body(*refs):
        src_refs, land_refs = refs[:n], refs[n:2 * n]
        send, recv = refs[2 * n], refs[2 * n + 1]
        copies = _remote_copies(src_refs, land_refs, scatter, send, recv)
        for cp in copies:
            cp.wait_send()
        for cp in copies:
            cp.wait_recv()

    outs = pl.pallas_call(
        body, name=name + "_wait",
        out_shape=tuple(pltpu.HBM(a.shape, a.dtype) for a in (*srcs, *lands)),
        in_specs=(_HBM,) * (2 * n) + (_SEM, _SEM, pl.BlockSpec(memory_space=pl.ANY)),
        out_specs=(_HBM,) * (2 * n),
        input_output_aliases={i: i for i in range(2 * n)},
        compiler_params=pltpu.CompilerParams(has_side_effects=_EFFECT),
    )(*srcs, *lands, send_sems, recv_sems, after)
    return outs[n:]


def _tied_call(body, deps, in_specs, **kw):
    nd = len(deps)

    def tied_body(*refs):
        body(*refs[nd:])

    call = pl.pallas_call(tied_body, in_specs=[pl.BlockSpec(memory_space=pl.ANY)] * nd + list(in_specs), **kw)
    return lambda *args: call(*deps, *args)


def _pool_fwd(x, pw, scale, g, b, alpha, tm, deps=()):
    T, D = x.shape
    G = len(POOL_WINDOWS)
    Dg = D // G
    H = POOL_HALO
    r = tm // H

    def body(x_ref, xh_ref, pw_ref, sc_ref, g_ref, b_ref, y_ref, z_ref, ext_ref):
        i = pl.program_id(0)
        ext_ref[0:H, :] = jnp.where(i > 0, xh_ref[...], 0.0)
        ext_ref[H:, :] = x_ref[...]
        row = i * tm + lax.broadcasted_iota(jnp.int32, (tm, 1), 0)
        rowf = (row + 1).astype(F32)
        for gi, w in enumerate(POOL_WINDOWS):
            sl = slice(gi * Dg, (gi + 1) * Dg)
            s = ext_ref[:, sl]
            k = 1
            while k < w:
                s = s + pltpu.roll(s, k, 0)
                k *= 2
            inv = 1.0 / jnp.minimum(rowf, float(w))
            xg = x_ref[:, sl]
            d = s[H:, :] * inv - xg
            mix = _mm(d.astype(BF16), pw_ref[gi]) * sc_ref[:, sl]
            z_ref[:, sl] = alpha * xg + mix
        y_ref[...] = _ln_fwd(z_ref[...], g_ref[...], b_ref[...])

    return _tied_call(
        body, deps, name="pool_fwd", grid=(T // tm,),
        out_shape=[jax.ShapeDtypeStruct((T, D), F32)] * 2,
        in_specs=[_rows(tm, D), pl.BlockSpec((H, D), lambda i: (jnp.maximum(i * r - 1, 0), 0)),
                  _full(pw.shape), _full((1, D)), _full((1, D)), _full((1, D))],
        out_specs=[_rows(tm, D)] * 2,
        scratch_shapes=[pltpu.VMEM((tm + H, D), F32)],
        compiler_params=_params("parallel"),
    )(x, x, pw, scale, g, b)


def _mlp_up(name, h, w1b, b1, tm):
    T, D = h.shape
    nb, _, Fs = w1b.shape
    Fd = nb * Fs

    def body(h_ref, w_ref, b_ref, u_ref):
        hb = h_ref[...].astype(BF16)
        for j in range(nb):
            sl = slice(j * Fs, (j + 1) * Fs)
            u_ref[:, sl] = (_mm(hb, w_ref[j]) + b_ref[:, sl]).astype(BF16)

    return pl.pallas_call(
        body, name=name, grid=(T // tm,),
        out_shape=jax.ShapeDtypeStruct((T, Fd), BF16),
        in_specs=[_rows(tm, D), _full(w1b.shape), _full((1, Fd))],
        out_specs=_rows(tm, Fd),
        compiler_params=_params("parallel"),
    )(h, w1b, b1)


def _sq_relu(u):
    r = jnp.maximum(u, 0.0)
    return r * r


def _proj_ln(name, a, w, bias, res, g, b, alpha, tm, act=False):
    T, K = a.shape
    D = w.shape[1]

    def body(a_ref, w_ref, bias_ref, res_ref, g_ref, b_ref, y_ref, z_ref):
        av = _sq_relu(a_ref[...]) if act else a_ref[...]
        z = alpha * res_ref[...] + _mm(av, w_ref[...]) + bias_ref[...]
        z_ref[...] = z
        y_ref[...] = _ln_fwd(z, g_ref[...], b_ref[...])

    return pl.pallas_call(
        body, name=name, grid=(T // tm,),
        out_shape=[jax.ShapeDtypeStruct((T, D), F32)] * 2,
        in_specs=[_rows(tm, K), _full((K, D)), _full((1, D)), _rows(tm, D), _full((1, D)), _full((1, D))],
        out_specs=[_rows(tm, D)] * 2,
        compiler_params=_params("parallel"),
    )(a, w, bias, res, g, b)


def _proj_ln_loss(name, a, w, bias, res, g, b, target, alpha, tm):
    T, K = a.shape
    D = w.shape[1]

    def body(a_ref, w_ref, bias_ref, res_ref, g_ref, b_ref, t_ref, dz_ref, dzb_ref, gg_ref, gb_ref, sdz_ref, loss_ref):
        first = pl.program_id(0) == 0
        z = alpha * res_ref[...] + _mm(_sq_relu(a_ref[...]), w_ref[...]) + bias_ref[...]
        gv = g_ref[...]
        e = _ln_fwd(z, gv, b_ref[...]) - t_ref[...]
        dy = e * (1.0 / D)
        dz, xhat = _ln_bwd(dy, z, gv)
        dz_ref[...] = dz
        dzb_ref[...] = dz.astype(BF16)
        _acc(gg_ref, first, _colsum(dy * xhat))
        _acc(gb_ref, first, _colsum(dy))
        _acc(sdz_ref, first, _colsum(dz))
        _acc(loss_ref, first, _colsum(e * e))

    return pl.pallas_call(
        body, name=name, grid=(T // tm,),
        out_shape=[jax.ShapeDtypeStruct((T, D), F32), jax.ShapeDtypeStruct((T, D), BF16)]
        + [jax.ShapeDtypeStruct((1, D), F32)] * 4,
        in_specs=[_rows(tm, K), _full((K, D)), _full((1, D)), _rows(tm, D), _full((1, D)), _full((1, D)), _rows(tm, D)],
        out_specs=[_rows(tm, D)] * 2 + [_full((1, D))] * 4,
        compiler_params=_params("arbitrary"),
    )(a, w, bias, res, g, b, target)


def _conv_in(x, wb, b_in, tm):
    T, D = x.shape
    nb, _, Ns = wb.shape
    half = nb // 2

    def body(x_ref, w_ref, b_ref, p_ref, glu_ref):
        xb = x_ref[...].astype(BF16)
        for j in range(half):
            sa = slice(j * Ns, (j + 1) * Ns)
            sg = slice(D + j * Ns, D + (j + 1) * Ns)
            a = _mm(xb, w_ref[j]) + b_ref[:, sa]
            gate = _mm(xb, w_ref[half + j]) + b_ref[:, sg]
            p_ref[:, sa] = a
            p_ref[:, sg] = gate
            glu_ref[:, sa] = a * _sigmoid(gate)

    return pl.pallas_call(
        body, name="conv_in", grid=(T // tm,),
        out_shape=[jax.ShapeDtypeStruct((T, 2 * D), F32), jax.ShapeDtypeStruct((T, D), F32)],
        in_specs=[_rows(tm, D), _full(wb.shape), _full((1, 2 * D))],
        out_specs=[_rows(tm, 2 * D), _rows(tm, D)],
        compiler_params=_params("parallel"),
    )(x, wb, b_in)


def _lane_chunk(d):
    return 128 if d % 128 == 0 else d


SUBLANES = 8


def _row_chunk(tm, pref):
    return pref if tm % pref == 0 else tm


def _slabs(ref, base, n_taps, r0, rows, cs):
    out = []
    for r in range(min(SUBLANES, n_taps)):
        nq = (n_taps - 1 - r) // SUBLANES + 1
        lo = base + r + r0
        slab = ref[lo:lo + rows + SUBLANES * (nq - 1), cs]
        out.append((slab, [(q, SUBLANES * q + r) for q in range(nq)]))
    return out


def _dwconv_fwd(glu, dw, dw_b, g, b, tm):
    T, D = glu.shape
    K = dw.shape[0]
    H = CONV_HALO
    off = H - (K - 1)
    r = tm // H
    cc = _lane_chunk(D)

    def body(x_ref, xh_ref, dw_ref, dwb_ref, g_ref, b_ref, cz_ref, s_ref, ext_ref, sh_ref):
        i = pl.program_id(0)
        ext_ref[0:H, :] = jnp.where(i > 0, xh_ref[...], 0.0)
        ext_ref[H:, :] = x_ref[...]
        for ph in range(min(SUBLANES, K)):
            n = tm + SUBLANES * ((K - 1 - ph) // SUBLANES)
            sh_ref[ph, 0:n, :] = ext_ref[off + ph:off + ph + n, :]
        rc = _row_chunk(tm, 128)
        for c0 in range(0, D, cc):
            cs = slice(c0, c0 + cc)
            for r0 in range(0, tm, rc):
                acc = jnp.zeros((rc, cc), F32) + dwb_ref[:, cs]
                for k in range(K):
                    q, ph = divmod(k, SUBLANES)
                    acc = acc + sh_ref[ph, SUBLANES * q + r0:SUBLANES * q + r0 + rc, cs] * dw_ref[k:k + 1, cs]
                cz_ref[r0:r0 + rc, cs] = acc
        ln = _ln_fwd(cz_ref[...], g_ref[...], b_ref[...])
        s_ref[...] = (ln * _sigmoid(ln)).astype(BF16)

    return pl.pallas_call(
        body, name="dwconv_fwd", grid=(T // tm,),
        out_shape=[jax.ShapeDtypeStruct((T, D), F32), jax.ShapeDtypeStruct((T, D), BF16)],
        in_specs=[_rows(tm, D), pl.BlockSpec((H, D), lambda i: (jnp.maximum(i * r - 1, 0), 0)),
                  _full((K, D)), _full((1, D)), _full((1, D)), _full((1, D))],
        out_specs=[_rows(tm, D)] * 2,
        scratch_shapes=[pltpu.VMEM((tm + H, D), F32),
                        pltpu.VMEM((SUBLANES, tm + SUBLANES * ((K - 1) // SUBLANES), D), F32)],
        compiler_params=_params("parallel"),
    )(glu, glu, dw, dw_b, g, b)


def _ln_bwd_store(dy, z_ref, g_ref, first, dz_ref, dzb_ref, gg_ref, gb_ref, sdz_ref):
    dz, xhat = _ln_bwd(dy, z_ref[...], g_ref[...])
    dz_ref[...] = dz
    dzb_ref[...] = dz.astype(BF16)
    _acc(gg_ref, first, _colsum(dy * xhat))
    _acc(gb_ref, first, _colsum(dy))
    _acc(sdz_ref, first, _colsum(dz))


def _ln_bwd_outs(T, D):
    shapes = [jax.ShapeDtypeStruct((T, D), F32), jax.ShapeDtypeStruct((T, D), BF16)] + [jax.ShapeDtypeStruct((1, D), F32)] * 3
    return shapes


def _mlp_act_bwd(name, dzb, u, w2b, tm, deps=()):
    T, D = dzb.shape
    nb, Fs, _ = w2b.shape
    Fd = nb * Fs

    def body(dzb_ref, u_ref, w2_ref, du_ref, gb1_ref):
        dzv = dzb_ref[...]
        sums = []
        for j in range(nb):
            sl = slice(j * Fs, (j + 1) * Fs)
            du = _mm_nt(dzv, w2_ref[j]) * (2.0 * jnp.maximum(u_ref[:, sl].astype(F32), 0.0))
            du_ref[:, sl] = du.astype(BF16)
            sums.append(_colsum(du))
        _acc(gb1_ref, pl.program_id(0) == 0, jnp.concatenate(sums, axis=1))

    return _tied_call(
        body, deps, name=name, grid=(T // tm,),
        out_shape=[jax.ShapeDtypeStruct((T, Fd), BF16), jax.ShapeDtypeStruct((1, Fd), F32)],
        in_specs=[_rows(tm, D), _rows(tm, Fd), _full(w2b.shape)],
        out_specs=[_rows(tm, Fd), _full((1, Fd))],
        compiler_params=_params("arbitrary"),
    )(dzb, u, w2b)


def _wgrad(name, xm, dy, col_blocks, nk, nj, tm, act=False):
    T, K = xm.shape
    N = dy.shape[1]
    Kb, Nb = K // nk, N // nj
    nt = T // tm
    if col_blocks:
        per, Ns = N_DEV // nj, N // N_DEV
        out_shape = (N_DEV, K, Ns)
        out_spec = pl.BlockSpec((per, Kb, Ns), lambda k, j, t: (j, k, 0))
    else:
        per, Ks = N_DEV // nk, K // N_DEV
        out_shape = (N_DEV, Ks, N)
        out_spec = pl.BlockSpec((per, Ks, Nb), lambda k, j, t: (k, 0, j))

    def body(x_ref, dy_ref, o_ref, acc_ref):
        t = pl.program_id(2)
        xb = _sq_relu(x_ref[...]) if act else x_ref[...].astype(BF16)
        _acc(acc_ref, t == 0, _mm_tn(xb, dy_ref[...]))

        @pl.when(t == nt - 1)
        def _():
            for q in range(per):
                if col_blocks:
                    o_ref[q] = acc_ref[:, q * Ns:(q + 1) * Ns].astype(BF16)
                else:
                    o_ref[q] = acc_ref[q * Ks:(q + 1) * Ks, :].astype(BF16)

    return pl.pallas_call(
        body, name=name, grid=(nk, nj, nt),
        out_shape=jax.ShapeDtypeStruct(out_shape, BF16),
        in_specs=[pl.BlockSpec((tm, Kb), lambda k, j, t: (t, k)), pl.BlockSpec((tm, Nb), lambda k, j, t: (t, j))],
        out_specs=out_spec,
        scratch_shapes=[pltpu.VMEM((Kb, Nb), F32)],
        compiler_params=_params("parallel", "parallel", "arbitrary"),
    )(xm, dy)


def _conv_out_bwd(dzb, w_out, cz, g, b, tm, deps=()):
    T, D = cz.shape

    def body(dz_ref, w_ref, cz_ref, g_ref, b_ref, dc_ref, gg_ref, gb_ref, sdc_ref):
        first = pl.program_id(0) == 0
        ds = _mm_nt(dz_ref[...], w_ref[...])
        czv = cz_ref[...]
        gv = g_ref[...]
        ln = _ln_fwd(czv, gv, b_ref[...])
        sg = _sigmoid(ln)
        dln = ds * (sg * (1.0 + ln * (1.0 - sg)))
        dc, xhat = _ln_bwd(dln, czv, gv)
        dc_ref[...] = dc
        _acc(gg_ref, first, _colsum(dln * xhat))
        _acc(gb_ref, first, _colsum(dln))
        _acc(sdc_ref, first, _colsum(dc))

    return _tied_call(
        body, deps, name="conv_out_bwd", grid=(T // tm,),
        out_shape=[jax.ShapeDtypeStruct((T, D), F32)] + [jax.ShapeDtypeStruct((1, D), F32)] * 3,
        in_specs=[_rows(tm, D), _full((D, D)), _rows(tm, D), _full((1, D)), _full((1, D))],
        out_specs=[_rows(tm, D)] + [_full((1, D))] * 3,
        compiler_params=_params("arbitrary"),
    )(dzb, w_out, cz, g, b)


def _dwconv_bwd(dc, glu, p, dw, tm):
    T, D = dc.shape
    K = dw.shape[0]
    Kp = -(-K // 8) * 8
    H = CONV_HALO
    off = H - (K - 1)
    r = tm // H
    last = T // H - 1
    nt = T // tm
    cc = _lane_chunk(D)

    def body(dc_ref, dch_ref, x_ref, xh_ref, p_ref, dw_ref, dp_ref, gdw_ref, gbin_ref, edc_ref, ex_ref, dglu_ref,
             gacc_ref):
        i = pl.program_id(0)
        first = i == 0
        edc_ref[0:tm, :] = dc_ref[...]
        edc_ref[tm:, :] = jnp.where(i < nt - 1, dch_ref[...], 0.0)
        ex_ref[0:H, :] = jnp.where(i > 0, xh_ref[...], 0.0)
        ex_ref[H:, :] = x_ref[...]

        @pl.when(first)
        def _():
            gacc_ref[...] = jnp.zeros_like(gacc_ref)

        rc = _row_chunk(tm, 64)
        for c0 in range(0, D, cc):
            cs = slice(c0, c0 + cc)
            for r0 in range(0, tm, rc):
                dcv = dc_ref[r0:r0 + rc, cs]
                acc = jnp.zeros((rc, cc), F32)
                for slab, taps in _slabs(edc_ref, 0, K, r0, rc, cs):
                    for q, m in taps:
                        acc = acc + slab[SUBLANES * q:SUBLANES * q + rc] * dw_ref[K - 1 - m:K - m, cs]
                dglu_ref[r0:r0 + rc, cs] = acc
                for slab, taps in _slabs(ex_ref, off, K, r0, rc, cs):
                    for q, k in taps:
                        part = (dcv * slab[SUBLANES * q:SUBLANES * q + rc]).reshape(rc // SUBLANES, SUBLANES, cc)
                        gacc_ref[k, :, cs] += jnp.sum(part, axis=0)

        @pl.when(i == nt - 1)
        def _():
            gdw_ref[...] = jnp.zeros_like(gdw_ref)
            gdw_ref[0:K, :] = jnp.sum(gacc_ref[...], axis=1)
        dglu = dglu_ref[...]
        a = p_ref[:, 0:D]
        sg = _sigmoid(p_ref[:, D:2 * D])
        da = dglu * sg
        dgate = dglu * a * (sg * (1.0 - sg))
        dp_ref[:, 0:D] = da.astype(BF16)
        dp_ref[:, D:2 * D] = dgate.astype(BF16)
        _acc(gbin_ref.at[:, 0:D], first, _colsum(da))
        _acc(gbin_ref.at[:, D:2 * D], first, _colsum(dgate))

    return pl.pallas_call(
        body, name="dwconv_bwd", grid=(nt,),
        out_shape=[jax.ShapeDtypeStruct((T, 2 * D), BF16), jax.ShapeDtypeStruct((Kp, D), F32),
                   jax.ShapeDtypeStruct((1, 2 * D), F32)],
        in_specs=[_rows(tm, D), pl.BlockSpec((H, D), lambda i: (jnp.minimum((i + 1) * r, last), 0)),
                  _rows(tm, D), pl.BlockSpec((H, D), lambda i: (jnp.maximum(i * r - 1, 0), 0)),
                  _rows(tm, 2 * D), _full((K, D))],
        out_specs=[_rows(tm, 2 * D), _full((Kp, D)), _full((1, 2 * D))],
        scratch_shapes=[pltpu.VMEM((tm + H, D), F32), pltpu.VMEM((tm + H, D), F32), pltpu.VMEM((tm, D), F32),
                        pltpu.VMEM((K, SUBLANES, D), F32)],
        compiler_params=_params("arbitrary"),
    )(dc, dc, glu, glu, p, dw)


def _dx_proj(name, dz, dy, wb, z_in, g_in, alpha, tm, deps=()):
    T, D = dz.shape
    nb, _, Ns = wb.shape
    N = nb * Ns

    def body(dz_ref, dy_ref, w_ref, zin_ref, gin_ref, dzo_ref, dzob_ref, gg_ref, gb_ref, sdz_ref):
        dx = alpha * dz_ref[...]
        for j in range(nb):
            dx = dx + _mm_nt(dy_ref[:, j * Ns:(j + 1) * Ns], w_ref[j])
        _ln_bwd_store(dx, zin_ref, gin_ref, pl.program_id(0) == 0, dzo_ref, dzob_ref, gg_ref, gb_ref, sdz_ref)

    return _tied_call(
        body, deps, name=name, grid=(T // tm,),
        out_shape=_ln_bwd_outs(T, D),
        in_specs=[_rows(tm, D), _rows(tm, N), _full(wb.shape), _rows(tm, D), _full((1, D))],
        out_specs=[_rows(tm, D)] * 2 + [_full((1, D))] * 3,
        compiler_params=_params("arbitrary"),
    )(dz, dy, wb, z_in, g_in)


def _pool_bwd(dz, x, pw, scale, alpha, tm, deps=()):
    T, D = x.shape
    G = len(POOL_WINDOWS)
    Dg = D // G
    H = POOL_HALO
    r = tm // H
    last = T // H - 1
    nt = T // tm
    n_ext = tm + H

    def body(dz_ref, dzh_ref, x_ref, xh_ref, pw_ref, sc_ref, dx_ref, gpw_ref, gsc_ref, edz_ref, ex_ref):
        i = pl.program_id(0)
        first = i == 0
        edz_ref[0:tm, :] = dz_ref[...]
        edz_ref[tm:, :] = jnp.where(i < nt - 1, dzh_ref[...], 0.0)
        ex_ref[0:H, :] = jnp.where(i > 0, xh_ref[...], 0.0)
        ex_ref[H:, :] = x_ref[...]
        row = i * tm + lax.broadcasted_iota(jnp.int32, (tm, 1), 0)
        rowf = (row + 1).astype(F32)
        erow = i * tm + lax.broadcasted_iota(jnp.int32, (n_ext, 1), 0)
        erowf = (erow + 1).astype(F32)
        g_scale, g_pw = [], []
        for gi, w in enumerate(POOL_WINDOWS):
            sl = slice(gi * Dg, (gi + 1) * Dg)
            s = ex_ref[:, sl]
            k = 1
            while k < w:
                s = s + pltpu.roll(s, k, 0)
                k *= 2
            xg = x_ref[:, sl]
            d = (s[H:, :] * (1.0 / jnp.minimum(rowf, float(w))) - xg).astype(BF16)
            wg = pw_ref[gi]
            premix = _mm(d, wg)
            dzg = dz_ref[:, sl]
            g_scale.append(_colsum(dzg * premix))
            dpre = edz_ref[:, sl] * sc_ref[:, sl]
            dpre_b = dpre.astype(BF16)
            g_pw.append(_mm_tn(d, dpre_b[0:tm, :]))
            dd = _mm_nt(dpre_b, wg)
            e = dd * (1.0 / jnp.minimum(erowf, float(w)))
            k = 1
            while k < w:
                e = e + pltpu.roll(e, n_ext - k, 0)
                k *= 2
            dx_ref[:, sl] = alpha * dzg + e[0:tm, :] - dd[0:tm, :]
        _acc(gsc_ref, first, jnp.concatenate(g_scale, axis=1))

        @pl.when(first)
        def _():
            for gi in range(G):
                gpw_ref[gi] = g_pw[gi]

        @pl.when(jnp.logical_not(first))
        def _():
            for gi in range(G):
                gpw_ref[gi] += g_pw[gi]

    return _tied_call(
        body, deps, name="pool_bwd", grid=(nt,),
        out_shape=[jax.ShapeDtypeStruct((T, D), F32), jax.ShapeDtypeStruct((G, Dg, Dg), F32),
                   jax.ShapeDtypeStruct((1, D), F32)],
        in_specs=[_rows(tm, D), pl.BlockSpec((H, D), lambda i: (jnp.minimum((i + 1) * r, last), 0)),
                  _rows(tm, D), pl.BlockSpec((H, D), lambda i: (jnp.maximum(i * r - 1, 0), 0)),
                  _full(pw.shape), _full((1, D))],
        out_specs=[_rows(tm, D), _full((G, Dg, Dg)), _full((1, D))],
        scratch_shapes=[pltpu.VMEM((n_ext, D), F32), pltpu.VMEM((n_ext, D), F32)],
        compiler_params=_params("arbitrary"),
    )(dz, dz, x, x, pw, scale)


def _adamw(name, recv, w, m, v, tm, layer=None, prev=None):
    R, C = w.shape[-2:]
    c1 = 1.0 - ADAM_B1 ** ADAM_STEP
    c2 = 1.0 - ADAM_B2 ** ADAM_STEP
    if layer is None:
        spec = _rows(tm, C)
    else:
        spec = pl.BlockSpec((None, tm, C), lambda i: (layer, i, 0))
    prev = list(prev) if prev is not None else []

    def body(r_ref, w_ref, m_ref, v_ref, *rest):
        g_ref, d_ref, nm_ref, nv_ref = rest[len(prev):]
        g = r_ref[0].astype(F32)
        for s in range(1, N_DEV):
            g = g + r_ref[s].astype(F32)
        m1 = ADAM_B1 * m_ref[...] + (1.0 - ADAM_B1) * g
        v1 = ADAM_B2 * v_ref[...] + (1.0 - ADAM_B2) * (g * g)
        m_hat = m1 / c1
        v_hat = v1 / c2
        g_ref[...] = g
        d_ref[...] = -ADAM_LR * (m_hat / (jnp.sqrt(v_hat) + ADAM_EPS) + ADAM_WD * w_ref[...])
        nm_ref[...] = m1
        nv_ref[...] = v1

    return pl.pallas_call(
        body, name=name, grid=(R // tm,),
        out_shape=[jax.ShapeDtypeStruct(w.shape, F32)] * 4,
        in_specs=[pl.BlockSpec((N_DEV, tm, C), lambda i: (0, i, 0))] + [spec] * 3
        + [pl.BlockSpec(memory_space=pl.ANY)] * len(prev),
        out_specs=[spec] * 4,
        input_output_aliases={4 + j: j for j in range(len(prev))},
        compiler_params=_params("parallel"),
    )(recv, w, m, v, *prev)


def _adamw_small(name, recv, w, m, v, pieces):
    R, C = w.shape
    c1 = 1.0 - ADAM_B1 ** ADAM_STEP
    c2 = 1.0 - ADAM_B2 ** ADAM_STEP
    n = len(pieces)

    def body(r_ref, w_ref, m_ref, v_ref, *rest):
        outs, packed = rest[:4 * n], rest[4 * n]
        g = r_ref[0]
        for s in range(1, N_DEV):
            g = g + r_ref[s]
        m1 = ADAM_B1 * m_ref[...] + (1.0 - ADAM_B1) * g
        v1 = ADAM_B2 * v_ref[...] + (1.0 - ADAM_B2) * (g * g)
        packed[0] = g
        packed[1] = -ADAM_LR * ((m1 / c1) / (jnp.sqrt(v1 / c2) + ADAM_EPS) + ADAM_WD * w_ref[...])
        packed[2] = m1
        packed[3] = v1
        for kind in range(4):
            for p, (r0, rows, shape) in enumerate(pieces):
                o_ref = outs[kind * n + p]
                if shape[-1] == C:
                    o_ref[...] = packed[kind, r0:r0 + rows, :].reshape(shape)
                else:
                    per = shape[-1] // C
                    for idx in range(rows):
                        l, q = divmod(idx, per)
                        o_ref[l:l + 1, q * C:(q + 1) * C] = packed[kind, r0 + idx:r0 + idx + 1, :]

    outs = pl.pallas_call(
        body, name=name,
        out_shape=[jax.ShapeDtypeStruct(shape, F32) for _ in range(4) for _, _, shape in pieces],
        scratch_shapes=[pltpu.VMEM((4, R, C), F32)],
    )(recv, w, m, v)
    return [outs[k * n:(k + 1) * n] for k in range(4)]


def _pad_rows(a, rows):
    return jnp.pad(a, ((0, rows - a.shape[0]), (0, 0)))


def kernel(x, pool_w, pool_scale, conv_w_in, conv_b_in, conv_dw, conv_dw_b, conv_ln_g, conv_ln_b, conv_w_out, conv_b_out, mix_ln_g, mix_ln_b, mlp_w1, mlp_b1, mlp_w2, mlp_b2, mlp_ln_g, mlp_ln_b, loss_target, m_pool_w, m_pool_scale, m_conv_w_in, m_conv_b_in, m_conv_dw, m_conv_dw_b, m_conv_ln_g, m_conv_ln_b, m_conv_w_out, m_conv_b_out, m_mix_ln_g, m_mix_ln_b, m_mlp_w1, m_mlp_b1, m_mlp_w2, m_mlp_b2, m_mlp_ln_g, m_mlp_ln_b, v_pool_w, v_pool_scale, v_conv_w_in, v_conv_b_in, v_conv_dw, v_conv_dw_b, v_conv_ln_g, v_conv_ln_b, v_conv_w_out, v_conv_b_out, v_mix_ln_g, v_mix_ln_b, v_mlp_w1, v_mlp_b1, v_mlp_w2, v_mlp_b2, v_mlp_ln_g, v_mlp_ln_b):
    _, T, D = x.shape
    L = mlp_w1.shape[0]
    assert L == 2 and pool_w.shape[0] == 1 and conv_w_in.shape[0] == 1
    G = pool_w.shape[1]
    Dg = D // G
    Fd = mlp_b1.shape[1]
    Fs = Fd // N_DEV
    Kc = conv_dw.shape[1]
    Dc = D // N_DEV
    alpha = float((2.0 * L) ** 0.25)
    x2d, tgt = x[0], loss_target[0]

    tm = _tile(T, 512)
    tm_wide = _tile(T, 512)
    tm_conv = _tile(T, 256)
    tm_wg = _tile(T, 1024)

    def pack_sh(dw, dwb, lg, lb, bo, bi):
        rows = jnp.concatenate([dw[0], dwb, lg, lb, bo, bi.reshape(2, Dc)], axis=0)
        return _pad_rows(rows, SH_ROWS)

    SH_ROWS = -(-(Kc + 6) // 8) * 8
    def pack_rep(ps, mg, mb, b1, b2, lg, lb):
        rows = jnp.concatenate([ps, mg, mb, b1.reshape(L * Fd // D, D), b2, lg, lb], axis=0)
        return _pad_rows(rows, REP_ROWS)

    n_rep = 1 + 2 * L + L * Fd // D + 3 * L
    REP_ROWS = -(-n_rep // 8) * 8

    w_sh = pack_sh(conv_dw, conv_dw_b, conv_ln_g, conv_ln_b, conv_b_out, conv_b_in)
    m_sh = pack_sh(m_conv_dw, m_conv_dw_b, m_conv_ln_g, m_conv_ln_b, m_conv_b_out, m_conv_b_in)
    v_sh = pack_sh(v_conv_dw, v_conv_dw_b, v_conv_ln_g, v_conv_ln_b, v_conv_b_out, v_conv_b_in)
    w_rep = pack_rep(pool_scale, mix_ln_g, mix_ln_b, mlp_b1, mlp_b2, mlp_ln_g, mlp_ln_b)
    m_rep = pack_rep(m_pool_scale, m_mix_ln_g, m_mix_ln_b, m_mlp_b1, m_mlp_b2, m_mlp_ln_g, m_mlp_ln_b)
    v_rep = pack_rep(v_pool_scale, v_mix_ln_g, v_mix_ln_b, v_mlp_b1, v_mlp_b2, v_mlp_ln_g, v_mlp_ln_b)

    groups = [[pool_w[0]], [mlp_w1[0]], [mlp_w2[0], conv_w_in[0], conv_w_out[0]], [mlp_w1[1], mlp_w2[1]]]
    handles, tokens = [], []
    for i, grp in enumerate(groups):
        tie = tokens[-1][0, 0] if tokens else 0.0
        srcs = [(a + tie).astype(BF16) for a in grp] + ([w_sh] if i == 0 else [])
        h, tk = _xstart("gather_%d" % i, srcs, [False] * len(srcs))
        handles.append(h)
        tokens.append(tk)
    pw_all, sh_all = _xwait(handles[0], tokens[-1])
    pw = pw_all.transpose(1, 0, 2, 3).reshape(G, Dg, Dg)
    dw_full = sh_all[:, 0:Kc].transpose(1, 0, 2).reshape(Kc, D)

    def sh_row(i):
        return sh_all[:, i].reshape(1, D)

    dwb_full, cg_full, cb_full, bout_full = (sh_row(Kc + i) for i in range(4))
    bin_full = sh_all[:, Kc + 4:Kc + 6].reshape(1, 2 * D)

    h0, z_m0 = _pool_fwd(x2d, pw, pool_scale, mix_ln_g[0:1], mix_ln_b[0:1], alpha, tm, tuple(tokens))
    (w1b0,) = _xwait(handles[1], h0)
    u0 = _mlp_up("mlp_up0", h0, w1b0, mlp_b1[0:1], tm)
    w2b0, win_b, wout_all = _xwait(handles[2], u0)
    x1, z_f0 = _proj_ln("mlp_down0", u0, w2b0.reshape(Fd, D), mlp_b2[0:1], h0, mlp_ln_g[0:1], mlp_ln_b[0:1],
                        alpha, tm, act=True)
    w_out = wout_all.reshape(D, D)
    p, glu = _conv_in(x1, win_b, bin_full, tm)
    cz, s = _dwconv_fwd(glu, dw_full, dwb_full, cg_full, cb_full, tm_conv)
    h1, z_m1 = _proj_ln("conv_out", s, w_out, bout_full, x1, mix_ln_g[1:2], mix_ln_b[1:2], alpha, tm)
    w1b1, w2b1 = _xwait(handles[3], h1)
    u1 = _mlp_up("mlp_up1", h1, w1b1, mlp_b1[1:2], tm)

    dz, dzb, g_fg1, g_fb1, g_b2_1, loss_cols = _proj_ln_loss(
        "mlp_down1", u1, w2b1.reshape(Fd, D), mlp_b2[1:2], h1, mlp_ln_g[1:2], mlp_ln_b[1:2], tgt, alpha, tm)
    gw2_1 = _wgrad("gw2_1", u1, dzb, False, 4, 1, tm_wg, act=True)
    e_w2_1, tk = _xstart("grads_w2_1", [gw2_1], [True])
    du, g_b1_1 = _mlp_act_bwd("mlp_act_bwd1", dzb, u1, w2b1, tm, (tk,))
    dz, dzb, g_mg1, g_mb1, g_bout = _dx_proj("mlp_in_bwd1", dz, du, w1b1, z_m1, mix_ln_g[1:2], alpha, tm_wide)
    gw1_1 = _wgrad("gw1_1", h1, du, True, 1, 2, tm_wg)
    e_w1_1, tk = _xstart("grads_w1_1", [gw1_1], [True])
    gwout = _wgrad("gw_out", s, dzb, False, 1, 1, tm_wg)
    dc, g_cg, g_cb, g_dwb = _conv_out_bwd(dzb, w_out, cz, cg_full, cb_full, tm, (tk,))
    dp, g_dw, g_bin = _dwconv_bwd(dc, glu, p, dw_full, tm_conv)
    gwin = _wgrad("gw_in", x1, dp, True, 1, 2, tm_wg)
    e_conv, tk = _xstart("grads_conv", [gwin, gwout], [True, True])
    dz, dzb, g_fg0, g_fb0, g_b2_0 = _dx_proj("conv_in_bwd", dz, dp, win_b, z_f0, mlp_ln_g[0:1], alpha, tm, (tk,))
    gw2_0 = _wgrad("gw2_0", u0, dzb, False, 4, 1, tm_wg, act=True)
    e_w2_0, tk = _xstart("grads_w2_0", [gw2_0], [True])
    du, g_b1_0 = _mlp_act_bwd("mlp_act_bwd0", dzb, u0, w2b0, tm, (tk,))
    dz, dzb, g_mg0, g_mb0, _ = _dx_proj("mlp_in_bwd0", dz, du, w1b0, z_m0, mix_ln_g[0:1], alpha, tm_wide)
    gw1_0 = _wgrad("gw1_0", h0, du, True, 1, 2, tm_wg)
    e_w1_0, tk = _xstart("grads_w1_0", [gw1_0], [True])
    grad_x, g_pw, g_ps = _pool_bwd(dz, x2d, pw, pool_scale, alpha, tm, (tk,))

    loss = lax.psum(0.5 / D * jnp.sum(loss_cols), MESH_AXES)

    gpw_b = g_pw.reshape(G, N_DEV, Dg // N_DEV, Dg).transpose(1, 0, 2, 3).astype(BF16)

    def to_dev(vec, rows):
        return vec.reshape(rows, N_DEV, Dc).transpose(1, 0, 2)

    g_sh = jnp.concatenate(
        [to_dev(g_dw[0:Kc], Kc), to_dev(g_dwb, 1), to_dev(g_cg, 1), to_dev(g_cb, 1), to_dev(g_bout, 1),
         g_bin.reshape(N_DEV, 2, Dc), jnp.zeros((N_DEV, SH_ROWS - Kc - 6, Dc), F32)], axis=1)
    g_rep = _pad_rows(jnp.concatenate(
        [g_ps, g_mg0, g_mg1, g_mb0, g_mb1, g_b1_0.reshape(Fd // D, D), g_b1_1.reshape(Fd // D, D),
         g_b2_0, g_b2_1, g_fg0, g_fg1, g_fb0, g_fb1], axis=0), REP_ROWS)

    e_small, tk = _xstart("grads_small", [gpw_b, g_sh, g_rep], [True, True, False])

    def upd(name, recv, w, m, v):
        shape = w.shape
        C = shape[-1]
        R = w.size // C
        outs = _adamw(name, recv.reshape(N_DEV, R, C), w.reshape(R, C), m.reshape(R, C), v.reshape(R, C), _tile(R, 256))
        return [o.reshape(shape) for o in outs]

    def upd_layer(name, recv, w, m, v, layer, prev):
        return _adamw(name, recv, w, m, v, _tile(w.shape[1], 256), layer, prev)

    (r_w2_1,) = _xwait(e_w2_1, tk)
    o_w2 = upd_layer("adam_w2_1", r_w2_1, mlp_w2, m_mlp_w2, v_mlp_w2, 1, None)
    (r_w1_1,) = _xwait(e_w1_1, o_w2[0])
    o_w1 = upd_layer("adam_w1_1", r_w1_1, mlp_w1, m_mlp_w1, v_mlp_w1, 1, None)
    r_win, r_wout = _xwait(e_conv, o_w1[0])
    o_win = upd("adam_w_in", r_win, conv_w_in, m_conv_w_in, v_conv_w_in)
    o_wout = upd("adam_w_out", r_wout, conv_w_out, m_conv_w_out, v_conv_w_out)
    (r_w2_0,) = _xwait(e_w2_0, o_wout[0])
    o_w2 = upd_layer("adam_w2_0", r_w2_0, mlp_w2, m_mlp_w2, v_mlp_w2, 0, o_w2)
    (r_w1_0,) = _xwait(e_w1_0, o_w2[0])
    o_w1 = upd_layer("adam_w1_0", r_w1_0, mlp_w1, m_mlp_w1, v_mlp_w1, 0, o_w1)
    r_pw, r_sh, r_rep = _xwait(e_small, o_w1[0])
    o_pw = upd("adam_pool_w", r_pw, pool_w, m_pool_w, v_pool_w)
    sh_pieces = [(0, Kc, (1, Kc, Dc))] + [(Kc + i, 1, (1, Dc)) for i in range(4)] + [(Kc + 4, 2, (1, 2 * Dc))]
    rep_pieces, o = [], 0
    for rows, shape in ((1, (1, D)), (L, (L, D)), (L, (L, D)), (L * Fd // D, (L, Fd)), (L, (L, D)), (L, (L, D)),
                        (L, (L, D))):
        rep_pieces.append((o, rows, shape))
        o += rows
    o_sh = _adamw_small("adam_conv_vec", r_sh, w_sh, m_sh, v_sh, sh_pieces)
    o_rep = _adamw_small("adam_replicated", r_rep, w_rep, m_rep, v_rep, rep_pieces)

    results = []
    for kind in range(4):
        dwv, dwb, lg, lb, bo, bi = o_sh[kind]
        ps, mg, mb, b1, b2, fg, fb = o_rep[kind]
        results.append([o_pw[kind], ps, o_win[kind], bi, dwv, dwb, lg, lb, o_wout[kind], bo, mg, mb,
                        o_w1[kind], b1, o_w2[kind], b2, fg, fb])
    return (loss, grad_x[None], *results[0], *results[1], *results[2], *results[3])
```

```python
import jax
import jax.numpy as jnp
from jax import lax
from jax.experimental import pallas as pl
from jax.experimental.pallas import tpu as pltpu

N_DEV = 8
MESH_AXES = ("x", "y", "c")
POOL_WINDOWS = (2, 4, 8, 16)
POOL_HALO = 16
CONV_HALO = 32
LN_EPS = 1e-5
ADAM_LR = 0.001
ADAM_B1 = 0.9
ADAM_B2 = 0.999
ADAM_EPS = 1e-08
ADAM_WD = 0.01
ADAM_STEP = 10
VMEM_LIMIT = 56 * 1024 * 1024

F32 = jnp.float32
BF16 = jnp.bfloat16


def _mm(a, b):
    return lax.dot_general(a, b, (((1,), (0,)), ((), ())), preferred_element_type=F32)


def _mm_nt(a, b):
    return lax.dot_general(a, b, (((1,), (1,)), ((), ())), preferred_element_type=F32)


def _mm_tn(a, b):
    return lax.dot_general(a, b, (((0,), (0,)), ((), ())), preferred_element_type=F32)


def _tile(n, pref):
    t = min(n, pref)
    assert n % t == 0, (n, pref)
    return t


def _params(*sem):
    return pltpu.CompilerParams(dimension_semantics=sem, vmem_limit_bytes=VMEM_LIMIT)


def _full(shape):
    nd = len(shape)
    return pl.BlockSpec(shape, lambda *_: (0,) * nd)


def _rows(tm, d):
    return pl.BlockSpec((tm, d), lambda i: (i, 0))


def _ln_stats(z):
    mu = jnp.mean(z, axis=-1, keepdims=True)
    zc = z - mu
    var = jnp.mean(zc * zc, axis=-1, keepdims=True)
    rstd = lax.rsqrt(var + LN_EPS)
    return zc * rstd, rstd


def _ln_fwd(z, g, b):
    xhat, _ = _ln_stats(z)
    return xhat * g + b


def _ln_bwd(dy, z, g):
    xhat, rstd = _ln_stats(z)
    dxh = dy * g
    m1 = jnp.mean(dxh, axis=-1, keepdims=True)
    m2 = jnp.mean(dxh * xhat, axis=-1, keepdims=True)
    return rstd * (dxh - m1 - xhat * m2), xhat


def _colsum(v):
    return jnp.sum(v, axis=0, keepdims=True)


def _sigmoid(v):
    return 1.0 / (1.0 + jnp.exp(-v))


def _acc(ref, first, val):
    @pl.when(first)
    def _():
        ref[...] = val

    @pl.when(jnp.logical_not(first))
    def _():
        ref[...] += val


_HBM = pl.BlockSpec(memory_space=pltpu.HBM)
_SEM = pl.BlockSpec(memory_space=pltpu.SEMAPHORE)
_EFFECT = pltpu.SideEffectType.DATAFLOW_SIDE_EFFECTING


def _peers():
    x, y, c = (lax.axis_index(a) for a in MESH_AXES)
    out = []
    for d in range(1, N_DEV):
        px = (x + ((d >> 2) & 1)) % 2
        py = (y + ((d >> 1) & 1)) % 2
        pc = (c + (d & 1)) % 2
        out.append((d - 1, (px, py, pc), 4 * px + 2 * py + pc))
    return 4 * x + 2 * y + c, out


def _remote_copies(src_refs, land_refs, scatter, send_sems, recv_sems):
    me, peers = _peers()
    copies = []
    for i, pos, pid in peers:
        for k, (src, land) in enumerate(zip(src_refs, land_refs)):
            copies.append(pltpu.make_async_remote_copy(
                src_ref=src.at[pid] if scatter[k] else src, dst_ref=land.at[me],
                send_sem=send_sems.at[k * (N_DEV - 1) + i], recv_sem=recv_sems.at[k * (N_DEV - 1) + i],
                device_id=pos, device_id_type=pl.DeviceIdType.MESH))
    return copies


def _xstart(name, srcs, scatter):
    n = len(srcs)
    me = 4 * lax.axis_index("x") + 2 * lax.axis_index("y") + lax.axis_index("c")
    lands = []
    for s, sc in zip(srcs, scatter):
        own = lax.dynamic_index_in_dim(s, me, 0, keepdims=True) if sc else s[None]
        shape = s.shape if sc else (N_DEV,) + s.shape
        lands.append(lax.dynamic_update_slice(lax.empty(shape, s.dtype), own, (me,) + (0,) * (len(shape) - 1)))

    def body(*refs):
        src_refs, land_refs = refs[:n], refs[n:2 * n]
        send_sems, recv_sems = refs[2 * n], refs[2 * n + 1]
        token = refs[-1]
        for cp in _remote_copies(src_refs, land_refs, scatter, send_sems, recv_sems):
            cp.start()
        token[...] = jnp.zeros_like(token)

    outs = pl.pallas_call(
        body, name=name,
        out_shape=(pltpu.SemaphoreType.DMA((n * (N_DEV - 1),)), pltpu.SemaphoreType.DMA((n * (N_DEV - 1),)),
                   *[pltpu.HBM(a.shape, a.dtype) for a in srcs + lands], jax.ShapeDtypeStruct((8, 128), F32)),
        in_specs=(_HBM,) * (2 * n),
        out_specs=(_SEM, _SEM) + (_HBM,) * (2 * n) + (pl.BlockSpec(memory_space=pltpu.VMEM),),
        input_output_aliases={i: 2 + i for i in range(2 * n)},
        compiler_params=pltpu.CompilerParams(has_side_effects=_EFFECT),
    )(*[pltpu.with_memory_space_constraint(a, pltpu.HBM) for a in srcs + lands])
    return (name, scatter, outs[0], outs[1], outs[2:2 + n], outs[2 + n:2 + 2 * n]), outs[-1]


def _xwait(handle, after):
    name, scatter, send_sems, recv_sems, srcs, lands = handle
    n = len(srcs)

    def body(*refs):
        src_refs, land_refs = refs[:n], refs[n:2 * n]
        send, recv = refs[2 * n], refs[2 * n + 1]
        copies = _remote_copies(src_refs, land_refs, scatter, send, recv)
        for cp in copies:
            cp.wait_send()
        for cp in copies:
            cp.wait_recv()

    outs = pl.pallas_call(
        body, name=name + "_wait",
        out_shape=tuple(pltpu.HBM(a.shape, a.dtype) for a in (*srcs, *lands)),
        in_specs=(_HBM,) * (2 * n) + (_SEM, _SEM, pl.BlockSpec(memory_space=pl.ANY)),
        out_specs=(_HBM,) * (2 * n),
        input_output_aliases={i: i for i in range(2 * n)},
        compiler_params=pltpu.CompilerParams(has_side_effects=_EFFECT),
    )(*srcs, *lands, send_sems, recv_sems, after)
    return outs[n:]


def _tied_call(body, deps, in_specs, **kw):
    nd = len(deps)

    def tied_body(*refs):
        body(*refs[nd:])

    call = pl.pallas_call(tied_body, in_specs=[pl.BlockSpec(memory_space=pl.ANY)] * nd + list(in_specs), **kw)
    return lambda *args: call(*deps, *args)


def _pool_fwd(x, pw, scale, g, b, alpha, tm, deps=()):
    T, D = x.shape
    G = len(POOL_WINDOWS)
    Dg = D // G
    H = POOL_HALO
    r = tm // H

    def body(x_ref, xh_ref, pw_ref, sc_ref, g_ref, b_ref, y_ref, z_ref, ext_ref):
        i = pl.program_id(0)
        ext_ref[0:H, :] = jnp.where(i > 0, xh_ref[...], 0.0)
        ext_ref[H:, :] = x_ref[...]
        row = i * tm + lax.broadcasted_iota(jnp.int32, (tm, 1), 0)
        rowf = (row + 1).astype(F32)
        for gi, w in enumerate(POOL_WINDOWS):
            sl = slice(gi * Dg, (gi + 1) * Dg)
            s = ext_ref[:, sl]
            k = 1
            while k < w:
                s = s + pltpu.roll(s, k, 0)
                k *= 2
            inv = 1.0 / jnp.minimum(rowf, float(w))
            xg = x_ref[:, sl]
            d = s[H:, :] * inv - xg
            mix = _mm(d.astype(BF16), pw_ref[gi]) * sc_ref[:, sl]
            z_ref[:, sl] = alpha * xg + mix
        y_ref[...] = _ln_fwd(z_ref[...], g_ref[...], b_ref[...])

    return _tied_call(
        body, deps, name="pool_fwd", grid=(T // tm,),
        out_shape=[jax.ShapeDtypeStruct((T, D), F32)] * 2,
        in_specs=[_rows(tm, D), pl.BlockSpec((H, D), lambda i: (jnp.maximum(i * r - 1, 0), 0)),
                  _full(pw.shape), _full((1, D)), _full((1, D)), _full((1, D))],
        out_specs=[_rows(tm, D)] * 2,
        scratch_shapes=[pltpu.VMEM((tm + H, D), F32)],
        compiler_params=_params("parallel"),
    )(x, x, pw, scale, g, b)


def _mlp_up(name, h, w1b, b1, tm):
    T, D = h.shape
    nb, _, Fs = w1b.shape
    Fd = nb * Fs

    def body(h_ref, w_ref, b_ref, u_ref):
        hb = h_ref[...].astype(BF16)
        for j in range(nb):
            sl = slice(j * Fs, (j + 1) * Fs)
            u_ref[:, sl] = (_mm(hb, w_ref[j]) + b_ref[:, sl]).astype(BF16)

    return pl.pallas_call(
        body, name=name, grid=(T // tm,),
        out_shape=jax.ShapeDtypeStruct((T, Fd), BF16),
        in_specs=[_rows(tm, D), _full(w1b.shape), _full((1, Fd))],
        out_specs=_rows(tm, Fd),
        compiler_params=_params("parallel"),
    )(h, w1b, b1)


def _sq_relu(u):
    r = jnp.maximum(u, 0.0)
    return r * r


def _proj_ln(name, a, w, bias, res, g, b, alpha, tm, act=False):
    T, K = a.shape
    D = w.shape[1]

    def body(a_ref, w_ref, bias_ref, res_ref, g_ref, b_ref, y_ref, z_ref):
        av = _sq_relu(a_ref[...]) if act else a_ref[...]
        z = alpha * res_ref[...] + _mm(av, w_ref[...]) + bias_ref[...]
        z_ref[...] = z
        y_ref[...] = _ln_fwd(z, g_ref[...], b_ref[...])

    return pl.pallas_call(
        body, name=name, grid=(T // tm,),
        out_shape=[jax.ShapeDtypeStruct((T, D), F32)] * 2,
        in_specs=[_rows(tm, K), _full((K, D)), _full((1, D)), _rows(tm, D), _full((1, D)), _full((1, D))],
        out_specs=[_rows(tm, D)] * 2,
        compiler_params=_params("parallel"),
    )(a, w, bias, res, g, b)


def _proj_ln_loss(name, a, w, bias, res, g, b, target, alpha, tm):
    T, K = a.shape
    D = w.shape[1]

    def body(a_ref, w_ref, bias_ref, res_ref, g_ref, b_ref, t_ref, dz_ref, dzb_ref, gg_ref, gb_ref, sdz_ref, loss_ref):
        first = pl.program_id(0) == 0
        z = alpha * res_ref[...] + _mm(_sq_relu(a_ref[...]), w_ref[...]) + bias_ref[...]
        gv = g_ref[...]
        e = _ln_fwd(z, gv, b_ref[...]) - t_ref[...]
        dy = e * (1.0 / D)
        dz, xhat = _ln_bwd(dy, z, gv)
        dz_ref[...] = dz
        dzb_ref[...] = dz.astype(BF16)
        _acc(gg_ref, first, _colsum(dy * xhat))
        _acc(gb_ref, first, _colsum(dy))
        _acc(sdz_ref, first, _colsum(dz))
        _acc(loss_ref, first, _colsum(e * e))

    return pl.pallas_call(
        body, name=name, grid=(T // tm,),
        out_shape=[jax.ShapeDtypeStruct((T, D), F32), jax.ShapeDtypeStruct((T, D), BF16)]
        + [jax.ShapeDtypeStruct((1, D), F32)] * 4,
        in_specs=[_rows(tm, K), _full((K, D)), _full((1, D)), _rows(tm, D), _full((1, D)), _full((1, D)), _rows(tm, D)],
        out_specs=[_rows(tm, D)] * 2 + [_full((1, D))] * 4,
        compiler_params=_params("arbitrary"),
    )(a, w, bias, res, g, b, target)


def _conv_in(x, wb, b_in, tm):
    T, D = x.shape
    nb, _, Ns = wb.shape
    half = nb // 2

    def body(x_ref, w_ref, b_ref, p_ref, glu_ref):
        xb = x_ref[...].astype(BF16)
        for j in range(half):
            sa = slice(j * Ns, (j + 1) * Ns)
            sg = slice(D + j * Ns, D + (j + 1) * Ns)
            a = _mm(xb, w_ref[j]) + b_ref[:, sa]
            gate = _mm(xb, w_ref[half + j]) + b_ref[:, sg]
            p_ref[:, sa] = a.astype(BF16)
            p_ref[:, sg] = gate.astype(BF16)
            glu_ref[:, sa] = a * _sigmoid(gate)

    return pl.pallas_call(
        body, name="conv_in", grid=(T // tm,),
        out_shape=[jax.ShapeDtypeStruct((T, 2 * D), BF16), jax.ShapeDtypeStruct((T, D), F32)],
        in_specs=[_rows(tm, D), _full(wb.shape), _full((1, 2 * D))],
        out_specs=[_rows(tm, 2 * D), _rows(tm, D)],
        compiler_params=_params("parallel"),
    )(x, wb, b_in)


def _lane_chunk(d):
    return 128 if d % 128 == 0 else d


SUBLANES = 8


def _row_chunk(tm, pref):
    return pref if tm % pref == 0 else tm


def _slabs(ref, base, n_taps, r0, rows, cs):
    out = []
    for r in range(min(SUBLANES, n_taps)):
        nq = (n_taps - 1 - r) // SUBLANES + 1
        lo = base + r + r0
        slab = ref[lo:lo + rows + SUBLANES * (nq - 1), cs]
        out.append((slab, [(q, SUBLANES * q + r) for q in range(nq)]))
    return out


def _dwconv_fwd(glu, dw, dw_b, g, b, tm):
    T, D = glu.shape
    K = dw.shape[0]
    H = CONV_HALO
    off = H - (K - 1)
    r = tm // H
    cc = _lane_chunk(D)

    def body(x_ref, xh_ref, dw_ref, dwb_ref, g_ref, b_ref, cz_ref, s_ref, ext_ref, sh_ref):
        i = pl.program_id(0)
        ext_ref[0:H, :] = jnp.where(i > 0, xh_ref[...], 0.0)
        ext_ref[H:, :] = x_ref[...]
        for ph in range(min(SUBLANES, K)):
            n = tm + SUBLANES * ((K - 1 - ph) // SUBLANES)
            sh_ref[ph, 0:n, :] = ext_ref[off + ph:off + ph + n, :]
        rc = _row_chunk(tm, 128)
        for c0 in range(0, D, cc):
            cs = slice(c0, c0 + cc)
            for r0 in range(0, tm, rc):
                acc = jnp.zeros((rc, cc), F32) + dwb_ref[:, cs]
                for k in range(K):
                    q, ph = divmod(k, SUBLANES)
                    acc = acc + sh_ref[ph, SUBLANES * q + r0:SUBLANES * q + r0 + rc, cs] * dw_ref[k:k + 1, cs]
                cz_ref[r0:r0 + rc, cs] = acc
        ln = _ln_fwd(cz_ref[...], g_ref[...], b_ref[...])
        s_ref[...] = (ln * _sigmoid(ln)).astype(BF16)

    return pl.pallas_call(
        body, name="dwconv_fwd", grid=(T // tm,),
        out_shape=[jax.ShapeDtypeStruct((T, D), F32), jax.ShapeDtypeStruct((T, D), BF16)],
        in_specs=[_rows(tm, D), pl.BlockSpec((H, D), lambda i: (jnp.maximum(i * r - 1, 0), 0)),
                  _full((K, D)), _full((1, D)), _full((1, D)), _full((1, D))],
        out_specs=[_rows(tm, D)] * 2,
        scratch_shapes=[pltpu.VMEM((tm + H, D), F32),
                        pltpu.VMEM((SUBLANES, tm + SUBLANES * ((K - 1) // SUBLANES), D), F32)],
        compiler_params=_params("parallel"),
    )(glu, glu, dw, dw_b, g, b)


def _ln_bwd_store(dy, z_ref, g_ref, first, dz_ref, dzb_ref, gg_ref, gb_ref, sdz_ref):
    dz, xhat = _ln_bwd(dy, z_ref[...], g_ref[...])
    dz_ref[...] = dz
    dzb_ref[...] = dz.astype(BF16)
    _acc(gg_ref, first, _colsum(dy * xhat))
    _acc(gb_ref, first, _colsum(dy))
    _acc(sdz_ref, first, _colsum(dz))


def _ln_bwd_outs(T, D):
    shapes = [jax.ShapeDtypeStruct((T, D), F32), jax.ShapeDtypeStruct((T, D), BF16)] + [jax.ShapeDtypeStruct((1, D), F32)] * 3
    return shapes


def _mlp_act_bwd(name, dzb, u, w2b, tm, deps=()):
    T, D = dzb.shape
    nb, Fs, _ = w2b.shape
    Fd = nb * Fs

    def body(dzb_ref, u_ref, w2_ref, du_ref, gb1_ref):
        dzv = dzb_ref[...]
        sums = []
        for j in range(nb):
            sl = slice(j * Fs, (j + 1) * Fs)
            du = _mm_nt(dzv, w2_ref[j]) * (2.0 * jnp.maximum(u_ref[:, sl].astype(F32), 0.0))
            du_ref[:, sl] = du.astype(BF16)
            sums.append(_colsum(du))
        _acc(gb1_ref, pl.program_id(0) == 0, jnp.concatenate(sums, axis=1))

    return _tied_call(
        body, deps, name=name, grid=(T // tm,),
        out_shape=[jax.ShapeDtypeStruct((T, Fd), BF16), jax.ShapeDtypeStruct((1, Fd), F32)],
        in_specs=[_rows(tm, D), _rows(tm, Fd), _full(w2b.shape)],
        out_specs=[_rows(tm, Fd), _full((1, Fd))],
        compiler_params=_params("arbitrary"),
    )(dzb, u, w2b)


def _wgrad(name, xm, dy, col_blocks, nk, nj, tm, act=False):
    T, K = xm.shape
    N = dy.shape[1]
    Kb, Nb = K // nk, N // nj
    nt = T // tm
    if col_blocks:
        per, Ns = N_DEV // nj, N // N_DEV
        out_shape = (N_DEV, K, Ns)
        out_spec = pl.BlockSpec((per, Kb, Ns), lambda k, j, t: (j, k, 0))
    else:
        per, Ks = N_DEV // nk, K // N_DEV
        out_shape = (N_DEV, Ks, N)
        out_spec = pl.BlockSpec((per, Ks, Nb), lambda k, j, t: (k, 0, j))

    def body(x_ref, dy_ref, o_ref, acc_ref):
        t = pl.program_id(2)
        xb = _sq_relu(x_ref[...]) if act else x_ref[...].astype(BF16)
        _acc(acc_ref, t == 0, _mm_tn(xb, dy_ref[...]))

        @pl.when(t == nt - 1)
        def _():
            for q in range(per):
                if col_blocks:
                    o_ref[q] = acc_ref[:, q * Ns:(q + 1) * Ns].astype(BF16)
                else:
                    o_ref[q] = acc_ref[q * Ks:(q + 1) * Ks, :].astype(BF16)

    return pl.pallas_call(
        body, name=name, grid=(nk, nj, nt),
        out_shape=jax.ShapeDtypeStruct(out_shape, BF16),
        in_specs=[pl.BlockSpec((tm, Kb), lambda k, j, t: (t, k)), pl.BlockSpec((tm, Nb), lambda k, j, t: (t, j))],
        out_specs=out_spec,
        scratch_shapes=[pltpu.VMEM((Kb, Nb), F32)],
        compiler_params=_params("parallel", "parallel", "arbitrary"),
    )(xm, dy)


def _conv_out_bwd(dzb, w_out, cz, g, b, tm, deps=()):
    T, D = cz.shape

    def body(dz_ref, w_ref, cz_ref, g_ref, b_ref, dc_ref, gg_ref, gb_ref, sdc_ref):
        first = pl.program_id(0) == 0
        ds = _mm_nt(dz_ref[...], w_ref[...])
        czv = cz_ref[...]
        gv = g_ref[...]
        ln = _ln_fwd(czv, gv, b_ref[...])
        sg = _sigmoid(ln)
        dln = ds * (sg * (1.0 + ln * (1.0 - sg)))
        dc, xhat = _ln_bwd(dln, czv, gv)
        dc_ref[...] = dc
        _acc(gg_ref, first, _colsum(dln * xhat))
        _acc(gb_ref, first, _colsum(dln))
        _acc(sdc_ref, first, _colsum(dc))

    return _tied_call(
        body, deps, name="conv_out_bwd", grid=(T // tm,),
        out_shape=[jax.ShapeDtypeStruct((T, D), F32)] + [jax.ShapeDtypeStruct((1, D), F32)] * 3,
        in_specs=[_rows(tm, D), _full((D, D)), _rows(tm, D), _full((1, D)), _full((1, D))],
        out_specs=[_rows(tm, D)] + [_full((1, D))] * 3,
        compiler_params=_params("arbitrary"),
    )(dzb, w_out, cz, g, b)


def _dwconv_bwd(dc, glu, p, dw, tm):
    T, D = dc.shape
    K = dw.shape[0]
    Kp = -(-K // 8) * 8
    H = CONV_HALO
    off = H - (K - 1)
    r = tm // H
    last = T // H - 1
    nt = T // tm
    cc = _lane_chunk(D)

    def body(dc_ref, dch_ref, x_ref, xh_ref, p_ref, dw_ref, dp_ref, gdw_ref, gbin_ref, edc_ref, ex_ref, dglu_ref,
             gacc_ref):
        i = pl.program_id(0)
        first = i == 0
        edc_ref[0:tm, :] = dc_ref[...]
        edc_ref[tm:, :] = jnp.where(i < nt - 1, dch_ref[...], 0.0)
        ex_ref[0:H, :] = jnp.where(i > 0, xh_ref[...], 0.0)
        ex_ref[H:, :] = x_ref[...]

        @pl.when(first)
        def _():
            gacc_ref[...] = jnp.zeros_like(gacc_ref)

        rc = _row_chunk(tm, 64)
        for c0 in range(0, D, cc):
            cs = slice(c0, c0 + cc)
            for r0 in range(0, tm, rc):
                dcv = dc_ref[r0:r0 + rc, cs]
                acc = jnp.zeros((rc, cc), F32)
                for slab, taps in _slabs(edc_ref, 0, K, r0, rc, cs):
                    for q, m in taps:
                        acc = acc + slab[SUBLANES * q:SUBLANES * q + rc] * dw_ref[K - 1 - m:K - m, cs]
                dglu_ref[r0:r0 + rc, cs] = acc
                for slab, taps in _slabs(ex_ref, off, K, r0, rc, cs):
                    for q, k in taps:
                        part = (dcv * slab[SUBLANES * q:SUBLANES * q + rc]).reshape(rc // SUBLANES, SUBLANES, cc)
                        gacc_ref[k, :, cs] += jnp.sum(part, axis=0)

        @pl.when(i == nt - 1)
        def _():
            gdw_ref[...] = jnp.zeros_like(gdw_ref)
            gdw_ref[0:K, :] = jnp.sum(gacc_ref[...], axis=1)
        dglu = dglu_ref[...]
        a = p_ref[:, 0:D].astype(F32)
        sg = _sigmoid(p_ref[:, D:2 * D].astype(F32))
        da = dglu * sg
        dgate = dglu * a * (sg * (1.0 - sg))
        dp_ref[:, 0:D] = da.astype(BF16)
        dp_ref[:, D:2 * D] = dgate.astype(BF16)
        _acc(gbin_ref.at[:, 0:D], first, _colsum(da))
        _acc(gbin_ref.at[:, D:2 * D], first, _colsum(dgate))

    return pl.pallas_call(
        body, name="dwconv_bwd", grid=(nt,),
        out_shape=[jax.ShapeDtypeStruct((T, 2 * D), BF16), jax.ShapeDtypeStruct((Kp, D), F32),
                   jax.ShapeDtypeStruct((1, 2 * D), F32)],
        in_specs=[_rows(tm, D), pl.BlockSpec((H, D), lambda i: (jnp.minimum((i + 1) * r, last), 0)),
                  _rows(tm, D), pl.BlockSpec((H, D), lambda i: (jnp.maximum(i * r - 1, 0), 0)),
                  _rows(tm, 2 * D), _full((K, D))],
        out_specs=[_rows(tm, 2 * D), _full((Kp, D)), _full((1, 2 * D))],
        scratch_shapes=[pltpu.VMEM((tm + H, D), F32), pltpu.VMEM((tm + H, D), F32), pltpu.VMEM((tm, D), F32),
                        pltpu.VMEM((K, SUBLANES, D), F32)],
        compiler_params=_params("arbitrary"),
    )(dc, dc, glu, glu, p, dw)


def _dx_proj(name, dz, dy, wb, z_in, g_in, alpha, tm, deps=()):
    T, D = dz.shape
    nb, _, Ns = wb.shape
    N = nb * Ns

    def body(dz_ref, dy_ref, w_ref, zin_ref, gin_ref, dzo_ref, dzob_ref, gg_ref, gb_ref, sdz_ref):
        dx = alpha * dz_ref[...]
        for j in range(nb):
            dx = dx + _mm_nt(dy_ref[:, j * Ns:(j + 1) * Ns], w_ref[j])
        _ln_bwd_store(dx, zin_ref, gin_ref, pl.program_id(0) == 0, dzo_ref, dzob_ref, gg_ref, gb_ref, sdz_ref)

    return _tied_call(
        body, deps, name=name, grid=(T // tm,),
        out_shape=_ln_bwd_outs(T, D),
        in_specs=[_rows(tm, D), _rows(tm, N), _full(wb.shape), _rows(tm, D), _full((1, D))],
        out_specs=[_rows(tm, D)] * 2 + [_full((1, D))] * 3,
        compiler_params=_params("arbitrary"),
    )(dz, dy, wb, z_in, g_in)


def _pool_bwd(dz, x, pw, scale, alpha, tm, deps=()):
    T, D = x.shape
    G = len(POOL_WINDOWS)
    Dg = D // G
    H = POOL_HALO
    r = tm // H
    last = T // H - 1
    nt = T // tm
    n_ext = tm + H

    def body(dz_ref, dzh_ref, x_ref, xh_ref, pw_ref, sc_ref, dx_ref, gpw_ref, gsc_ref, edz_ref, ex_ref):
        i = pl.program_id(0)
        first = i == 0
        edz_ref[0:tm, :] = dz_ref[...]
        edz_ref[tm:, :] = jnp.where(i < nt - 1, dzh_ref[...], 0.0)
        ex_ref[0:H, :] = jnp.where(i > 0, xh_ref[...], 0.0)
        ex_ref[H:, :] = x_ref[...]
        row = i * tm + lax.broadcasted_iota(jnp.int32, (tm, 1), 0)
        rowf = (row + 1).astype(F32)
        erow = i * tm + lax.broadcasted_iota(jnp.int32, (n_ext, 1), 0)
        erowf = (erow + 1).astype(F32)
        g_scale, g_pw = [], []
        for gi, w in enumerate(POOL_WINDOWS):
            sl = slice(gi * Dg, (gi + 1) * Dg)
            s = ex_ref[:, sl]
            k = 1
            while k < w:
                s = s + pltpu.roll(s, k, 0)
                k *= 2
            xg = x_ref[:, sl]
            d = (s[H:, :] * (1.0 / jnp.minimum(rowf, float(w))) - xg).astype(BF16)
            wg = pw_ref[gi]
            premix = _mm(d, wg)
            dzg = dz_ref[:, sl]
            g_scale.append(_colsum(dzg * premix))
            dpre = edz_ref[:, sl] * sc_ref[:, sl]
            dpre_b = dpre.astype(BF16)
            g_pw.append(_mm_tn(d, dpre_b[0:tm, :]))
            dd = _mm_nt(dpre_b, wg)
            e = dd * (1.0 / jnp.minimum(erowf, float(w)))
            k = 1
            while k < w:
                e = e + pltpu.roll(e, n_ext - k, 0)
                k *= 2
            dx_ref[:, sl] = alpha * dzg + e[0:tm, :] - dd[0:tm, :]
        _acc(gsc_ref, first, jnp.concatenate(g_scale, axis=1))

        @pl.when(first)
        def _():
            for gi in range(G):
                gpw_ref[gi] = g_pw[gi]

        @pl.when(jnp.logical_not(first))
        def _():
            for gi in range(G):
                gpw_ref[gi] += g_pw[gi]

    return _tied_call(
        body, deps, name="pool_bwd", grid=(nt,),
        out_shape=[jax.ShapeDtypeStruct((T, D), F32), jax.ShapeDtypeStruct((G, Dg, Dg), F32),
                   jax.ShapeDtypeStruct((1, D), F32)],
        in_specs=[_rows(tm, D), pl.BlockSpec((H, D), lambda i: (jnp.minimum((i + 1) * r, last), 0)),
                  _rows(tm, D), pl.BlockSpec((H, D), lambda i: (jnp.maximum(i * r - 1, 0), 0)),
                  _full(pw.shape), _full((1, D))],
        out_specs=[_rows(tm, D), _full((G, Dg, Dg)), _full((1, D))],
        scratch_shapes=[pltpu.VMEM((n_ext, D), F32), pltpu.VMEM((n_ext, D), F32)],
        compiler_params=_params("arbitrary"),
    )(dz, dz, x, x, pw, scale)


def _adamw(name, recv, w, m, v, tm, layer=None, prev=None):
    R, C = w.shape[-2:]
    c1 = 1.0 - ADAM_B1 ** ADAM_STEP
    c2 = 1.0 - ADAM_B2 ** ADAM_STEP
    if layer is None:
        spec = _rows(tm, C)
    else:
        spec = pl.BlockSpec((None, tm, C), lambda i: (layer, i, 0))
    prev = list(prev) if prev is not None else []

    def body(r_ref, w_ref, m_ref, v_ref, *rest):
        g_ref, d_ref, nm_ref, nv_ref = rest[len(prev):]
        g = r_ref[0].astype(F32)
        for s in range(1, N_DEV):
            g = g + r_ref[s].astype(F32)
        m1 = ADAM_B1 * m_ref[...] + (1.0 - ADAM_B1) * g
        v1 = ADAM_B2 * v_ref[...] + (1.0 - ADAM_B2) * (g * g)
        m_hat = m1 / c1
        v_hat = v1 / c2
        g_ref[...] = g
        d_ref[...] = -ADAM_LR * (m_hat / (jnp.sqrt(v_hat) + ADAM_EPS) + ADAM_WD * w_ref[...])
        nm_ref[...] = m1
        nv_ref[...] = v1

    return pl.pallas_call(
        body, name=name, grid=(R // tm,),
        out_shape=[jax.ShapeDtypeStruct(w.shape, F32)] * 4,
        in_specs=[pl.BlockSpec((N_DEV, tm, C), lambda i: (0, i, 0))] + [spec] * 3
        + [pl.BlockSpec(memory_space=pl.ANY)] * len(prev),
        out_specs=[spec] * 4,
        input_output_aliases={4 + j: j for j in range(len(prev))},
        compiler_params=_params("parallel"),
    )(recv, w, m, v, *prev)


def _adamw_small(name, recv, w, m, v, pieces):
    R, C = w.shape
    c1 = 1.0 - ADAM_B1 ** ADAM_STEP
    c2 = 1.0 - ADAM_B2 ** ADAM_STEP
    n = len(pieces)

    def body(r_ref, w_ref, m_ref, v_ref, *rest):
        outs, packed = rest[:4 * n], rest[4 * n]
        g = r_ref[0]
        for s in range(1, N_DEV):
            g = g + r_ref[s]
        m1 = ADAM_B1 * m_ref[...] + (1.0 - ADAM_B1) * g
        v1 = ADAM_B2 * v_ref[...] + (1.0 - ADAM_B2) * (g * g)
        packed[0] = g
        packed[1] = -ADAM_LR * ((m1 / c1) / (jnp.sqrt(v1 / c2) + ADAM_EPS) + ADAM_WD * w_ref[...])
        packed[2] = m1
        packed[3] = v1
        for kind in range(4):
            for p, (r0, rows, shape) in enumerate(pieces):
                o_ref = outs[kind * n + p]
                if shape[-1] == C:
                    o_ref[...] = packed[kind, r0:r0 + rows, :].reshape(shape)
                else:
                    per = shape[-1] // C
                    for idx in range(rows):
                        l, q = divmod(idx, per)
                        o_ref[l:l + 1, q * C:(q + 1) * C] = packed[kind, r0 + idx:r0 + idx + 1, :]

    outs = pl.pallas_call(
        body, name=name,
        out_shape=[jax.ShapeDtypeStruct(shape, F32) for _ in range(4) for _, _, shape in pieces],
        scratch_shapes=[pltpu.VMEM((4, R, C), F32)],
    )(recv, w, m, v)
    return [outs[k * n:(k + 1) * n] for k in range(4)]


def _pad_rows(a, rows):
    return jnp.pad(a, ((0, rows - a.shape[0]), (0, 0)))


def kernel(x, pool_w, pool_scale, conv_w_in, conv_b_in, conv_dw, conv_dw_b, conv_ln_g, conv_ln_b, conv_w_out, conv_b_out, mix_ln_g, mix_ln_b, mlp_w1, mlp_b1, mlp_w2, mlp_b2, mlp_ln_g, mlp_ln_b, loss_target, m_pool_w, m_pool_scale, m_conv_w_in, m_conv_b_in, m_conv_dw, m_conv_dw_b, m_conv_ln_g, m_conv_ln_b, m_conv_w_out, m_conv_b_out, m_mix_ln_g, m_mix_ln_b, m_mlp_w1, m_mlp_b1, m_mlp_w2, m_mlp_b2, m_mlp_ln_g, m_mlp_ln_b, v_pool_w, v_pool_scale, v_conv_w_in, v_conv_b_in, v_conv_dw, v_conv_dw_b, v_conv_ln_g, v_conv_ln_b, v_conv_w_out, v_conv_b_out, v_mix_ln_g, v_mix_ln_b, v_mlp_w1, v_mlp_b1, v_mlp_w2, v_mlp_b2, v_mlp_ln_g, v_mlp_ln_b):
    _, T, D = x.shape
    L = mlp_w1.shape[0]
    assert L == 2 and pool_w.shape[0] == 1 and conv_w_in.shape[0] == 1
    G = pool_w.shape[1]
    Dg = D // G
    Fd = mlp_b1.shape[1]
    Fs = Fd // N_DEV
    Kc = conv_dw.shape[1]
    Dc = D // N_DEV
    alpha = float((2.0 * L) ** 0.25)
    x2d, tgt = x[0], loss_target[0]

    tm = _tile(T, 512)
    tm_wide = _tile(T, 512)
    tm_conv = _tile(T, 256)
    tm_wg = _tile(T, 1024)

    def pack_sh(dw, dwb, lg, lb, bo, bi):
        rows = jnp.concatenate([dw[0], dwb, lg, lb, bo, bi.reshape(2, Dc)], axis=0)
        return _pad_rows(rows, SH_ROWS)

    SH_ROWS = -(-(Kc + 6) // 8) * 8
    def pack_rep(ps, mg, mb, b1, b2, lg, lb):
        rows = jnp.concatenate([ps, mg, mb, b1.reshape(L * Fd // D, D), b2, lg, lb], axis=0)
        return _pad_rows(rows, REP_ROWS)

    n_rep = 1 + 2 * L + L * Fd // D + 3 * L
    REP_ROWS = -(-n_rep // 8) * 8

    w_sh = pack_sh(conv_dw, conv_dw_b, conv_ln_g, conv_ln_b, conv_b_out, conv_b_in)
    m_sh = pack_sh(m_conv_dw, m_conv_dw_b, m_conv_ln_g, m_conv_ln_b, m_conv_b_out, m_conv_b_in)
    v_sh = pack_sh(v_conv_dw, v_conv_dw_b, v_conv_ln_g, v_conv_ln_b, v_conv_b_out, v_conv_b_in)
    w_rep = pack_rep(pool_scale, mix_ln_g, mix_ln_b, mlp_b1, mlp_b2, mlp_ln_g, mlp_ln_b)
    m_rep = pack_rep(m_pool_scale, m_mix_ln_g, m_mix_ln_b, m_mlp_b1, m_mlp_b2, m_mlp_ln_g, m_mlp_ln_b)
    v_rep = pack_rep(v_pool_scale, v_mix_ln_g, v_mix_ln_b, v_mlp_b1, v_mlp_b2, v_mlp_ln_g, v_mlp_ln_b)

    groups = [[pool_w[0]], [mlp_w1[0]], [mlp_w2[0]], [conv_w_in[0], conv_w_out[0]], [mlp_w1[1]], [mlp_w2[1]]]
    handles, tokens = [], []
    for i, grp in enumerate(groups):
        tie = tokens[-1][0, 0] if tokens else 0.0
        srcs = [(a + tie).astype(BF16) for a in grp] + ([w_sh] if i == 0 else [])
        h, tk = _xstart("gather_%d" % i, srcs, [False] * len(srcs))
        handles.append(h)
        tokens.append(tk)
    pw_all, sh_all = _xwait(handles[0], tokens[-1])
    pw = pw_all.transpose(1, 0, 2, 3).reshape(G, Dg, Dg)
    dw_full = sh_all[:, 0:Kc].transpose(1, 0, 2).reshape(Kc, D)

    def sh_row(i):
        return sh_all[:, i].reshape(1, D)

    dwb_full, cg_full, cb_full, bout_full = (sh_row(Kc + i) for i in range(4))
    bin_full = sh_all[:, Kc + 4:Kc + 6].reshape(1, 2 * D)

    h0, z_m0 = _pool_fwd(x2d, pw, pool_scale, mix_ln_g[0:1], mix_ln_b[0:1], alpha, tm, tuple(tokens))
    (w1b0,) = _xwait(handles[1], h0)
    u0 = _mlp_up("mlp_up0", h0, w1b0, mlp_b1[0:1], tm)
    (w2b0,) = _xwait(handles[2], u0)
    x1, z_f0 = _proj_ln("mlp_down0", u0, w2b0.reshape(Fd, D), mlp_b2[0:1], h0, mlp_ln_g[0:1], mlp_ln_b[0:1],
                        alpha, tm, act=True)
    win_b, wout_all = _xwait(handles[3], x1)
    w_out = wout_all.reshape(D, D)
    p, glu = _conv_in(x1, win_b, bin_full, tm)
    cz, s = _dwconv_fwd(glu, dw_full, dwb_full, cg_full, cb_full, tm_conv)
    h1, z_m1 = _proj_ln("conv_out", s, w_out, bout_full, x1, mix_ln_g[1:2], mix_ln_b[1:2], alpha, tm)
    (w1b1,) = _xwait(handles[4], h1)
    u1 = _mlp_up("mlp_up1", h1, w1b1, mlp_b1[1:2], tm)
    (w2b1,) = _xwait(handles[5], u1)

    dz, dzb, g_fg1, g_fb1, g_b2_1, loss_cols = _proj_ln_loss(
        "mlp_down1", u1, w2b1.reshape(Fd, D), mlp_b2[1:2], h1, mlp_ln_g[1:2], mlp_ln_b[1:2], tgt, alpha, tm)
    gw2_1 = _wgrad("gw2_1", u1, dzb, False, 4, 1, tm_wg, act=True)
    e_w2_1, tk = _xstart("grads_w2_1", [gw2_1], [True])
    du, g_b1_1 = _mlp_act_bwd("mlp_act_bwd1", dzb, u1, w2b1, tm, (tk,))
    dz, dzb, g_mg1, g_mb1, g_bout = _dx_proj("mlp_in_bwd1", dz, du, w1b1, z_m1, mix_ln_g[1:2], alpha, tm_wide)
    gw1_1 = _wgrad("gw1_1", h1, du, True, 1, 2, tm_wg)
    e_w1_1, tk = _xstart("grads_w1_1", [gw1_1], [True])
    gwout = _wgrad("gw_out", s, dzb, False, 1, 1, tm_wg)
    dc, g_cg, g_cb, g_dwb = _conv_out_bwd(dzb, w_out, cz, cg_full, cb_full, tm, (tk,))
    dp, g_dw, g_bin = _dwconv_bwd(dc, glu, p, dw_full, tm_conv)
    gwin = _wgrad("gw_in", x1, dp, True, 1, 2, tm_wg)
    e_conv, tk = _xstart("grads_conv", [gwin, gwout], [True, True])
    dz, dzb, g_fg0, g_fb0, g_b2_0 = _dx_proj("conv_in_bwd", dz, dp, win_b, z_f0, mlp_ln_g[0:1], alpha, tm, (tk,))
    gw2_0 = _wgrad("gw2_0", u0, dzb, False, 4, 1, tm_wg, act=True)
    e_w2_0, tk = _xstart("grads_w2_0", [gw2_0], [True])
    du, g_b1_0 = _mlp_act_bwd("mlp_act_bwd0", dzb, u0, w2b0, tm, (tk,))
    dz, dzb, g_mg0, g_mb0, _ = _dx_proj("mlp_in_bwd0", dz, du, w1b0, z_m0, mix_ln_g[0:1], alpha, tm_wide)
    gw1_0 = _wgrad("gw1_0", h0, du, True, 1, 2, tm_wg)
    e_w1_0, tk = _xstart("grads_w1_0", [gw1_0], [True])
    grad_x, g_pw, g_ps = _pool_bwd(dz, x2d, pw, pool_scale, alpha, tm, (tk,))

    loss = lax.psum(0.5 / D * jnp.sum(loss_cols), MESH_AXES)

    gpw_b = g_pw.reshape(G, N_DEV, Dg // N_DEV, Dg).transpose(1, 0, 2, 3).astype(BF16)

    def to_dev(vec, rows):
        return vec.reshape(rows, N_DEV, Dc).transpose(1, 0, 2)

    g_sh = jnp.concatenate(
        [to_dev(g_dw[0:Kc], Kc), to_dev(g_dwb, 1), to_dev(g_cg, 1), to_dev(g_cb, 1), to_dev(g_bout, 1),
         g_bin.reshape(N_DEV, 2, Dc), jnp.zeros((N_DEV, SH_ROWS - Kc - 6, Dc), F32)], axis=1)
    g_rep = _pad_rows(jnp.concatenate(
        [g_ps, g_mg0, g_mg1, g_mb0, g_mb1, g_b1_0.reshape(Fd // D, D), g_b1_1.reshape(Fd // D, D),
         g_b2_0, g_b2_1, g_fg0, g_fg1, g_fb0, g_fb1], axis=0), REP_ROWS)

    e_small, tk = _xstart("grads_small", [gpw_b, g_sh, g_rep], [True, True, False])

    def upd(name, recv, w, m, v):
        shape = w.shape
        C = shape[-1]
        R = w.size // C
        outs = _adamw(name, recv.reshape(N_DEV, R, C), w.reshape(R, C), m.reshape(R, C), v.reshape(R, C), _tile(R, 256))
        return [o.reshape(shape) for o in outs]

    def upd_layer(name, recv, w, m, v, layer, prev):
        return _adamw(name, recv, w, m, v, _tile(w.shape[1], 256), layer, prev)

    (r_w2_1,) = _xwait(e_w2_1, tk)
    o_w2 = upd_layer("adam_w2_1", r_w2_1, mlp_w2, m_mlp_w2, v_mlp_w2, 1, None)
    (r_w1_1,) = _xwait(e_w1_1, o_w2[0])
    o_w1 = upd_layer("adam_w1_1", r_w1_1, mlp_w1, m_mlp_w1, v_mlp_w1, 1, None)
    r_win, r_wout = _xwait(e_conv, o_w1[0])
    o_win = upd("adam_w_in", r_win, conv_w_in, m_conv_w_in, v_conv_w_in)
    o_wout = upd("adam_w_out", r_wout, conv_w_out, m_conv_w_out, v_conv_w_out)
    (r_w2_0,) = _xwait(e_w2_0, o_wout[0])
    o_w2 = upd_layer("adam_w2_0", r_w2_0, mlp_w2, m_mlp_w2, v_mlp_w2, 0, o_w2)
    (r_w1_0,) = _xwait(e_w1_0, o_w2[0])
    o_w1 = upd_layer("adam_w1_0", r_w1_0, mlp_w1, m_mlp_w1, v_mlp_w1, 0, o_w1)
    r_pw, r_sh, r_rep = _xwait(e_small, o_w1[0])
    o_pw = upd("adam_pool_w", r_pw, pool_w, m_pool_w, v_pool_w)
    sh_pieces = [(0, Kc, (1, Kc, Dc))] + [(Kc + i, 1, (1, Dc)) for i in range(4)] + [(Kc + 4, 2, (1, 2 * Dc))]
    rep_pieces, o = [], 0
    for rows, shape in ((1, (1, D)), (L, (L, D)), (L, (L, D)), (L * Fd // D, (L, Fd)), (L, (L, D)), (L, (L, D)),
                        (L, (L, D))):
        rep_pieces.append((o, rows, shape))
        o += rows
    o_sh = _adamw_small("adam_conv_vec", r_sh, w_sh, m_sh, v_sh, sh_pieces)
    o_rep = _adamw_small("adam_replicated", r_rep, w_rep, m_rep, v_rep, rep_pieces)

    results = []
    for kind in range(4):
        dwv, dwb, lg, lb, bo, bi = o_sh[kind]
        ps, mg, mb, b1, b2, fg, fb = o_rep[kind]
        results.append([o_pw[kind], ps, o_win[kind], bi, dwv, dwb, lg, lb, o_wout[kind], bo, mg, mb,
                        o_w1[kind], b1, o_w2[kind], b2, fg, fb])
    return (loss, grad_x[None], *results[0], *results[1], *results[2], *results[3])
```

```python
import jax
import jax.numpy as jnp
from jax import lax
from jax.experimental import pallas as pl
from jax.experimental.pallas import tpu as pltpu

N_DEV = 8
MESH_AXES = ("x", "y", "c")
POOL_WINDOWS = (2, 4, 8, 16)
POOL_HALO = 16
CONV_HALO = 32
LN_EPS = 1e-5
ADAM_LR = 0.001
ADAM_B1 = 0.9
ADAM_B2 = 0.999
ADAM_EPS = 1e-08
ADAM_WD = 0.01
ADAM_STEP = 10
VMEM_LIMIT = 56 * 1024 * 1024

F32 = jnp.float32
BF16 = jnp.bfloat16


def _mm(a, b):
    return lax.dot_general(a, b, (((1,), (0,)), ((), ())), preferred_element_type=F32)


def _mm_nt(a, b):
    return lax.dot_general(a, b, (((1,), (1,)), ((), ())), preferred_element_type=F32)


def _mm_tn(a, b):
    return lax.dot_general(a, b, (((0,), (0,)), ((), ())), preferred_element_type=F32)


def _tile(n, pref):
    t = min(n, pref)
    assert n % t == 0, (n, pref)
    return t


def _params(*sem):
    return pltpu.CompilerParams(dimension_semantics=sem, vmem_limit_bytes=VMEM_LIMIT)


def _full(shape):
    nd = len(shape)
    return pl.BlockSpec(shape, lambda *_: (0,) * nd)


def _rows(tm, d):
    return pl.BlockSpec((tm, d), lambda i: (i, 0))


def _ln_stats(z):
    mu = jnp.mean(z, axis=-1, keepdims=True)
    zc = z - mu
    var = jnp.mean(zc * zc, axis=-1, keepdims=True)
    rstd = lax.rsqrt(var + LN_EPS)
    return zc * rstd, rstd


def _ln_fwd(z, g, b):
    xhat, _ = _ln_stats(z)
    return xhat * g + b


def _ln_bwd(dy, z, g):
    xhat, rstd = _ln_stats(z)
    dxh = dy * g
    m1 = jnp.mean(dxh, axis=-1, keepdims=True)
    m2 = jnp.mean(dxh * xhat, axis=-1, keepdims=True)
    return rstd * (dxh - m1 - xhat * m2), xhat


def _colsum(v):
    return jnp.sum(v, axis=0, keepdims=True)


def _sigmoid(v):
    return 1.0 / (1.0 + jnp.exp(-v))


def _acc(ref, first, val):
    @pl.when(first)
    def _():
        ref[...] = val

    @pl.when(jnp.logical_not(first))
    def _():
        ref[...] += val


_HBM = pl.BlockSpec(memory_space=pltpu.HBM)
_SEM = pl.BlockSpec(memory_space=pltpu.SEMAPHORE)
_EFFECT = pltpu.SideEffectType.DATAFLOW_SIDE_EFFECTING


def _peers():
    x, y, c = (lax.axis_index(a) for a in MESH_AXES)
    out = []
    for d in range(1, N_DEV):
        px = (x + ((d >> 2) & 1)) % 2
        py = (y + ((d >> 1) & 1)) % 2
        pc = (c + (d & 1)) % 2
        out.append((d - 1, (px, py, pc), 4 * px + 2 * py + pc))
    return 4 * x + 2 * y + c, out


def _remote_copies(src_refs, land_refs, scatter, send_sems, recv_sems):
    me, peers = _peers()
    copies = []
    for i, pos, pid in peers:
        for k, (src, land) in enumerate(zip(src_refs, land_refs)):
            copies.append(pltpu.make_async_remote_copy(
                src_ref=src.at[pid] if scatter[k] else src, dst_ref=land.at[me],
                send_sem=send_sems.at[k * (N_DEV - 1) + i], recv_sem=recv_sems.at[k * (N_DEV - 1) + i],
                device_id=pos, device_id_type=pl.DeviceIdType.MESH))
    return copies


def _xstart(name, srcs, scatter):
    n = len(srcs)
    me = 4 * lax.axis_index("x") + 2 * lax.axis_index("y") + lax.axis_index("c")
    lands = []
    for s, sc in zip(srcs, scatter):
        own = lax.dynamic_index_in_dim(s, me, 0, keepdims=True) if sc else s[None]
        shape = s.shape if sc else (N_DEV,) + s.shape
        lands.append(lax.dynamic_update_slice(lax.empty(shape, s.dtype), own, (me,) + (0,) * (len(shape) - 1)))

    def body(*refs):
        src_refs, land_refs = refs[:n], refs[n:2 * n]
        send_sems, recv_sems = refs[2 * n], refs[2 * n + 1]
        token = refs[-1]
        for cp in _remote_copies(src_refs, land_refs, scatter, send_sems, recv_sems):
            cp.start()
        token[...] = jnp.zeros_like(token)

    outs = pl.pallas_call(
        body, name=name,
        out_shape=(pltpu.SemaphoreType.DMA((n * (N_DEV - 1),)), pltpu.SemaphoreType.DMA((n * (N_DEV - 1),)),
                   *[pltpu.HBM(a.shape, a.dtype) for a in srcs + lands], jax.ShapeDtypeStruct((8, 128), F32)),
        in_specs=(_HBM,) * (2 * n),
        out_specs=(_SEM, _SEM) + (_HBM,) * (2 * n) + (pl.BlockSpec(memory_space=pltpu.VMEM),),
        input_output_aliases={i: 2 + i for i in range(2 * n)},
        compiler_params=pltpu.CompilerParams(has_side_effects=_EFFECT),
    )(*[pltpu.with_memory_space_constraint(a, pltpu.HBM) for a in srcs + lands])
    return (name, scatter, outs[0], outs[1], outs[2:2 + n], outs[2 + n:2 + 2 * n]), outs[-1]


def _xwait(handle, after):
    name, scatter, send_sems, recv_sems, srcs, lands = handle
    n = len(srcs)

    def body(*refs):
        src_refs, land_refs = refs[:n], refs[n:2 * n]
        send, recv = refs[2 * n], refs[2 * n + 1]
        copies = _remote_copies(src_refs, land_refs, scatter, send, recv)
        for cp in copies:
            cp.wait_send()
        for cp in copies:
            cp.wait_recv()

    outs = pl.pallas_call(
        body, name=name + "_wait",
        out_shape=tuple(pltpu.HBM(a.shape, a.dtype) for a in (*srcs, *lands)),
        in_specs=(_HBM,) * (2 * n) + (_SEM, _SEM, pl.BlockSpec(memory_space=pl.ANY)),
        out_specs=(_HBM,) * (2 * n),
        input_output_aliases={i: i for i in range(2 * n)},
        compiler_params=pltpu.CompilerParams(has_side_effects=_EFFECT),
    )(*srcs, *lands, send_sems, recv_sems, after)
    return outs[n:]


def _tied_call(body, deps, in_specs, **kw):
    nd = len(deps)

    def tied_body(*refs):
        body(*refs[nd:])

    call = pl.pallas_call(tied_body, in_specs=[pl.BlockSpec(memory_space=pl.ANY)] * nd + list(in_specs), **kw)
    return lambda *args: call(*deps, *args)


def _pool_fwd(x, pw, scale, g, b, alpha, tm, deps=()):
    T, D = x.shape
    G = len(POOL_WINDOWS)
    Dg = D // G
    H = POOL_HALO
    r = tm // H

    def body(x_ref, xh_ref, pw_ref, sc_ref, g_ref, b_ref, y_ref, z_ref, ext_ref):
        i = pl.program_id(0)
        ext_ref[0:H, :] = jnp.where(i > 0, xh_ref[...], 0.0)
        ext_ref[H:, :] = x_ref[...]
        row = i * tm + lax.broadcasted_iota(jnp.int32, (tm, 1), 0)
        rowf = (row + 1).astype(F32)
        for gi, w in enumerate(POOL_WINDOWS):
            sl = slice(gi * Dg, (gi + 1) * Dg)
            s = ext_ref[:, sl]
            k = 1
            while k < w:
                s = s + pltpu.roll(s, k, 0)
                k *= 2
            inv = 1.0 / jnp.minimum(rowf, float(w))
            xg = x_ref[:, sl]
            d = s[H:, :] * inv - xg
            mix = _mm(d.astype(BF16), pw_ref[gi]) * sc_ref[:, sl]
            z_ref[:, sl] = alpha * xg + mix
        y_ref[...] = _ln_fwd(z_ref[...], g_ref[...], b_ref[...])

    return _tied_call(
        body, deps, name="pool_fwd", grid=(T // tm,),
        out_shape=[jax.ShapeDtypeStruct((T, D), F32)] * 2,
        in_specs=[_rows(tm, D), pl.BlockSpec((H, D), lambda i: (jnp.maximum(i * r - 1, 0), 0)),
                  _full(pw.shape), _full((1, D)), _full((1, D)), _full((1, D))],
        out_specs=[_rows(tm, D)] * 2,
        scratch_shapes=[pltpu.VMEM((tm + H, D), F32)],
        compiler_params=_params("parallel"),
    )(x, x, pw, scale, g, b)


def _mlp_up(name, h, w1b, b1, tm):
    T, D = h.shape
    nb, _, Fs = w1b.shape
    Fd = nb * Fs

    def body(h_ref, w_ref, b_ref, u_ref):
        hb = h_ref[...].astype(BF16)
        for j in range(nb):
            sl = slice(j * Fs, (j + 1) * Fs)
            u_ref[:, sl] = (_mm(hb, w_ref[j]) + b_ref[:, sl]).astype(BF16)

    return pl.pallas_call(
        body, name=name, grid=(T // tm,),
        out_shape=jax.ShapeDtypeStruct((T, Fd), BF16),
        in_specs=[_rows(tm, D), _full(w1b.shape), _full((1, Fd))],
        out_specs=_rows(tm, Fd),
        compiler_params=_params("parallel"),
    )(h, w1b, b1)


def _sq_relu(u):
    r = jnp.maximum(u, 0.0)
    return r * r


def _proj_ln(name, a, w, bias, res, g, b, alpha, tm, act=False):
    T, K = a.shape
    D = w.shape[1]

    def body(a_ref, w_ref, bias_ref, res_ref, g_ref, b_ref, y_ref, z_ref):
        av = _sq_relu(a_ref[...]) if act else a_ref[...]
        z = alpha * res_ref[...] + _mm(av, w_ref[...]) + bias_ref[...]
        z_ref[...] = z
        y_ref[...] = _ln_fwd(z, g_ref[...], b_ref[...])

    return pl.pallas_call(
        body, name=name, grid=(T // tm,),
        out_shape=[jax.ShapeDtypeStruct((T, D), F32)] * 2,
        in_specs=[_rows(tm, K), _full((K, D)), _full((1, D)), _rows(tm, D), _full((1, D)), _full((1, D))],
        out_specs=[_rows(tm, D)] * 2,
        compiler_params=_params("parallel"),
    )(a, w, bias, res, g, b)


def _proj_ln_loss(name, a, w, bias, res, g, b, target, alpha, tm):
    T, K = a.shape
    D = w.shape[1]

    def body(a_ref, w_ref, bias_ref, res_ref, g_ref, b_ref, t_ref, dz_ref, dzb_ref, gg_ref, gb_ref, sdz_ref, loss_ref):
        first = pl.program_id(0) == 0
        z = alpha * res_ref[...] + _mm(_sq_relu(a_ref[...]), w_ref[...]) + bias_ref[...]
        gv = g_ref[...]
        e = _ln_fwd(z, gv, b_ref[...]) - t_ref[...]
        dy = e * (1.0 / D)
        dz, xhat = _ln_bwd(dy, z, gv)
        dz_ref[...] = dz
        dzb_ref[...] = dz.astype(BF16)
        _acc(gg_ref, first, _colsum(dy * xhat))
        _acc(gb_ref, first, _colsum(dy))
        _acc(sdz_ref, first, _colsum(dz))
        _acc(loss_ref, first, _colsum(e * e))

    return pl.pallas_call(
        body, name=name, grid=(T // tm,),
        out_shape=[jax.ShapeDtypeStruct((T, D), F32), jax.ShapeDtypeStruct((T, D), BF16)]
        + [jax.ShapeDtypeStruct((1, D), F32)] * 4,
        in_specs=[_rows(tm, K), _full((K, D)), _full((1, D)), _rows(tm, D), _full((1, D)), _full((1, D)), _rows(tm, D)],
        out_specs=[_rows(tm, D)] * 2 + [_full((1, D))] * 4,
        compiler_params=_params("arbitrary"),
    )(a, w, bias, res, g, b, target)


def _conv_in(x, wb, b_in, tm):
    T, D = x.shape
    nb, _, Ns = wb.shape
    half = nb // 2

    def body(x_ref, w_ref, b_ref, p_ref, glu_ref):
        xb = x_ref[...].astype(BF16)
        for j in range(half):
            sa = slice(j * Ns, (j + 1) * Ns)
            sg = slice(D + j * Ns, D + (j + 1) * Ns)
            a = _mm(xb, w_ref[j]) + b_ref[:, sa]
            gate = _mm(xb, w_ref[half + j]) + b_ref[:, sg]
            p_ref[:, sa] = a.astype(BF16)
            p_ref[:, sg] = gate.astype(BF16)
            glu_ref[:, sa] = a * _sigmoid(gate)

    return pl.pallas_call(
        body, name="conv_in", grid=(T // tm,),
        out_shape=[jax.ShapeDtypeStruct((T, 2 * D), BF16), jax.ShapeDtypeStruct((T, D), F32)],
        in_specs=[_rows(tm, D), _full(wb.shape), _full((1, 2 * D))],
        out_specs=[_rows(tm, 2 * D), _rows(tm, D)],
        compiler_params=_params("parallel"),
    )(x, wb, b_in)


def _lane_chunk(d):
    return 128 if d % 128 == 0 else d


SUBLANES = 8


def _row_chunk(tm, pref):
    return pref if tm % pref == 0 else tm


def _slabs(ref, base, n_taps, r0, rows, cs):
    out = []
    for r in range(min(SUBLANES, n_taps)):
        nq = (n_taps - 1 - r) // SUBLANES + 1
        lo = base + r + r0
        slab = ref[lo:lo + rows + SUBLANES * (nq - 1), cs]
        out.append((slab, [(q, SUBLANES * q + r) for q in range(nq)]))
    return out


def _dwconv_fwd(glu, dw, dw_b, g, b, tm):
    T, D = glu.shape
    K = dw.shape[0]
    H = CONV_HALO
    off = H - (K - 1)
    r = tm // H
    cc = _lane_chunk(D)

    def body(x_ref, xh_ref, dw_ref, dwb_ref, g_ref, b_ref, cz_ref, s_ref, ext_ref, sh_ref):
        i = pl.program_id(0)
        ext_ref[0:H, :] = jnp.where(i > 0, xh_ref[...], 0.0)
        ext_ref[H:, :] = x_ref[...]
        for ph in range(min(SUBLANES, K)):
            n = tm + SUBLANES * ((K - 1 - ph) // SUBLANES)
            sh_ref[ph, 0:n, :] = ext_ref[off + ph:off + ph + n, :]
        rc = _row_chunk(tm, 128)
        for c0 in range(0, D, cc):
            cs = slice(c0, c0 + cc)
            for r0 in range(0, tm, rc):
                acc = jnp.zeros((rc, cc), F32) + dwb_ref[:, cs]
                for k in range(K):
                    q, ph = divmod(k, SUBLANES)
                    acc = acc + sh_ref[ph, SUBLANES * q + r0:SUBLANES * q + r0 + rc, cs] * dw_ref[k:k + 1, cs]
                cz_ref[r0:r0 + rc, cs] = acc
        ln = _ln_fwd(cz_ref[...], g_ref[...], b_ref[...])
        s_ref[...] = (ln * _sigmoid(ln)).astype(BF16)

    return pl.pallas_call(
        body, name="dwconv_fwd", grid=(T // tm,),
        out_shape=[jax.ShapeDtypeStruct((T, D), F32), jax.ShapeDtypeStruct((T, D), BF16)],
        in_specs=[_rows(tm, D), pl.BlockSpec((H, D), lambda i: (jnp.maximum(i * r - 1, 0), 0)),
                  _full((K, D)), _full((1, D)), _full((1, D)), _full((1, D))],
        out_specs=[_rows(tm, D)] * 2,
        scratch_shapes=[pltpu.VMEM((tm + H, D), F32),
                        pltpu.VMEM((SUBLANES, tm + SUBLANES * ((K - 1) // SUBLANES), D), F32)],
        compiler_params=_params("parallel"),
    )(glu, glu, dw, dw_b, g, b)


def _ln_bwd_store(dy, z_ref, g_ref, first, dz_ref, dzb_ref, gg_ref, gb_ref, sdz_ref):
    dz, xhat = _ln_bwd(dy, z_ref[...], g_ref[...])
    dz_ref[...] = dz
    dzb_ref[...] = dz.astype(BF16)
    _acc(gg_ref, first, _colsum(dy * xhat))
    _acc(gb_ref, first, _colsum(dy))
    _acc(sdz_ref, first, _colsum(dz))


def _ln_bwd_outs(T, D):
    shapes = [jax.ShapeDtypeStruct((T, D), F32), jax.ShapeDtypeStruct((T, D), BF16)] + [jax.ShapeDtypeStruct((1, D), F32)] * 3
    return shapes


def _mlp_act_bwd(name, dzb, u, w2b, tm, deps=()):
    T, D = dzb.shape
    nb, Fs, _ = w2b.shape
    Fd = nb * Fs

    def body(dzb_ref, u_ref, w2_ref, du_ref, gb1_ref):
        dzv = dzb_ref[...]
        sums = []
        for j in range(nb):
            sl = slice(j * Fs, (j + 1) * Fs)
            du = _mm_nt(dzv, w2_ref[j]) * (2.0 * jnp.maximum(u_ref[:, sl].astype(F32), 0.0))
            du_ref[:, sl] = du.astype(BF16)
            sums.append(_colsum(du))
        _acc(gb1_ref, pl.program_id(0) == 0, jnp.concatenate(sums, axis=1))

    return _tied_call(
        body, deps, name=name, grid=(T // tm,),
        out_shape=[jax.ShapeDtypeStruct((T, Fd), BF16), jax.ShapeDtypeStruct((1, Fd), F32)],
        in_specs=[_rows(tm, D), _rows(tm, Fd), _full(w2b.shape)],
        out_specs=[_rows(tm, Fd), _full((1, Fd))],
        compiler_params=_params("arbitrary"),
    )(dzb, u, w2b)


def _wgrad(name, xm, dy, col_blocks, nk, nj, tm, act=False):
    T, K = xm.shape
    N = dy.shape[1]
    Kb, Nb = K // nk, N // nj
    nt = T // tm
    if col_blocks:
        per, Ns = N_DEV // nj, N // N_DEV
        out_shape = (N_DEV, K, Ns)
        out_spec = pl.BlockSpec((per, Kb, Ns), lambda k, j, t: (j, k, 0))
    else:
        per, Ks = N_DEV // nk, K // N_DEV
        out_shape = (N_DEV, Ks, N)
        out_spec = pl.BlockSpec((per, Ks, Nb), lambda k, j, t: (k, 0, j))

    def body(x_ref, dy_ref, o_ref, acc_ref):
        t = pl.program_id(2)
        xb = _sq_relu(x_ref[...]) if act else x_ref[...].astype(BF16)
        _acc(acc_ref, t == 0, _mm_tn(xb, dy_ref[...]))

        @pl.when(t == nt - 1)
        def _():
            for q in range(per):
                if col_blocks:
                    o_ref[q] = acc_ref[:, q * Ns:(q + 1) * Ns].astype(BF16)
                else:
                    o_ref[q] = acc_ref[q * Ks:(q + 1) * Ks, :].astype(BF16)

    return pl.pallas_call(
        body, name=name, grid=(nk, nj, nt),
        out_shape=jax.ShapeDtypeStruct(out_shape, BF16),
        in_specs=[pl.BlockSpec((tm, Kb), lambda k, j, t: (t, k)), pl.BlockSpec((tm, Nb), lambda k, j, t: (t, j))],
        out_specs=out_spec,
        scratch_shapes=[pltpu.VMEM((Kb, Nb), F32)],
        compiler_params=_params("parallel", "parallel", "arbitrary"),
    )(xm, dy)


def _conv_out_bwd(dzb, w_out, cz, g, b, tm, deps=()):
    T, D = cz.shape

    def body(dz_ref, w_ref, cz_ref, g_ref, b_ref, dc_ref, gg_ref, gb_ref, sdc_ref):
        first = pl.program_id(0) == 0
        ds = _mm_nt(dz_ref[...], w_ref[...])
        czv = cz_ref[...]
        gv = g_ref[...]
        ln = _ln_fwd(czv, gv, b_ref[...])
        sg = _sigmoid(ln)
        dln = ds * (sg * (1.0 + ln * (1.0 - sg)))
        dc, xhat = _ln_bwd(dln, czv, gv)
        dc_ref[...] = dc
        _acc(gg_ref, first, _colsum(dln * xhat))
        _acc(gb_ref, first, _colsum(dln))
        _acc(sdc_ref, first, _colsum(dc))

    return _tied_call(
        body, deps, name="conv_out_bwd", grid=(T // tm,),
        out_shape=[jax.ShapeDtypeStruct((T, D), F32)] + [jax.ShapeDtypeStruct((1, D), F32)] * 3,
        in_specs=[_rows(tm, D), _full((D, D)), _rows(tm, D), _full((1, D)), _full((1, D))],
        out_specs=[_rows(tm, D)] + [_full((1, D))] * 3,
        compiler_params=_params("arbitrary"),
    )(dzb, w_out, cz, g, b)


def _dwconv_bwd(dc, glu, p, dw, tm):
    T, D = dc.shape
    K = dw.shape[0]
    Kp = -(-K // 8) * 8
    H = CONV_HALO
    off = H - (K - 1)
    r = tm // H
    last = T // H - 1
    nt = T // tm
    cc = _lane_chunk(D)

    def body(dc_ref, dch_ref, x_ref, xh_ref, p_ref, dw_ref, dp_ref, gdw_ref, gbin_ref, edc_ref, ex_ref, dglu_ref,
             gacc_ref, shd_ref, shx_ref):
        i = pl.program_id(0)
        first = i == 0
        edc_ref[0:tm, :] = dc_ref[...]
        edc_ref[tm:, :] = jnp.where(i < nt - 1, dch_ref[...], 0.0)
        ex_ref[0:H, :] = jnp.where(i > 0, xh_ref[...], 0.0)
        ex_ref[H:, :] = x_ref[...]

        @pl.when(first)
        def _():
            gacc_ref[...] = jnp.zeros_like(gacc_ref)

        for ph in range(min(SUBLANES, K)):
            n = tm + SUBLANES * ((K - 1 - ph) // SUBLANES)
            shd_ref[ph, 0:n, :] = edc_ref[ph:ph + n, :]
            shx_ref[ph, 0:n, :] = ex_ref[off + ph:off + ph + n, :]
        rc = _row_chunk(tm, 64)
        for c0 in range(0, D, cc):
            cs = slice(c0, c0 + cc)
            for r0 in range(0, tm, rc):
                dcv = dc_ref[r0:r0 + rc, cs]
                acc = jnp.zeros((rc, cc), F32)
                for m in range(K):
                    q, ph = divmod(m, SUBLANES)
                    lo = SUBLANES * q + r0
                    acc = acc + shd_ref[ph, lo:lo + rc, cs] * dw_ref[K - 1 - m:K - m, cs]
                dglu_ref[r0:r0 + rc, cs] = acc
                for k in range(K):
                    q, ph = divmod(k, SUBLANES)
                    lo = SUBLANES * q + r0
                    part = (dcv * shx_ref[ph, lo:lo + rc, cs]).reshape(rc // SUBLANES, SUBLANES, cc)
                    gacc_ref[k, :, cs] += jnp.sum(part, axis=0)

        @pl.when(i == nt - 1)
        def _():
            gdw_ref[...] = jnp.zeros_like(gdw_ref)
            gdw_ref[0:K, :] = jnp.sum(gacc_ref[...], axis=1)
        dglu = dglu_ref[...]
        a = p_ref[:, 0:D].astype(F32)
        sg = _sigmoid(p_ref[:, D:2 * D].astype(F32))
        da = dglu * sg
        dgate = dglu * a * (sg * (1.0 - sg))
        dp_ref[:, 0:D] = da.astype(BF16)
        dp_ref[:, D:2 * D] = dgate.astype(BF16)
        _acc(gbin_ref.at[:, 0:D], first, _colsum(da))
        _acc(gbin_ref.at[:, D:2 * D], first, _colsum(dgate))

    return pl.pallas_call(
        body, name="dwconv_bwd", grid=(nt,),
        out_shape=[jax.ShapeDtypeStruct((T, 2 * D), BF16), jax.ShapeDtypeStruct((Kp, D), F32),
                   jax.ShapeDtypeStruct((1, 2 * D), F32)],
        in_specs=[_rows(tm, D), pl.BlockSpec((H, D), lambda i: (jnp.minimum((i + 1) * r, last), 0)),
                  _rows(tm, D), pl.BlockSpec((H, D), lambda i: (jnp.maximum(i * r - 1, 0), 0)),
                  _rows(tm, 2 * D), _full((K, D))],
        out_specs=[_rows(tm, 2 * D), _full((Kp, D)), _full((1, 2 * D))],
        scratch_shapes=[pltpu.VMEM((tm + H, D), F32), pltpu.VMEM((tm + H, D), F32), pltpu.VMEM((tm, D), F32),
                        pltpu.VMEM((K, SUBLANES, D), F32),
                        pltpu.VMEM((SUBLANES, tm + SUBLANES * ((K - 1) // SUBLANES), D), F32),
                        pltpu.VMEM((SUBLANES, tm + SUBLANES * ((K - 1) // SUBLANES), D), F32)],
        compiler_params=_params("arbitrary"),
    )(dc, dc, glu, glu, p, dw)


def _dx_proj(name, dz, dy, wb, z_in, g_in, alpha, tm, deps=()):
    T, D = dz.shape
    nb, _, Ns = wb.shape
    N = nb * Ns

    def body(dz_ref, dy_ref, w_ref, zin_ref, gin_ref, dzo_ref, dzob_ref, gg_ref, gb_ref, sdz_ref):
        dx = alpha * dz_ref[...]
        for j in range(nb):
            dx = dx + _mm_nt(dy_ref[:, j * Ns:(j + 1) * Ns], w_ref[j])
        _ln_bwd_store(dx, zin_ref, gin_ref, pl.program_id(0) == 0, dzo_ref, dzob_ref, gg_ref, gb_ref, sdz_ref)

    return _tied_call(
        body, deps, name=name, grid=(T // tm,),
        out_shape=_ln_bwd_outs(T, D),
        in_specs=[_rows(tm, D), _rows(tm, N), _full(wb.shape), _rows(tm, D), _full((1, D))],
        out_specs=[_rows(tm, D)] * 2 + [_full((1, D))] * 3,
        compiler_params=_params("arbitrary"),
    )(dz, dy, wb, z_in, g_in)


def _pool_bwd(dz, x, pw, scale, alpha, tm, deps=()):
    T, D = x.shape
    G = len(POOL_WINDOWS)
    Dg = D // G
    H = POOL_HALO
    r = tm // H
    last = T // H - 1
    nt = T // tm
    n_ext = tm + H

    def body(dz_ref, dzh_ref, x_ref, xh_ref, pw_ref, sc_ref, dx_ref, gpw_ref, gsc_ref, edz_ref, ex_ref):
        i = pl.program_id(0)
        first = i == 0
        edz_ref[0:tm, :] = dz_ref[...]
        edz_ref[tm:, :] = jnp.where(i < nt - 1, dzh_ref[...], 0.0)
        ex_ref[0:H, :] = jnp.where(i > 0, xh_ref[...], 0.0)
        ex_ref[H:, :] = x_ref[...]
        row = i * tm + lax.broadcasted_iota(jnp.int32, (tm, 1), 0)
        rowf = (row + 1).astype(F32)
        erow = i * tm + lax.broadcasted_iota(jnp.int32, (n_ext, 1), 0)
        erowf = (erow + 1).astype(F32)
        g_scale, g_pw = [], []
        for gi, w in enumerate(POOL_WINDOWS):
            sl = slice(gi * Dg, (gi + 1) * Dg)
            s = ex_ref[:, sl]
            k = 1
            while k < w:
                s = s + pltpu.roll(s, k, 0)
                k *= 2
            xg = x_ref[:, sl]
            d = (s[H:, :] * (1.0 / jnp.minimum(rowf, float(w))) - xg).astype(BF16)
            wg = pw_ref[gi]
            premix = _mm(d, wg)
            dzg = dz_ref[:, sl]
            g_scale.append(_colsum(dzg * premix))
            dpre = edz_ref[:, sl] * sc_ref[:, sl]
            dpre_b = dpre.astype(BF16)
            g_pw.append(_mm_tn(d, dpre_b[0:tm, :]))
            dd = _mm_nt(dpre_b, wg)
            e = dd * (1.0 / jnp.minimum(erowf, float(w)))
            k = 1
            while k < w:
                e = e + pltpu.roll(e, n_ext - k, 0)
                k *= 2
            dx_ref[:, sl] = alpha * dzg + e[0:tm, :] - dd[0:tm, :]
        _acc(gsc_ref, first, jnp.concatenate(g_scale, axis=1))

        @pl.when(first)
        def _():
            for gi in range(G):
                gpw_ref[gi] = g_pw[gi]

        @pl.when(jnp.logical_not(first))
        def _():
            for gi in range(G):
                gpw_ref[gi] += g_pw[gi]

    return _tied_call(
        body, deps, name="pool_bwd", grid=(nt,),
        out_shape=[jax.ShapeDtypeStruct((T, D), F32), jax.ShapeDtypeStruct((G, Dg, Dg), F32),
                   jax.ShapeDtypeStruct((1, D), F32)],
        in_specs=[_rows(tm, D), pl.BlockSpec((H, D), lambda i: (jnp.minimum((i + 1) * r, last), 0)),
                  _rows(tm, D), pl.BlockSpec((H, D), lambda i: (jnp.maximum(i * r - 1, 0), 0)),
                  _full(pw.shape), _full((1, D))],
        out_specs=[_rows(tm, D), _full((G, Dg, Dg)), _full((1, D))],
        scratch_shapes=[pltpu.VMEM((n_ext, D), F32), pltpu.VMEM((n_ext, D), F32)],
        compiler_params=_params("arbitrary"),
    )(dz, dz, x, x, pw, scale)


def _adamw(name, recv, w, m, v, tm, layer=None, prev=None):
    R, C = w.shape[-2:]
    c1 = 1.0 - ADAM_B1 ** ADAM_STEP
    c2 = 1.0 - ADAM_B2 ** ADAM_STEP
    if layer is None:
        spec = _rows(tm, C)
    else:
        spec = pl.BlockSpec((None, tm, C), lambda i: (layer, i, 0))
    prev = list(prev) if prev is not None else []

    def body(r_ref, w_ref, m_ref, v_ref, *rest):
        g_ref, d_ref, nm_ref, nv_ref = rest[len(prev):]
        g = r_ref[0].astype(F32)
        for s in range(1, N_DEV):
            g = g + r_ref[s].astype(F32)
        m1 = ADAM_B1 * m_ref[...] + (1.0 - ADAM_B1) * g
        v1 = ADAM_B2 * v_ref[...] + (1.0 - ADAM_B2) * (g * g)
        m_hat = m1 / c1
        v_hat = v1 / c2
        g_ref[...] = g
        d_ref[...] = -ADAM_LR * (m_hat / (jnp.sqrt(v_hat) + ADAM_EPS) + ADAM_WD * w_ref[...])
        nm_ref[...] = m1
        nv_ref[...] = v1

    return pl.pallas_call(
        body, name=name, grid=(R // tm,),
        out_shape=[jax.ShapeDtypeStruct(w.shape, F32)] * 4,
        in_specs=[pl.BlockSpec((N_DEV, tm, C), lambda i: (0, i, 0))] + [spec] * 3
        + [pl.BlockSpec(memory_space=pl.ANY)] * len(prev),
        out_specs=[spec] * 4,
        input_output_aliases={4 + j: j for j in range(len(prev))},
        compiler_params=_params("parallel"),
    )(recv, w, m, v, *prev)


def _adamw_small(name, recv, w, m, v, pieces):
    R, C = w.shape
    c1 = 1.0 - ADAM_B1 ** ADAM_STEP
    c2 = 1.0 - ADAM_B2 ** ADAM_STEP
    n = len(pieces)

    def body(r_ref, w_ref, m_ref, v_ref, *rest):
        outs, packed = rest[:4 * n], rest[4 * n]
        g = r_ref[0]
        for s in range(1, N_DEV):
            g = g + r_ref[s]
        m1 = ADAM_B1 * m_ref[...] + (1.0 - ADAM_B1) * g
        v1 = ADAM_B2 * v_ref[...] + (1.0 - ADAM_B2) * (g * g)
        packed[0] = g
        packed[1] = -ADAM_LR * ((m1 / c1) / (jnp.sqrt(v1 / c2) + ADAM_EPS) + ADAM_WD * w_ref[...])
        packed[2] = m1
        packed[3] = v1
        for kind in range(4):
            for p, (r0, rows, shape) in enumerate(pieces):
                o_ref = outs[kind * n + p]
                if shape[-1] == C:
                    o_ref[...] = packed[kind, r0:r0 + rows, :].reshape(shape)
                else:
                    per = shape[-1] // C
                    for idx in range(rows):
                        l, q = divmod(idx, per)
                        o_ref[l:l + 1, q * C:(q + 1) * C] = packed[kind, r0 + idx:r0 + idx + 1, :]

    outs = pl.pallas_call(
        body, name=name,
        out_shape=[jax.ShapeDtypeStruct(shape, F32) for _ in range(4) for _, _, shape in pieces],
        scratch_shapes=[pltpu.VMEM((4, R, C), F32)],
    )(recv, w, m, v)
    return [outs[k * n:(k + 1) * n] for k in range(4)]


def _pad_rows(a, rows):
    return jnp.pad(a, ((0, rows - a.shape[0]), (0, 0)))


def kernel(x, pool_w, pool_scale, conv_w_in, conv_b_in, conv_dw, conv_dw_b, conv_ln_g, conv_ln_b, conv_w_out, conv_b_out, mix_ln_g, mix_ln_b, mlp_w1, mlp_b1, mlp_w2, mlp_b2, mlp_ln_g, mlp_ln_b, loss_target, m_pool_w, m_pool_scale, m_conv_w_in, m_conv_b_in, m_conv_dw, m_conv_dw_b, m_conv_ln_g, m_conv_ln_b, m_conv_w_out, m_conv_b_out, m_mix_ln_g, m_mix_ln_b, m_mlp_w1, m_mlp_b1, m_mlp_w2, m_mlp_b2, m_mlp_ln_g, m_mlp_ln_b, v_pool_w, v_pool_scale, v_conv_w_in, v_conv_b_in, v_conv_dw, v_conv_dw_b, v_conv_ln_g, v_conv_ln_b, v_conv_w_out, v_conv_b_out, v_mix_ln_g, v_mix_ln_b, v_mlp_w1, v_mlp_b1, v_mlp_w2, v_mlp_b2, v_mlp_ln_g, v_mlp_ln_b):
    _, T, D = x.shape
    L = mlp_w1.shape[0]
    assert L == 2 and pool_w.shape[0] == 1 and conv_w_in.shape[0] == 1
    G = pool_w.shape[1]
    Dg = D // G
    Fd = mlp_b1.shape[1]
    Fs = Fd // N_DEV
    Kc = conv_dw.shape[1]
    Dc = D // N_DEV
    alpha = float((2.0 * L) ** 0.25)
    x2d, tgt = x[0], loss_target[0]

    tm = _tile(T, 512)
    tm_wide = _tile(T, 512)
    tm_conv = _tile(T, 256)
    tm_wg = _tile(T, 1024)

    def pack_sh(dw, dwb, lg, lb, bo, bi):
        rows = jnp.concatenate([dw[0], dwb, lg, lb, bo, bi.reshape(2, Dc)], axis=0)
        return _pad_rows(rows, SH_ROWS)

    SH_ROWS = -(-(Kc + 6) // 8) * 8
    def pack_rep(ps, mg, mb, b1, b2, lg, lb):
        rows = jnp.concatenate([ps, mg, mb, b1.reshape(L * Fd // D, D), b2, lg, lb], axis=0)
        return _pad_rows(rows, REP_ROWS)

    n_rep = 1 + 2 * L + L * Fd // D + 3 * L
    REP_ROWS = -(-n_rep // 8) * 8

    w_sh = pack_sh(conv_dw, conv_dw_b, conv_ln_g, conv_ln_b, conv_b_out, conv_b_in)
    m_sh = pack_sh(m_conv_dw, m_conv_dw_b, m_conv_ln_g, m_conv_ln_b, m_conv_b_out, m_conv_b_in)
    v_sh = pack_sh(v_conv_dw, v_conv_dw_b, v_conv_ln_g, v_conv_ln_b, v_conv_b_out, v_conv_b_in)
    w_rep = pack_rep(pool_scale, mix_ln_g, mix_ln_b, mlp_b1, mlp_b2, mlp_ln_g, mlp_ln_b)
    m_rep = pack_rep(m_pool_scale, m_mix_ln_g, m_mix_ln_b, m_mlp_b1, m_mlp_b2, m_mlp_ln_g, m_mlp_ln_b)
    v_rep = pack_rep(v_pool_scale, v_mix_ln_g, v_mix_ln_b, v_mlp_b1, v_mlp_b2, v_mlp_ln_g, v_mlp_ln_b)

    groups = [[pool_w[0]], [mlp_w1[0]], [mlp_w2[0]], [conv_w_in[0], conv_w_out[0]], [mlp_w1[1]], [mlp_w2[1]]]
    handles, tokens = [], []
    for i, grp in enumerate(groups):
        tie = tokens[-1][0, 0] if tokens else 0.0
        srcs = [(a + tie).astype(BF16) for a in grp] + ([w_sh] if i == 0 else [])
        h, tk = _xstart("gather_%d" % i, srcs, [False] * len(srcs))
        handles.append(h)
        tokens.append(tk)
    pw_all, sh_all = _xwait(handles[0], tokens[-1])
    pw = pw_all.transpose(1, 0, 2, 3).reshape(G, Dg, Dg)
    dw_full = sh_all[:, 0:Kc].transpose(1, 0, 2).reshape(Kc, D)

    def sh_row(i):
        return sh_all[:, i].reshape(1, D)

    dwb_full, cg_full, cb_full, bout_full = (sh_row(Kc + i) for i in range(4))
    bin_full = sh_all[:, Kc + 4:Kc + 6].reshape(1, 2 * D)

    h0, z_m0 = _pool_fwd(x2d, pw, pool_scale, mix_ln_g[0:1], mix_ln_b[0:1], alpha, tm, tuple(tokens))
    (w1b0,) = _xwait(handles[1], h0)
    u0 = _mlp_up("mlp_up0", h0, w1b0, mlp_b1[0:1], tm)
    (w2b0,) = _xwait(handles[2], u0)
    x1, z_f0 = _proj_ln("mlp_down0", u0, w2b0.reshape(Fd, D), mlp_b2[0:1], h0, mlp_ln_g[0:1], mlp_ln_b[0:1],
                        alpha, tm, act=True)
    win_b, wout_all = _xwait(handles[3], x1)
    w_out = wout_all.reshape(D, D)
    p, glu = _conv_in(x1, win_b, bin_full, tm)
    cz, s = _dwconv_fwd(glu, dw_full, dwb_full, cg_full, cb_full, tm_conv)
    h1, z_m1 = _proj_ln("conv_out", s, w_out, bout_full, x1, mix_ln_g[1:2], mix_ln_b[1:2], alpha, tm)
    (w1b1,) = _xwait(handles[4], h1)
    u1 = _mlp_up("mlp_up1", h1, w1b1, mlp_b1[1:2], tm)
    (w2b1,) = _xwait(handles[5], u1)

    dz, dzb, g_fg1, g_fb1, g_b2_1, loss_cols = _proj_ln_loss(
        "mlp_down1", u1, w2b1.reshape(Fd, D), mlp_b2[1:2], h1, mlp_ln_g[1:2], mlp_ln_b[1:2], tgt, alpha, tm)
    gw2_1 = _wgrad("gw2_1", u1, dzb, False, 4, 1, _tile(T, 2048), act=True)
    e_w2_1, tk = _xstart("grads_w2_1", [gw2_1], [True])
    du, g_b1_1 = _mlp_act_bwd("mlp_act_bwd1", dzb, u1, w2b1, tm, (tk,))
    dz, dzb, g_mg1, g_mb1, g_bout = _dx_proj("mlp_in_bwd1", dz, du, w1b1, z_m1, mix_ln_g[1:2], alpha, tm_wide)
    gw1_1 = _wgrad("gw1_1", h1, du, True, 1, 2, tm_wg)
    e_w1_1, tk = _xstart("grads_w1_1", [gw1_1], [True])
    gwout = _wgrad("gw_out", s, dzb, False, 1, 1, tm_wg)
    dc, g_cg, g_cb, g_dwb = _conv_out_bwd(dzb, w_out, cz, cg_full, cb_full, tm, (tk,))
    dp, g_dw, g_bin = _dwconv_bwd(dc, glu, p, dw_full, tm_conv)
    gwin = _wgrad("gw_in", x1, dp, True, 1, 2, tm_wg)
    e_conv, tk = _xstart("grads_conv", [gwin, gwout], [True, True])
    dz, dzb, g_fg0, g_fb0, g_b2_0 = _dx_proj("conv_in_bwd", dz, dp, win_b, z_f0, mlp_ln_g[0:1], alpha, tm, (tk,))
    gw2_0 = _wgrad("gw2_0", u0, dzb, False, 4, 1, _tile(T, 2048), act=True)
    e_w2_0, tk = _xstart("grads_w2_0", [gw2_0], [True])
    du, g_b1_0 = _mlp_act_bwd("mlp_act_bwd0", dzb, u0, w2b0, tm, (tk,))
    dz, dzb, g_mg0, g_mb0, _ = _dx_proj("mlp_in_bwd0", dz, du, w1b0, z_m0, mix_ln_g[0:1], alpha, tm_wide)
    gw1_0 = _wgrad("gw1_0", h0, du, True, 1, 2, tm_wg)
    e_w1_0, tk = _xstart("grads_w1_0", [gw1_0], [True])
    grad_x, g_pw, g_ps = _pool_bwd(dz, x2d, pw, pool_scale, alpha, tm, (tk,))

    loss = lax.psum(0.5 / D * jnp.sum(loss_cols), MESH_AXES)

    gpw_b = g_pw.reshape(G, N_DEV, Dg // N_DEV, Dg).transpose(1, 0, 2, 3).astype(BF16)

    def to_dev(vec, rows):
        return vec.reshape(rows, N_DEV, Dc).transpose(1, 0, 2)

    g_sh = jnp.concatenate(
        [to_dev(g_dw[0:Kc], Kc), to_dev(g_dwb, 1), to_dev(g_cg, 1), to_dev(g_cb, 1), to_dev(g_bout, 1),
         g_bin.reshape(N_DEV, 2, Dc), jnp.zeros((N_DEV, SH_ROWS - Kc - 6, Dc), F32)], axis=1)
    g_rep = _pad_rows(jnp.concatenate(
        [g_ps, g_mg0, g_mg1, g_mb0, g_mb1, g_b1_0.reshape(Fd // D, D), g_b1_1.reshape(Fd // D, D),
         g_b2_0, g_b2_1, g_fg0, g_fg1, g_fb0, g_fb1], axis=0), REP_ROWS)

    e_small, tk = _xstart("grads_small", [gpw_b, g_sh, g_rep], [True, True, False])

    def upd(name, recv, w, m, v):
        shape = w.shape
        C = shape[-1]
        R = w.size // C
        outs = _adamw(name, recv.reshape(N_DEV, R, C), w.reshape(R, C), m.reshape(R, C), v.reshape(R, C), _tile(R, 256))
        return [o.reshape(shape) for o in outs]

    def upd_layer(name, recv, w, m, v, layer, prev):
        return _adamw(name, recv, w, m, v, _tile(w.shape[1], 256), layer, prev)

    (r_w2_1,) = _xwait(e_w2_1, tk)
    o_w2 = upd_layer("adam_w2_1", r_w2_1, mlp_w2, m_mlp_w2, v_mlp_w2, 1, None)
    (r_w1_1,) = _xwait(e_w1_1, o_w2[0])
    o_w1 = upd_layer("adam_w1_1", r_w1_1, mlp_w1, m_mlp_w1, v_mlp_w1, 1, None)
    r_win, r_wout = _xwait(e_conv, o_w1[0])
    o_win = upd("adam_w_in", r_win, conv_w_in, m_conv_w_in, v_conv_w_in)
    o_wout = upd("adam_w_out", r_wout, conv_w_out, m_conv_w_out, v_conv_w_out)
    (r_w2_0,) = _xwait(e_w2_0, o_wout[0])
    o_w2 = upd_layer("adam_w2_0", r_w2_0, mlp_w2, m_mlp_w2, v_mlp_w2, 0, o_w2)
    (r_w1_0,) = _xwait(e_w1_0, o_w2[0])
    o_w1 = upd_layer("adam_w1_0", r_w1_0, mlp_w1, m_mlp_w1, v_mlp_w1, 0, o_w1)
    r_pw, r_sh, r_rep = _xwait(e_small, o_w1[0])
    o_pw = upd("adam_pool_w", r_pw, pool_w, m_pool_w, v_pool_w)
    sh_pieces = [(0, Kc, (1, Kc, Dc))] + [(Kc + i, 1, (1, Dc)) for i in range(4)] + [(Kc + 4, 2, (1, 2 * Dc))]
    rep_pieces, o = [], 0
    for rows, shape in ((1, (1, D)), (L, (L, D)), (L, (L, D)), (L * Fd // D, (L, Fd)), (L, (L, D)), (L, (L, D)),
                        (L, (L, D))):
        rep_pieces.append((o, rows, shape))
        o += rows
    o_sh = _adamw_small("adam_conv_vec", r_sh, w_sh, m_sh, v_sh, sh_pieces)
    o_rep = _adamw_small("adam_replicated", r_rep, w_rep, m_rep, v_rep, rep_pieces)

    results = []
    for kind in range(4):
        dwv, dwb, lg, lb, bo, bi = o_sh[kind]
        ps, mg, mb, b1, b2, fg, fb = o_rep[kind]
        results.append([o_pw[kind], ps, o_win[kind], bi, dwv, dwb, lg, lb, o_wout[kind], bo, mg, mb,
                        o_w1[kind], b1, o_w2[kind], b2, fg, fb])
    return (loss, grad_x[None], *results[0], *results[1], *results[2], *results[3])
```

```python
import jax
import jax.numpy as jnp
from jax import lax
from jax.experimental import pallas as pl
from jax.experimental.pallas import tpu as pltpu

N_DEV = 8
MESH_AXES = ("x", "y", "c")
POOL_WINDOWS = (2, 4, 8, 16)
POOL_HALO = 16
CONV_HALO = 32
LN_EPS = 1e-5
ADAM_LR = 0.001
ADAM_B1 = 0.9
ADAM_B2 = 0.999
ADAM_EPS = 1e-08
ADAM_WD = 0.01
ADAM_STEP = 10
VMEM_LIMIT = 56 * 1024 * 1024

F32 = jnp.float32
BF16 = jnp.bfloat16


def _mm(a, b):
    return lax.dot_general(a, b, (((1,), (0,)), ((), ())), preferred_element_type=F32)


def _mm_nt(a, b):
    return lax.dot_general(a, b, (((1,), (1,)), ((), ())), preferred_element_type=F32)


def _mm_tn(a, b):
    return lax.dot_general(a, b, (((0,), (0,)), ((), ())), preferred_element_type=F32)


def _tile(n, pref):
    t = min(n, pref)
    assert n % t == 0, (n, pref)
    return t


def _params(*sem):
    return pltpu.CompilerParams(dimension_semantics=sem, vmem_limit_bytes=VMEM_LIMIT)


def _full(shape):
    nd = len(shape)
    return pl.BlockSpec(shape, lambda *_: (0,) * nd)


def _rows(tm, d):
    return pl.BlockSpec((tm, d), lambda i: (i, 0))


def _ln_stats(z):
    mu = jnp.mean(z, axis=-1, keepdims=True)
    zc = z - mu
    var = jnp.mean(zc * zc, axis=-1, keepdims=True)
    rstd = lax.rsqrt(var + LN_EPS)
    return zc * rstd, rstd


def _ln_fwd(z, g, b):
    xhat, _ = _ln_stats(z)
    return xhat * g + b


def _ln_bwd(dy, z, g):
    xhat, rstd = _ln_stats(z)
    dxh = dy * g
    m1 = jnp.mean(dxh, axis=-1, keepdims=True)
    m2 = jnp.mean(dxh * xhat, axis=-1, keepdims=True)
    return rstd * (dxh - m1 - xhat * m2), xhat


def _colsum(v):
    return jnp.sum(v, axis=0, keepdims=True)


def _sigmoid(v):
    return 1.0 / (1.0 + jnp.exp(-v))


def _acc(ref, first, val):
    @pl.when(first)
    def _():
        ref[...] = val

    @pl.when(jnp.logical_not(first))
    def _():
        ref[...] += val


_HBM = pl.BlockSpec(memory_space=pltpu.HBM)
_SEM = pl.BlockSpec(memory_space=pltpu.SEMAPHORE)
_EFFECT = pltpu.SideEffectType.DATAFLOW_SIDE_EFFECTING


def _peers():
    x, y, c = (lax.axis_index(a) for a in MESH_AXES)
    out = []
    for d in range(1, N_DEV):
        px = (x + ((d >> 2) & 1)) % 2
        py = (y + ((d >> 1) & 1)) % 2
        pc = (c + (d & 1)) % 2
        out.append((d - 1, (px, py, pc), 4 * px + 2 * py + pc))
    return 4 * x + 2 * y + c, out


def _remote_copies(src_refs, land_refs, scatter, send_sems, recv_sems):
    me, peers = _peers()
    copies = []
    for i, pos, pid in peers:
        for k, (src, land) in enumerate(zip(src_refs, land_refs)):
            copies.append(pltpu.make_async_remote_copy(
                src_ref=src.at[pid] if scatter[k] else src, dst_ref=land.at[me],
                send_sem=send_sems.at[k * (N_DEV - 1) + i], recv_sem=recv_sems.at[k * (N_DEV - 1) + i],
                device_id=pos, device_id_type=pl.DeviceIdType.MESH))
    return copies


def _xstart(name, srcs, scatter):
    n = len(srcs)
    me = 4 * lax.axis_index("x") + 2 * lax.axis_index("y") + lax.axis_index("c")
    lands = []
    for s, sc in zip(srcs, scatter):
        own = lax.dynamic_index_in_dim(s, me, 0, keepdims=True) if sc else s[None]
        shape = s.shape if sc else (N_DEV,) + s.shape
        lands.append(lax.dynamic_update_slice(lax.empty(shape, s.dtype), own, (me,) + (0,) * (len(shape) - 1)))

    def body(*refs):
        src_refs, land_refs = refs[:n], refs[n:2 * n]
        send_sems, recv_sems = refs[2 * n], refs[2 * n + 1]
        token = refs[-1]
        for cp in _remote_copies(src_refs, land_refs, scatter, send_sems, recv_sems):
            cp.start()
        token[...] = jnp.zeros_like(token)

    outs = pl.pallas_call(
        body, name=name,
        out_shape=(pltpu.SemaphoreType.DMA((n * (N_DEV - 1),)), pltpu.SemaphoreType.DMA((n * (N_DEV - 1),)),
                   *[pltpu.HBM(a.shape, a.dtype) for a in srcs + lands], jax.ShapeDtypeStruct((8, 128), F32)),
        in_specs=(_HBM,) * (2 * n),
        out_specs=(_SEM, _SEM) + (_HBM,) * (2 * n) + (pl.BlockSpec(memory_space=pltpu.VMEM),),
        input_output_aliases={i: 2 + i for i in range(2 * n)},
        compiler_params=pltpu.CompilerParams(has_side_effects=_EFFECT),
    )(*[pltpu.with_memory_space_constraint(a, pltpu.HBM) for a in srcs + lands])
    return (name, scatter, outs[0], outs[1], outs[2:2 + n], outs[2 + n:2 + 2 * n]), outs[-1]


def _xwait(handle, after):
    name, scatter, send_sems, recv_sems, srcs, lands = handle
    n = len(srcs)

    def body(*refs):
        src_refs, land_refs = refs[:n], refs[n:2 * n]
        send, recv = refs[2 * n], refs[2 * n + 1]
        copies = _remote_copies(src_refs, land_refs, scatter, send, recv)
        for cp in copies:
            cp.wait_send()
        for cp in copies:
            cp.wait_recv()

    outs = pl.pallas_call(
        body, name=name + "_wait",
        out_shape=tuple(pltpu.HBM(a.shape, a.dtype) for a in (*srcs, *lands)),
        in_specs=(_HBM,) * (2 * n) + (_SEM, _SEM, pl.BlockSpec(memory_space=pl.ANY)),
        out_specs=(_HBM,) * (2 * n),
        input_output_aliases={i: i for i in range(2 * n)},
        compiler_params=pltpu.CompilerParams(has_side_effects=_EFFECT),
    )(*srcs, *lands, send_sems, recv_sems, after)
    return outs[n:]


def _tied_call(body, deps, in_specs, **kw):
    nd = len(deps)

    def tied_body(*refs):
        body(*refs[nd:])

    call = pl.pallas_call(tied_body, in_specs=[pl.BlockSpec(memory_space=pl.ANY)] * nd + list(in_specs), **kw)
    return lambda *args: call(*deps, *args)


def _pool_fwd(x, pw, scale, g, b, alpha, tm, deps=()):
    T, D = x.shape
    G = len(POOL_WINDOWS)
    Dg = D // G
    H = POOL_HALO
    r = tm // H

    def body(x_ref, xh_ref, pw_ref, sc_ref, g_ref, b_ref, y_ref, z_ref, yb_ref, ext_ref):
        i = pl.program_id(0)
        ext_ref[0:H, :] = jnp.where(i > 0, xh_ref[...], 0.0)
        ext_ref[H:, :] = x_ref[...]
        row = i * tm + lax.broadcasted_iota(jnp.int32, (tm, 1), 0)
        rowf = (row + 1).astype(F32)
        for gi, w in enumerate(POOL_WINDOWS):
            sl = slice(gi * Dg, (gi + 1) * Dg)
            s = ext_ref[:, sl]
            k = 1
            while k < w:
                s = s + pltpu.roll(s, k, 0)
                k *= 2
            inv = 1.0 / jnp.minimum(rowf, float(w))
            xg = x_ref[:, sl]
            d = s[H:, :] * inv - xg
            mix = _mm(d.astype(BF16), pw_ref[gi]) * sc_ref[:, sl]
            z_ref[:, sl] = alpha * xg + mix
        y = _ln_fwd(z_ref[...], g_ref[...], b_ref[...])
        y_ref[...] = y
        yb_ref[...] = y.astype(BF16)

    return _tied_call(
        body, deps, name="pool_fwd", grid=(T // tm,),
        out_shape=[jax.ShapeDtypeStruct((T, D), F32)] * 2 + [jax.ShapeDtypeStruct((T, D), BF16)],
        in_specs=[_rows(tm, D), pl.BlockSpec((H, D), lambda i: (jnp.maximum(i * r - 1, 0), 0)),
                  _full(pw.shape), _full((1, D)), _full((1, D)), _full((1, D))],
        out_specs=[_rows(tm, D)] * 3,
        scratch_shapes=[pltpu.VMEM((tm + H, D), F32)],
        compiler_params=_params("parallel"),
    )(x, x, pw, scale, g, b)


def _mlp_up(name, h, w1b, b1, tm):
    T, D = h.shape
    nb, _, Fs = w1b.shape
    Fd = nb * Fs

    def body(h_ref, w_ref, b_ref, u_ref):
        hb = h_ref[...]
        for j in range(nb):
            sl = slice(j * Fs, (j + 1) * Fs)
            u_ref[:, sl] = (_mm(hb, w_ref[j]) + b_ref[:, sl]).astype(BF16)

    return pl.pallas_call(
        body, name=name, grid=(T // tm,),
        out_shape=jax.ShapeDtypeStruct((T, Fd), BF16),
        in_specs=[_rows(tm, D), _full(w1b.shape), _full((1, Fd))],
        out_specs=_rows(tm, Fd),
        compiler_params=_params("parallel"),
    )(h, w1b, b1)


def _sq_relu(u):
    r = jnp.maximum(u, 0.0)
    return r * r


def _proj_ln(name, a, w, bias, res, g, b, alpha, tm, act=False):
    T, K = a.shape
    D = w.shape[1]

    def body(a_ref, w_ref, bias_ref, res_ref, g_ref, b_ref, y_ref, z_ref, yb_ref):
        av = _sq_relu(a_ref[...]) if act else a_ref[...]
        z = alpha * res_ref[...] + _mm(av, w_ref[...]) + bias_ref[...]
        z_ref[...] = z
        y = _ln_fwd(z, g_ref[...], b_ref[...])
        y_ref[...] = y
        yb_ref[...] = y.astype(BF16)

    return pl.pallas_call(
        body, name=name, grid=(T // tm,),
        out_shape=[jax.ShapeDtypeStruct((T, D), F32)] * 2 + [jax.ShapeDtypeStruct((T, D), BF16)],
        in_specs=[_rows(tm, K), _full((K, D)), _full((1, D)), _rows(tm, D), _full((1, D)), _full((1, D))],
        out_specs=[_rows(tm, D)] * 3,
        compiler_params=_params("parallel"),
    )(a, w, bias, res, g, b)


def _proj_ln_loss(name, a, w, bias, res, g, b, target, alpha, tm):
    T, K = a.shape
    D = w.shape[1]

    def body(a_ref, w_ref, bias_ref, res_ref, g_ref, b_ref, t_ref, dz_ref, dzb_ref, gg_ref, gb_ref, sdz_ref, loss_ref):
        first = pl.program_id(0) == 0
        z = alpha * res_ref[...] + _mm(_sq_relu(a_ref[...]), w_ref[...]) + bias_ref[...]
        gv = g_ref[...]
        e = _ln_fwd(z, gv, b_ref[...]) - t_ref[...]
        dy = e * (1.0 / D)
        dz, xhat = _ln_bwd(dy, z, gv)
        dz_ref[...] = dz
        dzb_ref[...] = dz.astype(BF16)
        _acc(gg_ref, first, _colsum(dy * xhat))
        _acc(gb_ref, first, _colsum(dy))
        _acc(sdz_ref, first, _colsum(dz))
        _acc(loss_ref, first, _colsum(e * e))

    return pl.pallas_call(
        body, name=name, grid=(T // tm,),
        out_shape=[jax.ShapeDtypeStruct((T, D), F32), jax.ShapeDtypeStruct((T, D), BF16)]
        + [jax.ShapeDtypeStruct((1, D), F32)] * 4,
        in_specs=[_rows(tm, K), _full((K, D)), _full((1, D)), _rows(tm, D), _full((1, D)), _full((1, D)), _rows(tm, D)],
        out_specs=[_rows(tm, D)] * 2 + [_full((1, D))] * 4,
        compiler_params=_params("arbitrary"),
    )(a, w, bias, res, g, b, target)


def _conv_in(x, wb, b_in, tm):
    T, D = x.shape
    nb, _, Ns = wb.shape
    half = nb // 2

    def body(x_ref, w_ref, b_ref, p_ref, glu_ref):
        xb = x_ref[...]
        for j in range(half):
            sa = slice(j * Ns, (j + 1) * Ns)
            sg = slice(D + j * Ns, D + (j + 1) * Ns)
            a = _mm(xb, w_ref[j]) + b_ref[:, sa]
            gate = _mm(xb, w_ref[half + j]) + b_ref[:, sg]
            p_ref[:, sa] = a.astype(BF16)
            p_ref[:, sg] = gate.astype(BF16)
            glu_ref[:, sa] = a * _sigmoid(gate)

    return pl.pallas_call(
        body, name="conv_in", grid=(T // tm,),
        out_shape=[jax.ShapeDtypeStruct((T, 2 * D), BF16), jax.ShapeDtypeStruct((T, D), F32)],
        in_specs=[_rows(tm, D), _full(wb.shape), _full((1, 2 * D))],
        out_specs=[_rows(tm, 2 * D), _rows(tm, D)],
        compiler_params=_params("parallel"),
    )(x, wb, b_in)


def _lane_chunk(d):
    return 128 if d % 128 == 0 else d


SUBLANES = 8


def _row_chunk(tm, pref):
    return pref if tm % pref == 0 else tm


def _slabs(ref, base, n_taps, r0, rows, cs):
    out = []
    for r in range(min(SUBLANES, n_taps)):
        nq = (n_taps - 1 - r) // SUBLANES + 1
        lo = base + r + r0
        slab = ref[lo:lo + rows + SUBLANES * (nq - 1), cs]
        out.append((slab, [(q, SUBLANES * q + r) for q in range(nq)]))
    return out


def _dwconv_fwd(glu, dw, dw_b, g, b, tm):
    T, D = glu.shape
    K = dw.shape[0]
    H = CONV_HALO
    off = H - (K - 1)
    r = tm // H
    cc = _lane_chunk(D)

    def body(x_ref, xh_ref, dw_ref, dwb_ref, g_ref, b_ref, cz_ref, s_ref, ext_ref, sh_ref):
        i = pl.program_id(0)
        ext_ref[0:H, :] = jnp.where(i > 0, xh_ref[...], 0.0)
        ext_ref[H:, :] = x_ref[...]
        for ph in range(min(SUBLANES, K)):
            n = tm + SUBLANES * ((K - 1 - ph) // SUBLANES)
            sh_ref[ph, 0:n, :] = ext_ref[off + ph:off + ph + n, :]
        rc = _row_chunk(tm, 128)
        for c0 in range(0, D, cc):
            cs = slice(c0, c0 + cc)
            for r0 in range(0, tm, rc):
                acc = jnp.zeros((rc, cc), F32) + dwb_ref[:, cs]
                for k in range(K):
                    q, ph = divmod(k, SUBLANES)
                    acc = acc + sh_ref[ph, SUBLANES * q + r0:SUBLANES * q + r0 + rc, cs] * dw_ref[k:k + 1, cs]
                cz_ref[r0:r0 + rc, cs] = acc
        ln = _ln_fwd(cz_ref[...], g_ref[...], b_ref[...])
        s_ref[...] = (ln * _sigmoid(ln)).astype(BF16)

    return pl.pallas_call(
        body, name="dwconv_fwd", grid=(T // tm,),
        out_shape=[jax.ShapeDtypeStruct((T, D), F32), jax.ShapeDtypeStruct((T, D), BF16)],
        in_specs=[_rows(tm, D), pl.BlockSpec((H, D), lambda i: (jnp.maximum(i * r - 1, 0), 0)),
                  _full((K, D)), _full((1, D)), _full((1, D)), _full((1, D))],
        out_specs=[_rows(tm, D)] * 2,
        scratch_shapes=[pltpu.VMEM((tm + H, D), F32),
                        pltpu.VMEM((SUBLANES, tm + SUBLANES * ((K - 1) // SUBLANES), D), F32)],
        compiler_params=_params("parallel"),
    )(glu, glu, dw, dw_b, g, b)


def _ln_bwd_store(dy, z_ref, g_ref, first, dz_ref, dzb_ref, gg_ref, gb_ref, sdz_ref):
    dz, xhat = _ln_bwd(dy, z_ref[...], g_ref[...])
    dz_ref[...] = dz
    dzb_ref[...] = dz.astype(BF16)
    _acc(gg_ref, first, _colsum(dy * xhat))
    _acc(gb_ref, first, _colsum(dy))
    _acc(sdz_ref, first, _colsum(dz))


def _ln_bwd_outs(T, D):
    shapes = [jax.ShapeDtypeStruct((T, D), F32), jax.ShapeDtypeStruct((T, D), BF16)] + [jax.ShapeDtypeStruct((1, D), F32)] * 3
    return shapes


def _mlp_act_bwd(name, dzb, u, w2b, tm, deps=()):
    T, D = dzb.shape
    nb, Fs, _ = w2b.shape
    Fd = nb * Fs

    def body(dzb_ref, u_ref, w2_ref, du_ref, gb1_ref):
        dzv = dzb_ref[...]
        sums = []
        for j in range(nb):
            sl = slice(j * Fs, (j + 1) * Fs)
            du = _mm_nt(dzv, w2_ref[j]) * (2.0 * jnp.maximum(u_ref[:, sl].astype(F32), 0.0))
            du_ref[:, sl] = du.astype(BF16)
            sums.append(_colsum(du))
        _acc(gb1_ref, pl.program_id(0) == 0, jnp.concatenate(sums, axis=1))

    return _tied_call(
        body, deps, name=name, grid=(T // tm,),
        out_shape=[jax.ShapeDtypeStruct((T, Fd), BF16), jax.ShapeDtypeStruct((1, Fd), F32)],
        in_specs=[_rows(tm, D), _rows(tm, Fd), _full(w2b.shape)],
        out_specs=[_rows(tm, Fd), _full((1, Fd))],
        compiler_params=_params("arbitrary"),
    )(dzb, u, w2b)


def _wgrad(name, xm, dy, col_blocks, nk, nj, tm, act=False):
    T, K = xm.shape
    N = dy.shape[1]
    Kb, Nb = K // nk, N // nj
    nt = T // tm
    if col_blocks:
        per, Ns = N_DEV // nj, N // N_DEV
        out_shape = (N_DEV, K, Ns)
        out_spec = pl.BlockSpec((per, Kb, Ns), lambda k, j, t: (j, k, 0))
    else:
        per, Ks = N_DEV // nk, K // N_DEV
        out_shape = (N_DEV, Ks, N)
        out_spec = pl.BlockSpec((per, Ks, Nb), lambda k, j, t: (k, 0, j))

    def body(x_ref, dy_ref, o_ref, acc_ref):
        t = pl.program_id(2)
        xb = _sq_relu(x_ref[...]) if act else x_ref[...].astype(BF16)
        _acc(acc_ref, t == 0, _mm_tn(xb, dy_ref[...]))

        @pl.when(t == nt - 1)
        def _():
            for q in range(per):
                if col_blocks:
                    o_ref[q] = acc_ref[:, q * Ns:(q + 1) * Ns].astype(BF16)
                else:
                    o_ref[q] = acc_ref[q * Ks:(q + 1) * Ks, :].astype(BF16)

    return pl.pallas_call(
        body, name=name, grid=(nk, nj, nt),
        out_shape=jax.ShapeDtypeStruct(out_shape, BF16),
        in_specs=[pl.BlockSpec((tm, Kb), lambda k, j, t: (t, k)), pl.BlockSpec((tm, Nb), lambda k, j, t: (t, j))],
        out_specs=out_spec,
        scratch_shapes=[pltpu.VMEM((Kb, Nb), F32)],
        compiler_params=_params("parallel", "parallel", "arbitrary"),
    )(xm, dy)


def _conv_out_bwd(dzb, w_out, cz, g, b, tm, deps=()):
    T, D = cz.shape

    def body(dz_ref, w_ref, cz_ref, g_ref, b_ref, dc_ref, gg_ref, gb_ref, sdc_ref):
        first = pl.program_id(0) == 0
        ds = _mm_nt(dz_ref[...], w_ref[...])
        czv = cz_ref[...]
        gv = g_ref[...]
        ln = _ln_fwd(czv, gv, b_ref[...])
        sg = _sigmoid(ln)
        dln = ds * (sg * (1.0 + ln * (1.0 - sg)))
        dc, xhat = _ln_bwd(dln, czv, gv)
        dc_ref[...] = dc
        _acc(gg_ref, first, _colsum(dln * xhat))
        _acc(gb_ref, first, _colsum(dln))
        _acc(sdc_ref, first, _colsum(dc))

    return _tied_call(
        body, deps, name="conv_out_bwd", grid=(T // tm,),
        out_shape=[jax.ShapeDtypeStruct((T, D), F32)] + [jax.ShapeDtypeStruct((1, D), F32)] * 3,
        in_specs=[_rows(tm, D), _full((D, D)), _rows(tm, D), _full((1, D)), _full((1, D))],
        out_specs=[_rows(tm, D)] + [_full((1, D))] * 3,
        compiler_params=_params("arbitrary"),
    )(dzb, w_out, cz, g, b)


def _dwconv_bwd(dc, glu, p, dw, tm):
    T, D = dc.shape
    K = dw.shape[0]
    Kp = -(-K // 8) * 8
    H = CONV_HALO
    off = H - (K - 1)
    r = tm // H
    last = T // H - 1
    nt = T // tm
    cc = _lane_chunk(D)

    def body(dc_ref, dch_ref, x_ref, xh_ref, p_ref, dw_ref, dp_ref, gdw_ref, gbin_ref, edc_ref, ex_ref, dglu_ref,
             gacc_ref, shd_ref, shx_ref):
        i = pl.program_id(0)
        first = i == 0
        edc_ref[0:tm, :] = dc_ref[...]
        edc_ref[tm:, :] = jnp.where(i < nt - 1, dch_ref[...], 0.0)
        ex_ref[0:H, :] = jnp.where(i > 0, xh_ref[...], 0.0)
        ex_ref[H:, :] = x_ref[...]

        @pl.when(first)
        def _():
            gacc_ref[...] = jnp.zeros_like(gacc_ref)

        for ph in range(min(SUBLANES, K)):
            n = tm + SUBLANES * ((K - 1 - ph) // SUBLANES)
            shd_ref[ph, 0:n, :] = edc_ref[ph:ph + n, :]
            shx_ref[ph, 0:n, :] = ex_ref[off + ph:off + ph + n, :]
        rc = _row_chunk(tm, 64)
        for c0 in range(0, D, cc):
            cs = slice(c0, c0 + cc)
            for r0 in range(0, tm, rc):
                dcv = dc_ref[r0:r0 + rc, cs]
                acc = jnp.zeros((rc, cc), F32)
                for m in range(K):
                    q, ph = divmod(m, SUBLANES)
                    lo = SUBLANES * q + r0
                    acc = acc + shd_ref[ph, lo:lo + rc, cs] * dw_ref[K - 1 - m:K - m, cs]
                dglu_ref[r0:r0 + rc, cs] = acc
                for k in range(K):
                    q, ph = divmod(k, SUBLANES)
                    lo = SUBLANES * q + r0
                    part = (dcv * shx_ref[ph, lo:lo + rc, cs]).reshape(rc // SUBLANES, SUBLANES, cc)
                    gacc_ref[k, :, cs] += jnp.sum(part, axis=0)

        @pl.when(i == nt - 1)
        def _():
            gdw_ref[...] = jnp.zeros_like(gdw_ref)
            gdw_ref[0:K, :] = jnp.sum(gacc_ref[...], axis=1)
        dglu = dglu_ref[...]
        a = p_ref[:, 0:D].astype(F32)
        sg = _sigmoid(p_ref[:, D:2 * D].astype(F32))
        da = dglu * sg
        dgate = dglu * a * (sg * (1.0 - sg))
        dp_ref[:, 0:D] = da.astype(BF16)
        dp_ref[:, D:2 * D] = dgate.astype(BF16)
        _acc(gbin_ref.at[:, 0:D], first, _colsum(da))
        _acc(gbin_ref.at[:, D:2 * D], first, _colsum(dgate))

    return pl.pallas_call(
        body, name="dwconv_bwd", grid=(nt,),
        out_shape=[jax.ShapeDtypeStruct((T, 2 * D), BF16), jax.ShapeDtypeStruct((Kp, D), F32),
                   jax.ShapeDtypeStruct((1, 2 * D), F32)],
        in_specs=[_rows(tm, D), pl.BlockSpec((H, D), lambda i: (jnp.minimum((i + 1) * r, last), 0)),
                  _rows(tm, D), pl.BlockSpec((H, D), lambda i: (jnp.maximum(i * r - 1, 0), 0)),
                  _rows(tm, 2 * D), _full((K, D))],
        out_specs=[_rows(tm, 2 * D), _full((Kp, D)), _full((1, 2 * D))],
        scratch_shapes=[pltpu.VMEM((tm + H, D), F32), pltpu.VMEM((tm + H, D), F32), pltpu.VMEM((tm, D), F32),
                        pltpu.VMEM((K, SUBLANES, D), F32),
                        pltpu.VMEM((SUBLANES, tm + SUBLANES * ((K - 1) // SUBLANES), D), F32),
                        pltpu.VMEM((SUBLANES, tm + SUBLANES * ((K - 1) // SUBLANES), D), F32)],
        compiler_params=_params("arbitrary"),
    )(dc, dc, glu, glu, p, dw)


def _dx_proj(name, dz, dy, wb, z_in, g_in, alpha, tm, deps=()):
    T, D = dz.shape
    nb, _, Ns = wb.shape
    N = nb * Ns

    def body(dz_ref, dy_ref, w_ref, zin_ref, gin_ref, dzo_ref, dzob_ref, gg_ref, gb_ref, sdz_ref):
        dx = alpha * dz_ref[...]
        for j in range(nb):
            dx = dx + _mm_nt(dy_ref[:, j * Ns:(j + 1) * Ns], w_ref[j])
        _ln_bwd_store(dx, zin_ref, gin_ref, pl.program_id(0) == 0, dzo_ref, dzob_ref, gg_ref, gb_ref, sdz_ref)

    return _tied_call(
        body, deps, name=name, grid=(T // tm,),
        out_shape=_ln_bwd_outs(T, D),
        in_specs=[_rows(tm, D), _rows(tm, N), _full(wb.shape), _rows(tm, D), _full((1, D))],
        out_specs=[_rows(tm, D)] * 2 + [_full((1, D))] * 3,
        compiler_params=_params("arbitrary"),
    )(dz, dy, wb, z_in, g_in)


def _pool_bwd(dz, x, pw, scale, alpha, tm, deps=()):
    T, D = x.shape
    G = len(POOL_WINDOWS)
    Dg = D // G
    H = POOL_HALO
    r = tm // H
    last = T // H - 1
    nt = T // tm
    n_ext = tm + H

    def body(dz_ref, dzh_ref, x_ref, xh_ref, pw_ref, sc_ref, dx_ref, gpw_ref, gsc_ref, edz_ref, ex_ref):
        i = pl.program_id(0)
        first = i == 0
        edz_ref[0:tm, :] = dz_ref[...]
        edz_ref[tm:, :] = jnp.where(i < nt - 1, dzh_ref[...], 0.0)
        ex_ref[0:H, :] = jnp.where(i > 0, xh_ref[...], 0.0)
        ex_ref[H:, :] = x_ref[...]
        row = i * tm + lax.broadcasted_iota(jnp.int32, (tm, 1), 0)
        rowf = (row + 1).astype(F32)
        erow = i * tm + lax.broadcasted_iota(jnp.int32, (n_ext, 1), 0)
        erowf = (erow + 1).astype(F32)
        g_scale, g_pw = [], []
        for gi, w in enumerate(POOL_WINDOWS):
            sl = slice(gi * Dg, (gi + 1) * Dg)
            s = ex_ref[:, sl]
            k = 1
            while k < w:
                s = s + pltpu.roll(s, k, 0)
                k *= 2
            xg = x_ref[:, sl]
            d = (s[H:, :] * (1.0 / jnp.minimum(rowf, float(w))) - xg).astype(BF16)
            wg = pw_ref[gi]
            premix = _mm(d, wg)
            dzg = dz_ref[:, sl]
            g_scale.append(_colsum(dzg * premix))
            dpre = edz_ref[:, sl] * sc_ref[:, sl]
            dpre_b = dpre.astype(BF16)
            g_pw.append(_mm_tn(d, dpre_b[0:tm, :]))
            dd = _mm_nt(dpre_b, wg)
            e = dd * (1.0 / jnp.minimum(erowf, float(w)))
            k = 1
            while k < w:
                e = e + pltpu.roll(e, n_ext - k, 0)
                k *= 2
            dx_ref[:, sl] = alpha * dzg + e[0:tm, :] - dd[0:tm, :]
        _acc(gsc_ref, first, jnp.concatenate(g_scale, axis=1))

        @pl.when(first)
        def _():
            for gi in range(G):
                gpw_ref[gi] = g_pw[gi]

        @pl.when(jnp.logical_not(first))
        def _():
            for gi in range(G):
                gpw_ref[gi] += g_pw[gi]

    return _tied_call(
        body, deps, name="pool_bwd", grid=(nt,),
        out_shape=[jax.ShapeDtypeStruct((T, D), F32), jax.ShapeDtypeStruct((G, Dg, Dg), F32),
                   jax.ShapeDtypeStruct((1, D), F32)],
        in_specs=[_rows(tm, D), pl.BlockSpec((H, D), lambda i: (jnp.minimum((i + 1) * r, last), 0)),
                  _rows(tm, D), pl.BlockSpec((H, D), lambda i: (jnp.maximum(i * r - 1, 0), 0)),
                  _full(pw.shape), _full((1, D))],
        out_specs=[_rows(tm, D), _full((G, Dg, Dg)), _full((1, D))],
        scratch_shapes=[pltpu.VMEM((n_ext, D), F32), pltpu.VMEM((n_ext, D), F32)],
        compiler_params=_params("arbitrary"),
    )(dz, dz, x, x, pw, scale)


def _adamw(name, recv, w, m, v, tm, layer=None, prev=None):
    R, C = w.shape[-2:]
    c1 = 1.0 - ADAM_B1 ** ADAM_STEP
    c2 = 1.0 - ADAM_B2 ** ADAM_STEP
    if layer is None:
        spec = _rows(tm, C)
    else:
        spec = pl.BlockSpec((None, tm, C), lambda i: (layer, i, 0))
    prev = list(prev) if prev is not None else []

    def body(r_ref, w_ref, m_ref, v_ref, *rest):
        g_ref, d_ref, nm_ref, nv_ref = rest[len(prev):]
        g = r_ref[0].astype(F32)
        for s in range(1, N_DEV):
            g = g + r_ref[s].astype(F32)
        m1 = ADAM_B1 * m_ref[...] + (1.0 - ADAM_B1) * g
        v1 = ADAM_B2 * v_ref[...] + (1.0 - ADAM_B2) * (g * g)
        m_hat = m1 / c1
        v_hat = v1 / c2
        g_ref[...] = g
        d_ref[...] = -ADAM_LR * (m_hat / (jnp.sqrt(v_hat) + ADAM_EPS) + ADAM_WD * w_ref[...])
        nm_ref[...] = m1
        nv_ref[...] = v1

    return pl.pallas_call(
        body, name=name, grid=(R // tm,),
        out_shape=[jax.ShapeDtypeStruct(w.shape, F32)] * 4,
        in_specs=[pl.BlockSpec((N_DEV, tm, C), lambda i: (0, i, 0))] + [spec] * 3
        + [pl.BlockSpec(memory_space=pl.ANY)] * len(prev),
        out_specs=[spec] * 4,
        input_output_aliases={4 + j: j for j in range(len(prev))},
        compiler_params=_params("parallel"),
    )(recv, w, m, v, *prev)


def _adamw_small(name, recv, w, m, v, pieces):
    R, C = w.shape
    c1 = 1.0 - ADAM_B1 ** ADAM_STEP
    c2 = 1.0 - ADAM_B2 ** ADAM_STEP
    n = len(pieces)

    def body(r_ref, w_ref, m_ref, v_ref, *rest):
        outs, packed = rest[:4 * n], rest[4 * n]
        g = r_ref[0]
        for s in range(1, N_DEV):
            g = g + r_ref[s]
        m1 = ADAM_B1 * m_ref[...] + (1.0 - ADAM_B1) * g
        v1 = ADAM_B2 * v_ref[...] + (1.0 - ADAM_B2) * (g * g)
        packed[0] = g
        packed[1] = -ADAM_LR * ((m1 / c1) / (jnp.sqrt(v1 / c2) + ADAM_EPS) + ADAM_WD * w_ref[...])
        packed[2] = m1
        packed[3] = v1
        for kind in range(4):
            for p, (r0, rows, shape) in enumerate(pieces):
                o_ref = outs[kind * n + p]
                if shape[-1] == C:
                    o_ref[...] = packed[kind, r0:r0 + rows, :].reshape(shape)
                else:
                    per = shape[-1] // C
                    for idx in range(rows):
                        l, q = divmod(idx, per)
                        o_ref[l:l + 1, q * C:(q + 1) * C] = packed[kind, r0 + idx:r0 + idx + 1, :]

    outs = pl.pallas_call(
        body, name=name,
        out_shape=[jax.ShapeDtypeStruct(shape, F32) for _ in range(4) for _, _, shape in pieces],
        scratch_shapes=[pltpu.VMEM((4, R, C), F32)],
    )(recv, w, m, v)
    return [outs[k * n:(k + 1) * n] for k in range(4)]


def _pad_rows(a, rows):
    return jnp.pad(a, ((0, rows - a.shape[0]), (0, 0)))


def kernel(x, pool_w, pool_scale, conv_w_in, conv_b_in, conv_dw, conv_dw_b, conv_ln_g, conv_ln_b, conv_w_out, conv_b_out, mix_ln_g, mix_ln_b, mlp_w1, mlp_b1, mlp_w2, mlp_b2, mlp_ln_g, mlp_ln_b, loss_target, m_pool_w, m_pool_scale, m_conv_w_in, m_conv_b_in, m_conv_dw, m_conv_dw_b, m_conv_ln_g, m_conv_ln_b, m_conv_w_out, m_conv_b_out, m_mix_ln_g, m_mix_ln_b, m_mlp_w1, m_mlp_b1, m_mlp_w2, m_mlp_b2, m_mlp_ln_g, m_mlp_ln_b, v_pool_w, v_pool_scale, v_conv_w_in, v_conv_b_in, v_conv_dw, v_conv_dw_b, v_conv_ln_g, v_conv_ln_b, v_conv_w_out, v_conv_b_out, v_mix_ln_g, v_mix_ln_b, v_mlp_w1, v_mlp_b1, v_mlp_w2, v_mlp_b2, v_mlp_ln_g, v_mlp_ln_b):
    _, T, D = x.shape
    L = mlp_w1.shape[0]
    assert L == 2 and pool_w.shape[0] == 1 and conv_w_in.shape[0] == 1
    G = pool_w.shape[1]
    Dg = D // G
    Fd = mlp_b1.shape[1]
    Fs = Fd // N_DEV
    Kc = conv_dw.shape[1]
    Dc = D // N_DEV
    alpha = float((2.0 * L) ** 0.25)
    x2d, tgt = x[0], loss_target[0]

    tm = _tile(T, 512)
    tm_wide = _tile(T, 512)
    tm_conv = _tile(T, 256)
    tm_wg = _tile(T, 2048)

    def pack_sh(dw, dwb, lg, lb, bo, bi):
        rows = jnp.concatenate([dw[0], dwb, lg, lb, bo, bi.reshape(2, Dc)], axis=0)
        return _pad_rows(rows, SH_ROWS)

    SH_ROWS = -(-(Kc + 6) // 8) * 8
    def pack_rep(ps, mg, mb, b1, b2, lg, lb):
        rows = jnp.concatenate([ps, mg, mb, b1.reshape(L * Fd // D, D), b2, lg, lb], axis=0)
        return _pad_rows(rows, REP_ROWS)

    n_rep = 1 + 2 * L + L * Fd // D + 3 * L
    REP_ROWS = -(-n_rep // 8) * 8

    w_sh = pack_sh(conv_dw, conv_dw_b, conv_ln_g, conv_ln_b, conv_b_out, conv_b_in)
    m_sh = pack_sh(m_conv_dw, m_conv_dw_b, m_conv_ln_g, m_conv_ln_b, m_conv_b_out, m_conv_b_in)
    v_sh = pack_sh(v_conv_dw, v_conv_dw_b, v_conv_ln_g, v_conv_ln_b, v_conv_b_out, v_conv_b_in)
    w_rep = pack_rep(pool_scale, mix_ln_g, mix_ln_b, mlp_b1, mlp_b2, mlp_ln_g, mlp_ln_b)
    m_rep = pack_rep(m_pool_scale, m_mix_ln_g, m_mix_ln_b, m_mlp_b1, m_mlp_b2, m_mlp_ln_g, m_mlp_ln_b)
    v_rep = pack_rep(v_pool_scale, v_mix_ln_g, v_mix_ln_b, v_mlp_b1, v_mlp_b2, v_mlp_ln_g, v_mlp_ln_b)

    groups = [[pool_w[0]], [mlp_w1[0]], [mlp_w2[0]], [conv_w_in[0], conv_w_out[0]], [mlp_w1[1]], [mlp_w2[1]]]
    handles, tokens = [], []
    for i, grp in enumerate(groups):
        tie = tokens[-1][0, 0] if tokens else 0.0
        srcs = [(a + tie).astype(BF16) for a in grp] + ([w_sh] if i == 0 else [])
        h, tk = _xstart("gather_%d" % i, srcs, [False] * len(srcs))
        handles.append(h)
        tokens.append(tk)
    pw_all, sh_all = _xwait(handles[0], tokens[-1])
    pw = pw_all.transpose(1, 0, 2, 3).reshape(G, Dg, Dg)
    dw_full = sh_all[:, 0:Kc].transpose(1, 0, 2).reshape(Kc, D)

    def sh_row(i):
        return sh_all[:, i].reshape(1, D)

    dwb_full, cg_full, cb_full, bout_full = (sh_row(Kc + i) for i in range(4))
    bin_full = sh_all[:, Kc + 4:Kc + 6].reshape(1, 2 * D)

    h0, z_m0, h0b = _pool_fwd(x2d, pw, pool_scale, mix_ln_g[0:1], mix_ln_b[0:1], alpha, tm, tuple(tokens))
    (w1b0,) = _xwait(handles[1], h0)
    u0 = _mlp_up("mlp_up0", h0b, w1b0, mlp_b1[0:1], tm)
    (w2b0,) = _xwait(handles[2], u0)
    x1, z_f0, x1b = _proj_ln("mlp_down0", u0, w2b0.reshape(Fd, D), mlp_b2[0:1], h0, mlp_ln_g[0:1], mlp_ln_b[0:1],
                        alpha, tm, act=True)
    win_b, wout_all = _xwait(handles[3], x1)
    w_out = wout_all.reshape(D, D)
    p, glu = _conv_in(x1b, win_b, bin_full, tm)
    cz, s = _dwconv_fwd(glu, dw_full, dwb_full, cg_full, cb_full, tm_conv)
    h1, z_m1, h1b = _proj_ln("conv_out", s, w_out, bout_full, x1, mix_ln_g[1:2], mix_ln_b[1:2], alpha, tm)
    (w1b1,) = _xwait(handles[4], h1)
    u1 = _mlp_up("mlp_up1", h1b, w1b1, mlp_b1[1:2], tm)
    (w2b1,) = _xwait(handles[5], u1)

    dz, dzb, g_fg1, g_fb1, g_b2_1, loss_cols = _proj_ln_loss(
        "mlp_down1", u1, w2b1.reshape(Fd, D), mlp_b2[1:2], h1, mlp_ln_g[1:2], mlp_ln_b[1:2], tgt, alpha, tm)
    gw2_1 = _wgrad("gw2_1", u1, dzb, False, 4, 1, tm_wg, act=True)
    e_w2_1, tk = _xstart("grads_w2_1", [gw2_1], [True])
    du, g_b1_1 = _mlp_act_bwd("mlp_act_bwd1", dzb, u1, w2b1, tm, (tk,))
    dz, dzb, g_mg1, g_mb1, g_bout = _dx_proj("mlp_in_bwd1", dz, du, w1b1, z_m1, mix_ln_g[1:2], alpha, tm_wide)
    gw1_1 = _wgrad("gw1_1", h1b, du, True, 1, 2, tm_wg)
    e_w1_1, tk = _xstart("grads_w1_1", [gw1_1], [True])
    gwout = _wgrad("gw_out", s, dzb, False, 1, 1, tm_wg)
    dc, g_cg, g_cb, g_dwb = _conv_out_bwd(dzb, w_out, cz, cg_full, cb_full, tm, (tk,))
    dp, g_dw, g_bin = _dwconv_bwd(dc, glu, p, dw_full, tm_conv)
    gwin = _wgrad("gw_in", x1b, dp, True, 1, 2, tm_wg)
    e_conv, tk = _xstart("grads_conv", [gwin, gwout], [True, True])
    dz, dzb, g_fg0, g_fb0, g_b2_0 = _dx_proj("conv_in_bwd", dz, dp, win_b, z_f0, mlp_ln_g[0:1], alpha, tm, (tk,))
    gw2_0 = _wgrad("gw2_0", u0, dzb, False, 4, 1, tm_wg, act=True)
    e_w2_0, tk = _xstart("grads_w2_0", [gw2_0], [True])
    du, g_b1_0 = _mlp_act_bwd("mlp_act_bwd0", dzb, u0, w2b0, tm, (tk,))
    dz, dzb, g_mg0, g_mb0, _ = _dx_proj("mlp_in_bwd0", dz, du, w1b0, z_m0, mix_ln_g[0:1], alpha, tm_wide)
    gw1_0 = _wgrad("gw1_0", h0b, du, True, 1, 2, tm_wg)
    e_w1_0, tk = _xstart("grads_w1_0", [gw1_0], [True])
    grad_x, g_pw, g_ps = _pool_bwd(dz, x2d, pw, pool_scale, alpha, tm, (tk,))

    loss = lax.psum(0.5 / D * jnp.sum(loss_cols), MESH_AXES)

    gpw_b = g_pw.reshape(G, N_DEV, Dg // N_DEV, Dg).transpose(1, 0, 2, 3).astype(BF16)

    def to_dev(vec, rows):
        return vec.reshape(rows, N_DEV, Dc).transpose(1, 0, 2)

    g_sh = jnp.concatenate(
        [to_dev(g_dw[0:Kc], Kc), to_dev(g_dwb, 1), to_dev(g_cg, 1), to_dev(g_cb, 1), to_dev(g_bout, 1),
         g_bin.reshape(N_DEV, 2, Dc), jnp.zeros((N_DEV, SH_ROWS - Kc - 6, Dc), F32)], axis=1)
    g_rep = _pad_rows(jnp.concatenate(
        [g_ps, g_mg0, g_mg1, g_mb0, g_mb1, g_b1_0.reshape(Fd // D, D), g_b1_1.reshape(Fd // D, D),
         g_b2_0, g_b2_1, g_fg0, g_fg1, g_fb0, g_fb1], axis=0), REP_ROWS)

    e_small, tk = _xstart("grads_small", [gpw_b, g_sh, g_rep], [True, True, False])

    def upd(name, recv, w, m, v):
        shape = w.shape
        C = shape[-1]
        R = w.size // C
        outs = _adamw(name, recv.reshape(N_DEV, R, C), w.reshape(R, C), m.reshape(R, C), v.reshape(R, C), _tile(R, 256))
        return [o.reshape(shape) for o in outs]

    def upd_layer(name, recv, w, m, v, layer, prev):
        return _adamw(name, recv, w, m, v, _tile(w.shape[1], 256), layer, prev)

    (r_w2_1,) = _xwait(e_w2_1, tk)
    o_w2 = upd_layer("adam_w2_1", r_w2_1, mlp_w2, m_mlp_w2, v_mlp_w2, 1, None)
    (r_w1_1,) = _xwait(e_w1_1, o_w2[0])
    o_w1 = upd_layer("adam_w1_1", r_w1_1, mlp_w1, m_mlp_w1, v_mlp_w1, 1, None)
    r_win, r_wout = _xwait(e_conv, o_w1[0])
    o_win = upd("adam_w_in", r_win, conv_w_in, m_conv_w_in, v_conv_w_in)
    o_wout = upd("adam_w_out", r_wout, conv_w_out, m_conv_w_out, v_conv_w_out)
    (r_w2_0,) = _xwait(e_w2_0, o_wout[0])
    o_w2 = upd_layer("adam_w2_0", r_w2_0, mlp_w2, m_mlp_w2, v_mlp_w2, 0, o_w2)
    (r_w1_0,) = _xwait(e_w1_0, o_w2[0])
    o_w1 = upd_layer("adam_w1_0", r_w1_0, mlp_w1, m_mlp_w1, v_mlp_w1, 0, o_w1)
    r_pw, r_sh, r_rep = _xwait(e_small, o_w1[0])
    o_pw = upd("adam_pool_w", r_pw, pool_w, m_pool_w, v_pool_w)
    sh_pieces = [(0, Kc, (1, Kc, Dc))] + [(Kc + i, 1, (1, Dc)) for i in range(4)] + [(Kc + 4, 2, (1, 2 * Dc))]
    rep_pieces, o = [], 0
    for rows, shape in ((1, (1, D)), (L, (L, D)), (L, (L, D)), (L * Fd // D, (L, Fd)), (L, (L, D)), (L, (L, D)),
                        (L, (L, D))):
        rep_pieces.append((o, rows, shape))
        o += rows
    o_sh = _adamw_small("adam_conv_vec", r_sh, w_sh, m_sh, v_sh, sh_pieces)
    o_rep = _adamw_small("adam_replicated", r_rep, w_rep, m_rep, v_rep, rep_pieces)

    results = []
    for kind in range(4):
        dwv, dwb, lg, lb, bo, bi = o_sh[kind]
        ps, mg, mb, b1, b2, fg, fb = o_rep[kind]
        results.append([o_pw[kind], ps, o_win[kind], bi, dwv, dwb, lg, lb, o_wout[kind], bo, mg, mb,
                        o_w1[kind], b1, o_w2[kind], b2, fg, fb])
    return (loss, grad_x[None], *results[0], *results[1], *results[2], *results[3])
```

```python
import jax
import jax.numpy as jnp
from jax import lax
from jax.experimental import pallas as pl
from jax.experimental.pallas import tpu as pltpu

N_DEV = 8
MESH_AXES = ("x", "y", "c")
POOL_WINDOWS = (2, 4, 8, 16)
POOL_HALO = 16
CONV_HALO = 32
LN_EPS = 1e-5
ADAM_LR = 0.001
ADAM_B1 = 0.9
ADAM_B2 = 0.999
ADAM_EPS = 1e-08
ADAM_WD = 0.01
ADAM_STEP = 10
VMEM_LIMIT = 56 * 1024 * 1024

F32 = jnp.float32
BF16 = jnp.bfloat16


def _mm(a, b):
    return lax.dot_general(a, b, (((1,), (0,)), ((), ())), preferred_element_type=F32)


def _mm_nt(a, b):
    return lax.dot_general(a, b, (((1,), (1,)), ((), ())), preferred_element_type=F32)


def _mm_tn(a, b):
    return lax.dot_general(a, b, (((0,), (0,)), ((), ())), preferred_element_type=F32)


def _tile(n, pref):
    t = min(n, pref)
    assert n % t == 0, (n, pref)
    return t


def _params(*sem):
    return pltpu.CompilerParams(dimension_semantics=sem, vmem_limit_bytes=VMEM_LIMIT)


def _full(shape):
    nd = len(shape)
    return pl.BlockSpec(shape, lambda *_: (0,) * nd)


def _rows(tm, d):
    return pl.BlockSpec((tm, d), lambda i: (i, 0))


def _ln_stats(z):
    mu = jnp.mean(z, axis=-1, keepdims=True)
    zc = z - mu
    var = jnp.mean(zc * zc, axis=-1, keepdims=True)
    rstd = lax.rsqrt(var + LN_EPS)
    return zc * rstd, rstd


def _ln_fwd(z, g, b):
    xhat, _ = _ln_stats(z)
    return xhat * g + b


def _ln_bwd(dy, z, g):
    xhat, rstd = _ln_stats(z)
    dxh = dy * g
    m1 = jnp.mean(dxh, axis=-1, keepdims=True)
    m2 = jnp.mean(dxh * xhat, axis=-1, keepdims=True)
    return rstd * (dxh - m1 - xhat * m2), xhat


def _colsum(v):
    return jnp.sum(v, axis=0, keepdims=True)


def _sigmoid(v):
    return 1.0 / (1.0 + jnp.exp(-v))


def _acc(ref, first, val):
    @pl.when(first)
    def _():
        ref[...] = val

    @pl.when(jnp.logical_not(first))
    def _():
        ref[...] += val


_HBM = pl.BlockSpec(memory_space=pltpu.HBM)
_SEM = pl.BlockSpec(memory_space=pltpu.SEMAPHORE)
_EFFECT = pltpu.SideEffectType.DATAFLOW_SIDE_EFFECTING


def _peers():
    x, y, c = (lax.axis_index(a) for a in MESH_AXES)
    out = []
    for d in range(1, N_DEV):
        px = (x + ((d >> 2) & 1)) % 2
        py = (y + ((d >> 1) & 1)) % 2
        pc = (c + (d & 1)) % 2
        out.append((d - 1, (px, py, pc), 4 * px + 2 * py + pc))
    return 4 * x + 2 * y + c, out


def _remote_copies(src_refs, land_refs, scatter, send_sems, recv_sems):
    me, peers = _peers()
    copies = []
    for i, pos, pid in peers:
        for k, (src, land) in enumerate(zip(src_refs, land_refs)):
            copies.append(pltpu.make_async_remote_copy(
                src_ref=src.at[pid] if scatter[k] else src, dst_ref=land.at[me],
                send_sem=send_sems.at[k * (N_DEV - 1) + i], recv_sem=recv_sems.at[k * (N_DEV - 1) + i],
                device_id=pos, device_id_type=pl.DeviceIdType.MESH))
    return copies


def _xstart(name, srcs, scatter):
    n = len(srcs)
    me = 4 * lax.axis_index("x") + 2 * lax.axis_index("y") + lax.axis_index("c")
    lands = []
    for s, sc in zip(srcs, scatter):
        own = lax.dynamic_index_in_dim(s, me, 0, keepdims=True) if sc else s[None]
        shape = s.shape if sc else (N_DEV,) + s.shape
        lands.append(lax.dynamic_update_slice(lax.empty(shape, s.dtype), own, (me,) + (0,) * (len(shape) - 1)))

    def body(*refs):
        src_refs, land_refs = refs[:n], refs[n:2 * n]
        send_sems, recv_sems = refs[2 * n], refs[2 * n + 1]
        token = refs[-1]
        for cp in _remote_copies(src_refs, land_refs, scatter, send_sems, recv_sems):
            cp.start()
        token[...] = jnp.zeros_like(token)

    outs = pl.pallas_call(
        body, name=name,
        out_shape=(pltpu.SemaphoreType.DMA((n * (N_DEV - 1),)), pltpu.SemaphoreType.DMA((n * (N_DEV - 1),)),
                   *[pltpu.HBM(a.shape, a.dtype) for a in srcs + lands], jax.ShapeDtypeStruct((8, 128), F32)),
        in_specs=(_HBM,) * (2 * n),
        out_specs=(_SEM, _SEM) + (_HBM,) * (2 * n) + (pl.BlockSpec(memory_space=pltpu.VMEM),),
        input_output_aliases={i: 2 + i for i in range(2 * n)},
        compiler_params=pltpu.CompilerParams(has_side_effects=_EFFECT),
    )(*[pltpu.with_memory_space_constraint(a, pltpu.HBM) for a in srcs + lands])
    return (name, scatter, outs[0], outs[1], outs[2:2 + n], outs[2 + n:2 + 2 * n]), outs[-1]


def _xwait(handle, after):
    name, scatter, send_sems, recv_sems, srcs, lands = handle
    n = len(srcs)

    def body(*refs):
        src_refs, land_refs = refs[:n], refs[n:2 * n]
        send, recv = refs[2 * n], refs[2 * n + 1]
        copies = _remote_copies(src_refs, land_refs, scatter, send, recv)
        for cp in copies:
            cp.wait_send()
        for cp in copies:
            cp.wait_recv()

    outs = pl.pallas_call(
        body, name=name + "_wait",
        out_shape=tuple(pltpu.HBM(a.shape, a.dtype) for a in (*srcs, *lands)),
        in_specs=(_HBM,) * (2 * n) + (_SEM, _SEM, pl.BlockSpec(memory_space=pl.ANY)),
        out_specs=(_HBM,) * (2 * n),
        input_output_aliases={i: i for i in range(2 * n)},
        compiler_params=pltpu.CompilerParams(has_side_effects=_EFFECT),
    )(*srcs, *lands, send_sems, recv_sems, after)
    return outs[n:]


def _tied_call(body, deps, in_specs, **kw):
    nd = len(deps)

    def tied_body(*refs):
        body(*refs[nd:])

    call = pl.pallas_call(tied_body, in_specs=[pl.BlockSpec(memory_space=pl.ANY)] * nd + list(in_specs), **kw)
    return lambda *args: call(*deps, *args)


def _pool_fwd(x, pw, scale, g, b, alpha, tm, deps=()):
    T, D = x.shape
    G = len(POOL_WINDOWS)
    Dg = D // G
    H = POOL_HALO
    r = tm // H

    def body(x_ref, xh_ref, pw_ref, sc_ref, g_ref, b_ref, y_ref, z_ref, yb_ref, ext_ref):
        i = pl.program_id(0)
        ext_ref[0:H, :] = jnp.where(i > 0, xh_ref[...], 0.0)
        ext_ref[H:, :] = x_ref[...]
        row = i * tm + lax.broadcasted_iota(jnp.int32, (tm, 1), 0)
        rowf = (row + 1).astype(F32)
        for gi, w in enumerate(POOL_WINDOWS):
            sl = slice(gi * Dg, (gi + 1) * Dg)
            s = ext_ref[:, sl]
            k = 1
            while k < w:
                s = s + pltpu.roll(s, k, 0)
                k *= 2
            inv = 1.0 / jnp.minimum(rowf, float(w))
            xg = x_ref[:, sl]
            d = s[H:, :] * inv - xg
            mix = _mm(d.astype(BF16), pw_ref[gi]) * sc_ref[:, sl]
            z_ref[:, sl] = alpha * xg + mix
        y = _ln_fwd(z_ref[...], g_ref[...], b_ref[...])
        y_ref[...] = y
        yb_ref[...] = y.astype(BF16)

    return _tied_call(
        body, deps, name="pool_fwd", grid=(T // tm,),
        out_shape=[jax.ShapeDtypeStruct((T, D), F32)] * 2 + [jax.ShapeDtypeStruct((T, D), BF16)],
        in_specs=[_rows(tm, D), pl.BlockSpec((H, D), lambda i: (jnp.maximum(i * r - 1, 0), 0)),
                  _full(pw.shape), _full((1, D)), _full((1, D)), _full((1, D))],
        out_specs=[_rows(tm, D)] * 3,
        scratch_shapes=[pltpu.VMEM((tm + H, D), F32)],
        compiler_params=_params("parallel"),
    )(x, x, pw, scale, g, b)


def _mlp_up(name, h, w1b, b1, tm):
    T, D = h.shape
    nb, _, Fs = w1b.shape
    Fd = nb * Fs

    def body(h_ref, w_ref, b_ref, u_ref):
        hb = h_ref[...]
        for j in range(nb):
            sl = slice(j * Fs, (j + 1) * Fs)
            u_ref[:, sl] = (_mm(hb, w_ref[j]) + b_ref[:, sl]).astype(BF16)

    return pl.pallas_call(
        body, name=name, grid=(T // tm,),
        out_shape=jax.ShapeDtypeStruct((T, Fd), BF16),
        in_specs=[_rows(tm, D), _full(w1b.shape), _full((1, Fd))],
        out_specs=_rows(tm, Fd),
        compiler_params=_params("parallel"),
    )(h, w1b, b1)


def _sq_relu(u):
    r = jnp.maximum(u, 0.0)
    return r * r


def _proj_ln(name, a, w, bias, res, g, b, alpha, tm, act=False):
    T, K = a.shape
    D = w.shape[1]

    def body(a_ref, w_ref, bias_ref, res_ref, g_ref, b_ref, y_ref, z_ref, yb_ref):
        av = _sq_relu(a_ref[...]) if act else a_ref[...]
        z = alpha * res_ref[...] + _mm(av, w_ref[...]) + bias_ref[...]
        z_ref[...] = z
        y = _ln_fwd(z, g_ref[...], b_ref[...])
        y_ref[...] = y
        yb_ref[...] = y.astype(BF16)

    return pl.pallas_call(
        body, name=name, grid=(T // tm,),
        out_shape=[jax.ShapeDtypeStruct((T, D), F32)] * 2 + [jax.ShapeDtypeStruct((T, D), BF16)],
        in_specs=[_rows(tm, K), _full((K, D)), _full((1, D)), _rows(tm, D), _full((1, D)), _full((1, D))],
        out_specs=[_rows(tm, D)] * 3,
        compiler_params=_params("parallel"),
    )(a, w, bias, res, g, b)


def _proj_ln_loss(name, a, w, bias, res, g, b, target, alpha, tm):
    T, K = a.shape
    D = w.shape[1]

    def body(a_ref, w_ref, bias_ref, res_ref, g_ref, b_ref, t_ref, dz_ref, dzb_ref, gg_ref, gb_ref, sdz_ref, loss_ref):
        first = pl.program_id(0) == 0
        z = alpha * res_ref[...] + _mm(_sq_relu(a_ref[...]), w_ref[...]) + bias_ref[...]
        gv = g_ref[...]
        e = _ln_fwd(z, gv, b_ref[...]) - t_ref[...]
        dy = e * (1.0 / D)
        dz, xhat = _ln_bwd(dy, z, gv)
        dz_ref[...] = dz
        dzb_ref[...] = dz.astype(BF16)
        _acc(gg_ref, first, _colsum(dy * xhat))
        _acc(gb_ref, first, _colsum(dy))
        _acc(sdz_ref, first, _colsum(dz))
        _acc(loss_ref, first, _colsum(e * e))

    return pl.pallas_call(
        body, name=name, grid=(T // tm,),
        out_shape=[jax.ShapeDtypeStruct((T, D), F32), jax.ShapeDtypeStruct((T, D), BF16)]
        + [jax.ShapeDtypeStruct((1, D), F32)] * 4,
        in_specs=[_rows(tm, K), _full((K, D)), _full((1, D)), _rows(tm, D), _full((1, D)), _full((1, D)), _rows(tm, D)],
        out_specs=[_rows(tm, D)] * 2 + [_full((1, D))] * 4,
        compiler_params=_params("arbitrary"),
    )(a, w, bias, res, g, b, target)


def _conv_in(x, wb, b_in, tm):
    T, D = x.shape
    nb, _, Ns = wb.shape
    half = nb // 2

    def body(x_ref, w_ref, b_ref, p_ref, glu_ref):
        xb = x_ref[...]
        for j in range(half):
            sa = slice(j * Ns, (j + 1) * Ns)
            sg = slice(D + j * Ns, D + (j + 1) * Ns)
            a = _mm(xb, w_ref[j]) + b_ref[:, sa]
            gate = _mm(xb, w_ref[half + j]) + b_ref[:, sg]
            p_ref[:, sa] = a.astype(BF16)
            p_ref[:, sg] = gate.astype(BF16)
            glu_ref[:, sa] = a * _sigmoid(gate)

    return pl.pallas_call(
        body, name="conv_in", grid=(T // tm,),
        out_shape=[jax.ShapeDtypeStruct((T, 2 * D), BF16), jax.ShapeDtypeStruct((T, D), F32)],
        in_specs=[_rows(tm, D), _full(wb.shape), _full((1, 2 * D))],
        out_specs=[_rows(tm, 2 * D), _rows(tm, D)],
        compiler_params=_params("parallel"),
    )(x, wb, b_in)


def _lane_chunk(d):
    return 128 if d % 128 == 0 else d


SUBLANES = 8


def _row_chunk(tm, pref):
    return pref if tm % pref == 0 else tm


def _dwconv_fwd(glu, dw, dw_b, g, b, tm):
    T, D = glu.shape
    K = dw.shape[0]
    H = CONV_HALO
    off = H - (K - 1)
    r = tm // H
    cc = _lane_chunk(D)

    def body(x_ref, xh_ref, dw_ref, dwb_ref, g_ref, b_ref, cz_ref, s_ref, ext_ref, sh_ref):
        i = pl.program_id(0)
        ext_ref[0:H, :] = jnp.where(i > 0, xh_ref[...], 0.0)
        ext_ref[H:, :] = x_ref[...]
        for ph in range(min(SUBLANES, K)):
            n = tm + SUBLANES * ((K - 1 - ph) // SUBLANES)
            sh_ref[ph, 0:n, :] = ext_ref[off + ph:off + ph + n, :]
        rc = _row_chunk(tm, 128)
        for c0 in range(0, D, cc):
            cs = slice(c0, c0 + cc)
            for r0 in range(0, tm, rc):
                acc = jnp.zeros((rc, cc), F32) + dwb_ref[:, cs]
                for k in range(K):
                    q, ph = divmod(k, SUBLANES)
                    acc = acc + sh_ref[ph, SUBLANES * q + r0:SUBLANES * q + r0 + rc, cs] * dw_ref[k:k + 1, cs]
                cz_ref[r0:r0 + rc, cs] = acc
        ln = _ln_fwd(cz_ref[...], g_ref[...], b_ref[...])
        s_ref[...] = (ln * _sigmoid(ln)).astype(BF16)

    return pl.pallas_call(
        body, name="dwconv_fwd", grid=(T // tm,),
        out_shape=[jax.ShapeDtypeStruct((T, D), F32), jax.ShapeDtypeStruct((T, D), BF16)],
        in_specs=[_rows(tm, D), pl.BlockSpec((H, D), lambda i: (jnp.maximum(i * r - 1, 0), 0)),
                  _full((K, D)), _full((1, D)), _full((1, D)), _full((1, D))],
        out_specs=[_rows(tm, D)] * 2,
        scratch_shapes=[pltpu.VMEM((tm + H, D), F32),
                        pltpu.VMEM((SUBLANES, tm + SUBLANES * ((K - 1) // SUBLANES), D), F32)],
        compiler_params=_params("parallel"),
    )(glu, glu, dw, dw_b, g, b)


def _ln_bwd_store(dy, z_ref, g_ref, first, dz_ref, dzb_ref, gg_ref, gb_ref, sdz_ref):
    dz, xhat = _ln_bwd(dy, z_ref[...], g_ref[...])
    dz_ref[...] = dz
    dzb_ref[...] = dz.astype(BF16)
    _acc(gg_ref, first, _colsum(dy * xhat))
    _acc(gb_ref, first, _colsum(dy))
    _acc(sdz_ref, first, _colsum(dz))


def _ln_bwd_outs(T, D):
    shapes = [jax.ShapeDtypeStruct((T, D), F32), jax.ShapeDtypeStruct((T, D), BF16)] + [jax.ShapeDtypeStruct((1, D), F32)] * 3
    return shapes


def _mlp_act_bwd(name, dzb, u, w2b, tm, deps=()):
    T, D = dzb.shape
    nb, Fs, _ = w2b.shape
    Fd = nb * Fs

    def body(dzb_ref, u_ref, w2_ref, du_ref, gb1_ref):
        dzv = dzb_ref[...]
        sums = []
        for j in range(nb):
            sl = slice(j * Fs, (j + 1) * Fs)
            du = _mm_nt(dzv, w2_ref[j]) * (2.0 * jnp.maximum(u_ref[:, sl].astype(F32), 0.0))
            du_ref[:, sl] = du.astype(BF16)
            sums.append(_colsum(du))
        _acc(gb1_ref, pl.program_id(0) == 0, jnp.concatenate(sums, axis=1))

    return _tied_call(
        body, deps, name=name, grid=(T // tm,),
        out_shape=[jax.ShapeDtypeStruct((T, Fd), BF16), jax.ShapeDtypeStruct((1, Fd), F32)],
        in_specs=[_rows(tm, D), _rows(tm, Fd), _full(w2b.shape)],
        out_specs=[_rows(tm, Fd), _full((1, Fd))],
        compiler_params=_params("arbitrary"),
    )(dzb, u, w2b)


def _wgrad(name, xm, dy, col_blocks, nk, nj, tm, act=False):
    T, K = xm.shape
    N = dy.shape[1]
    Kb, Nb = K // nk, N // nj
    nt = T // tm
    if col_blocks:
        per, Ns = N_DEV // nj, N // N_DEV
        out_shape = (N_DEV, K, Ns)
        out_spec = pl.BlockSpec((per, Kb, Ns), lambda k, j, t: (j, k, 0))
    else:
        per, Ks = N_DEV // nk, K // N_DEV
        out_shape = (N_DEV, Ks, N)
        out_spec = pl.BlockSpec((per, Ks, Nb), lambda k, j, t: (k, 0, j))

    def body(x_ref, dy_ref, o_ref, acc_ref):
        t = pl.program_id(2)
        xb = _sq_relu(x_ref[...]) if act else x_ref[...].astype(BF16)
        _acc(acc_ref, t == 0, _mm_tn(xb, dy_ref[...]))

        @pl.when(t == nt - 1)
        def _():
            for q in range(per):
                if col_blocks:
                    o_ref[q] = acc_ref[:, q * Ns:(q + 1) * Ns].astype(BF16)
                else:
                    o_ref[q] = acc_ref[q * Ks:(q + 1) * Ks, :].astype(BF16)

    return pl.pallas_call(
        body, name=name, grid=(nk, nj, nt),
        out_shape=jax.ShapeDtypeStruct(out_shape, BF16),
        in_specs=[pl.BlockSpec((tm, Kb), lambda k, j, t: (t, k)), pl.BlockSpec((tm, Nb), lambda k, j, t: (t, j))],
        out_specs=out_spec,
        scratch_shapes=[pltpu.VMEM((Kb, Nb), F32)],
        compiler_params=_params("parallel", "parallel", "arbitrary"),
    )(xm, dy)


def _conv_out_bwd(dzb, w_out, cz, s, g, b, tm, deps=()):
    T, D = cz.shape
    nt = T // tm
    Ks = D // N_DEV

    def body(dz_ref, w_ref, cz_ref, s_ref, g_ref, b_ref, dc_ref, gg_ref, gb_ref, sdc_ref, gw_ref, acc_ref):
        first = pl.program_id(0) == 0
        dzv = dz_ref[...]
        gw = _mm_tn(s_ref[...], dzv)
        ds = _mm_nt(dzv, w_ref[...])
        czv = cz_ref[...]
        gv = g_ref[...]
        ln = _ln_fwd(czv, gv, b_ref[...])
        sg = _sigmoid(ln)
        dln = ds * (sg * (1.0 + ln * (1.0 - sg)))
        dc, xhat = _ln_bwd(dln, czv, gv)
        dc_ref[...] = dc
        _acc(gg_ref, first, _colsum(dln * xhat))
        _acc(gb_ref, first, _colsum(dln))
        _acc(sdc_ref, first, _colsum(dc))
        _acc(acc_ref, first, gw)

        @pl.when(pl.program_id(0) == nt - 1)
        def _():
            for q in range(N_DEV):
                gw_ref[q] = acc_ref[q * Ks:(q + 1) * Ks, :].astype(BF16)

    return _tied_call(
        body, deps, name="conv_out_bwd", grid=(nt,),
        out_shape=[jax.ShapeDtypeStruct((T, D), F32)] + [jax.ShapeDtypeStruct((1, D), F32)] * 3
        + [jax.ShapeDtypeStruct((N_DEV, Ks, D), BF16)],
        in_specs=[_rows(tm, D), _full((D, D)), _rows(tm, D), _rows(tm, D), _full((1, D)), _full((1, D))],
        out_specs=[_rows(tm, D)] + [_full((1, D))] * 3 + [_full((N_DEV, Ks, D))],
        scratch_shapes=[pltpu.VMEM((D, D), F32)],
        compiler_params=_params("arbitrary"),
    )(dzb, w_out, cz, s, g, b)


def _dwconv_bwd(dc, glu, p, dw, tm):
    T, D = dc.shape
    K = dw.shape[0]
    Kp = -(-K // 8) * 8
    H = CONV_HALO
    off = H - (K - 1)
    r = tm // H
    last = T // H - 1
    nt = T // tm
    cc = _lane_chunk(D)

    def body(dc_ref, dch_ref, x_ref, xh_ref, p_ref, dw_ref, dp_ref, gdw_ref, gbin_ref, edc_ref, ex_ref, dglu_ref,
             gacc_ref, shd_ref, shx_ref):
        i = pl.program_id(0)
        first = i == 0
        edc_ref[0:tm, :] = dc_ref[...]
        edc_ref[tm:, :] = jnp.where(i < nt - 1, dch_ref[...], 0.0)
        ex_ref[0:H, :] = jnp.where(i > 0, xh_ref[...], 0.0)
        ex_ref[H:, :] = x_ref[...]

        @pl.when(first)
        def _():
            gacc_ref[...] = jnp.zeros_like(gacc_ref)

        for ph in range(min(SUBLANES, K)):
            n = tm + SUBLANES * ((K - 1 - ph) // SUBLANES)
            shd_ref[ph, 0:n, :] = edc_ref[ph:ph + n, :]
            shx_ref[ph, 0:n, :] = ex_ref[off + ph:off + ph + n, :]
        rc = _row_chunk(tm, 64)
        for c0 in range(0, D, cc):
            cs = slice(c0, c0 + cc)
            for r0 in range(0, tm, rc):
                dcv = dc_ref[r0:r0 + rc, cs]
                acc = jnp.zeros((rc, cc), F32)
                for m in range(K):
                    q, ph = divmod(m, SUBLANES)
                    lo = SUBLANES * q + r0
                    acc = acc + shd_ref[ph, lo:lo + rc, cs] * dw_ref[K - 1 - m:K - m, cs]
                dglu_ref[r0:r0 + rc, cs] = acc
                for k in range(K):
                    q, ph = divmod(k, SUBLANES)
                    lo = SUBLANES * q + r0
                    part = (dcv * shx_ref[ph, lo:lo + rc, cs]).reshape(rc // SUBLANES, SUBLANES, cc)
                    gacc_ref[k, :, cs] += jnp.sum(part, axis=0)

        @pl.when(i == nt - 1)
        def _():
            gdw_ref[...] = jnp.zeros_like(gdw_ref)
            gdw_ref[0:K, :] = jnp.sum(gacc_ref[...], axis=1)
        dglu = dglu_ref[...]
        a = p_ref[:, 0:D].astype(F32)
        sg = _sigmoid(p_ref[:, D:2 * D].astype(F32))
        da = dglu * sg
        dgate = dglu * a * (sg * (1.0 - sg))
        dp_ref[:, 0:D] = da.astype(BF16)
        dp_ref[:, D:2 * D] = dgate.astype(BF16)
        _acc(gbin_ref.at[:, 0:D], first, _colsum(da))
        _acc(gbin_ref.at[:, D:2 * D], first, _colsum(dgate))

    return pl.pallas_call(
        body, name="dwconv_bwd", grid=(nt,),
        out_shape=[jax.ShapeDtypeStruct((T, 2 * D), BF16), jax.ShapeDtypeStruct((Kp, D), F32),
                   jax.ShapeDtypeStruct((1, 2 * D), F32)],
        in_specs=[_rows(tm, D), pl.BlockSpec((H, D), lambda i: (jnp.minimum((i + 1) * r, last), 0)),
                  _rows(tm, D), pl.BlockSpec((H, D), lambda i: (jnp.maximum(i * r - 1, 0), 0)),
                  _rows(tm, 2 * D), _full((K, D))],
        out_specs=[_rows(tm, 2 * D), _full((Kp, D)), _full((1, 2 * D))],
        scratch_shapes=[pltpu.VMEM((tm + H, D), F32), pltpu.VMEM((tm + H, D), F32), pltpu.VMEM((tm, D), F32),
                        pltpu.VMEM((K, SUBLANES, D), F32),
                        pltpu.VMEM((SUBLANES, tm + SUBLANES * ((K - 1) // SUBLANES), D), F32),
                        pltpu.VMEM((SUBLANES, tm + SUBLANES * ((K - 1) // SUBLANES), D), F32)],
        compiler_params=_params("arbitrary"),
    )(dc, dc, glu, glu, p, dw)


def _dx_proj(name, dz, dy, wb, z_in, g_in, alpha, tm, deps=()):
    T, D = dz.shape
    nb, _, Ns = wb.shape
    N = nb * Ns

    def body(dz_ref, dy_ref, w_ref, zin_ref, gin_ref, dzo_ref, dzob_ref, gg_ref, gb_ref, sdz_ref):
        dx = alpha * dz_ref[...]
        for j in range(nb):
            dx = dx + _mm_nt(dy_ref[:, j * Ns:(j + 1) * Ns], w_ref[j])
        _ln_bwd_store(dx, zin_ref, gin_ref, pl.program_id(0) == 0, dzo_ref, dzob_ref, gg_ref, gb_ref, sdz_ref)

    return _tied_call(
        body, deps, name=name, grid=(T // tm,),
        out_shape=_ln_bwd_outs(T, D),
        in_specs=[_rows(tm, D), _rows(tm, N), _full(wb.shape), _rows(tm, D), _full((1, D))],
        out_specs=[_rows(tm, D)] * 2 + [_full((1, D))] * 3,
        compiler_params=_params("arbitrary"),
    )(dz, dy, wb, z_in, g_in)


def _pool_bwd(dz, x, pw, scale, alpha, tm, deps=()):
    T, D = x.shape
    G = len(POOL_WINDOWS)
    Dg = D // G
    H = POOL_HALO
    r = tm // H
    last = T // H - 1
    nt = T // tm
    n_ext = tm + H

    def body(dz_ref, dzh_ref, x_ref, xh_ref, pw_ref, sc_ref, dx_ref, gpw_ref, gsc_ref, edz_ref, ex_ref):
        i = pl.program_id(0)
        first = i == 0
        edz_ref[0:tm, :] = dz_ref[...]
        edz_ref[tm:, :] = jnp.where(i < nt - 1, dzh_ref[...], 0.0)
        ex_ref[0:H, :] = jnp.where(i > 0, xh_ref[...], 0.0)
        ex_ref[H:, :] = x_ref[...]
        row = i * tm + lax.broadcasted_iota(jnp.int32, (tm, 1), 0)
        rowf = (row + 1).astype(F32)
        erow = i * tm + lax.broadcasted_iota(jnp.int32, (n_ext, 1), 0)
        erowf = (erow + 1).astype(F32)
        g_scale, g_pw = [], []
        for gi, w in enumerate(POOL_WINDOWS):
            sl = slice(gi * Dg, (gi + 1) * Dg)
            s = ex_ref[:, sl]
            k = 1
            while k < w:
                s = s + pltpu.roll(s, k, 0)
                k *= 2
            xg = x_ref[:, sl]
            d = (s[H:, :] * (1.0 / jnp.minimum(rowf, float(w))) - xg).astype(BF16)
            wg = pw_ref[gi]
            premix = _mm(d, wg)
            dzg = dz_ref[:, sl]
            g_scale.append(_colsum(dzg * premix))
            dpre = edz_ref[:, sl] * sc_ref[:, sl]
            dpre_b = dpre.astype(BF16)
            g_pw.append(_mm_tn(d, dpre_b[0:tm, :]))
            dd = _mm_nt(dpre_b, wg)
            e = dd * (1.0 / jnp.minimum(erowf, float(w)))
            k = 1
            while k < w:
                e = e + pltpu.roll(e, n_ext - k, 0)
                k *= 2
            dx_ref[:, sl] = alpha * dzg + e[0:tm, :] - dd[0:tm, :]
        _acc(gsc_ref, first, jnp.concatenate(g_scale, axis=1))

        @pl.when(first)
        def _():
            for gi in range(G):
                gpw_ref[gi] = g_pw[gi]

        @pl.when(jnp.logical_not(first))
        def _():
            for gi in range(G):
                gpw_ref[gi] += g_pw[gi]

    return _tied_call(
        body, deps, name="pool_bwd", grid=(nt,),
        out_shape=[jax.ShapeDtypeStruct((T, D), F32), jax.ShapeDtypeStruct((G, Dg, Dg), F32),
                   jax.ShapeDtypeStruct((1, D), F32)],
        in_specs=[_rows(tm, D), pl.BlockSpec((H, D), lambda i: (jnp.minimum((i + 1) * r, last), 0)),
                  _rows(tm, D), pl.BlockSpec((H, D), lambda i: (jnp.maximum(i * r - 1, 0), 0)),
                  _full(pw.shape), _full((1, D))],
        out_specs=[_rows(tm, D), _full((G, Dg, Dg)), _full((1, D))],
        scratch_shapes=[pltpu.VMEM((n_ext, D), F32), pltpu.VMEM((n_ext, D), F32)],
        compiler_params=_params("arbitrary"),
    )(dz, dz, x, x, pw, scale)


def _adamw(name, recv, w, m, v, tm, layer=None, prev=None):
    R, C = w.shape[-2:]
    c1 = 1.0 - ADAM_B1 ** ADAM_STEP
    c2 = 1.0 - ADAM_B2 ** ADAM_STEP
    if layer is None:
        spec = _rows(tm, C)
    else:
        spec = pl.BlockSpec((None, tm, C), lambda i: (layer, i, 0))
    prev = list(prev) if prev is not None else []

    def body(r_ref, w_ref, m_ref, v_ref, *rest):
        g_ref, d_ref, nm_ref, nv_ref = rest[len(prev):]
        g = r_ref[0].astype(F32)
        for s in range(1, N_DEV):
            g = g + r_ref[s].astype(F32)
        m1 = ADAM_B1 * m_ref[...] + (1.0 - ADAM_B1) * g
        v1 = ADAM_B2 * v_ref[...] + (1.0 - ADAM_B2) * (g * g)
        m_hat = m1 / c1
        v_hat = v1 / c2
        g_ref[...] = g
        d_ref[...] = -ADAM_LR * (m_hat / (jnp.sqrt(v_hat) + ADAM_EPS) + ADAM_WD * w_ref[...])
        nm_ref[...] = m1
        nv_ref[...] = v1

    return pl.pallas_call(
        body, name=name, grid=(R // tm,),
        out_shape=[jax.ShapeDtypeStruct(w.shape, F32)] * 4,
        in_specs=[pl.BlockSpec((N_DEV, tm, C), lambda i: (0, i, 0))] + [spec] * 3
        + [pl.BlockSpec(memory_space=pl.ANY)] * len(prev),
        out_specs=[spec] * 4,
        input_output_aliases={4 + j: j for j in range(len(prev))},
        compiler_params=_params("parallel"),
    )(recv, w, m, v, *prev)


def _adamw_small(name, recv, w, m, v, pieces):
    R, C = w.shape
    c1 = 1.0 - ADAM_B1 ** ADAM_STEP
    c2 = 1.0 - ADAM_B2 ** ADAM_STEP
    n = len(pieces)

    def body(r_ref, w_ref, m_ref, v_ref, *rest):
        outs, packed = rest[:4 * n], rest[4 * n]
        g = r_ref[0]
        for s in range(1, N_DEV):
            g = g + r_ref[s]
        m1 = ADAM_B1 * m_ref[...] + (1.0 - ADAM_B1) * g
        v1 = ADAM_B2 * v_ref[...] + (1.0 - ADAM_B2) * (g * g)
        packed[0] = g
        packed[1] = -ADAM_LR * ((m1 / c1) / (jnp.sqrt(v1 / c2) + ADAM_EPS) + ADAM_WD * w_ref[...])
        packed[2] = m1
        packed[3] = v1
        for kind in range(4):
            for p, (r0, rows, shape) in enumerate(pieces):
                o_ref = outs[kind * n + p]
                if shape[-1] == C:
                    o_ref[...] = packed[kind, r0:r0 + rows, :].reshape(shape)
                else:
                    per = shape[-1] // C
                    for idx in range(rows):
                        l, q = divmod(idx, per)
                        o_ref[l:l + 1, q * C:(q + 1) * C] = packed[kind, r0 + idx:r0 + idx + 1, :]

    outs = pl.pallas_call(
        body, name=name,
        out_shape=[jax.ShapeDtypeStruct(shape, F32) for _ in range(4) for _, _, shape in pieces],
        scratch_shapes=[pltpu.VMEM((4, R, C), F32)],
    )(recv, w, m, v)
    return [outs[k * n:(k + 1) * n] for k in range(4)]


def _pad_rows(a, rows):
    return jnp.pad(a, ((0, rows - a.shape[0]), (0, 0)))


def kernel(x, pool_w, pool_scale, conv_w_in, conv_b_in, conv_dw, conv_dw_b, conv_ln_g, conv_ln_b, conv_w_out, conv_b_out, mix_ln_g, mix_ln_b, mlp_w1, mlp_b1, mlp_w2, mlp_b2, mlp_ln_g, mlp_ln_b, loss_target, m_pool_w, m_pool_scale, m_conv_w_in, m_conv_b_in, m_conv_dw, m_conv_dw_b, m_conv_ln_g, m_conv_ln_b, m_conv_w_out, m_conv_b_out, m_mix_ln_g, m_mix_ln_b, m_mlp_w1, m_mlp_b1, m_mlp_w2, m_mlp_b2, m_mlp_ln_g, m_mlp_ln_b, v_pool_w, v_pool_scale, v_conv_w_in, v_conv_b_in, v_conv_dw, v_conv_dw_b, v_conv_ln_g, v_conv_ln_b, v_conv_w_out, v_conv_b_out, v_mix_ln_g, v_mix_ln_b, v_mlp_w1, v_mlp_b1, v_mlp_w2, v_mlp_b2, v_mlp_ln_g, v_mlp_ln_b):
    _, T, D = x.shape
    L = mlp_w1.shape[0]
    assert L == 2 and pool_w.shape[0] == 1 and conv_w_in.shape[0] == 1
    G = pool_w.shape[1]
    Dg = D // G
    Fd = mlp_b1.shape[1]
    Fs = Fd // N_DEV
    Kc = conv_dw.shape[1]
    Dc = D // N_DEV
    alpha = float((2.0 * L) ** 0.25)
    x2d, tgt = x[0], loss_target[0]

    tm = _tile(T, 512)
    tm_wide = _tile(T, 512)
    tm_conv = _tile(T, 256)
    tm_wg = _tile(T, 2048)

    def pack_sh(dw, dwb, lg, lb, bo, bi):
        rows = jnp.concatenate([dw[0], dwb, lg, lb, bo, bi.reshape(2, Dc)], axis=0)
        return _pad_rows(rows, SH_ROWS)

    SH_ROWS = -(-(Kc + 6) // 8) * 8
    def pack_rep(ps, mg, mb, b1, b2, lg, lb):
        rows = jnp.concatenate([ps, mg, mb, b1.reshape(L * Fd // D, D), b2, lg, lb], axis=0)
        return _pad_rows(rows, REP_ROWS)

    n_rep = 1 + 2 * L + L * Fd // D + 3 * L
    REP_ROWS = -(-n_rep // 8) * 8

    w_sh = pack_sh(conv_dw, conv_dw_b, conv_ln_g, conv_ln_b, conv_b_out, conv_b_in)
    m_sh = pack_sh(m_conv_dw, m_conv_dw_b, m_conv_ln_g, m_conv_ln_b, m_conv_b_out, m_conv_b_in)
    v_sh = pack_sh(v_conv_dw, v_conv_dw_b, v_conv_ln_g, v_conv_ln_b, v_conv_b_out, v_conv_b_in)
    w_rep = pack_rep(pool_scale, mix_ln_g, mix_ln_b, mlp_b1, mlp_b2, mlp_ln_g, mlp_ln_b)
    m_rep = pack_rep(m_pool_scale, m_mix_ln_g, m_mix_ln_b, m_mlp_b1, m_mlp_b2, m_mlp_ln_g, m_mlp_ln_b)
    v_rep = pack_rep(v_pool_scale, v_mix_ln_g, v_mix_ln_b, v_mlp_b1, v_mlp_b2, v_mlp_ln_g, v_mlp_ln_b)

    groups = [[pool_w[0]], [mlp_w1[0]], [mlp_w2[0]], [conv_w_in[0], conv_w_out[0]], [mlp_w1[1]], [mlp_w2[1]]]
    handles, tokens = [], []
    for i, grp in enumerate(groups):
        tie = tokens[-1][0, 0] if tokens else 0.0
        srcs = [(a + tie).astype(BF16) for a in grp] + ([w_sh] if i == 0 else [])
        h, tk = _xstart("gather_%d" % i, srcs, [False] * len(srcs))
        handles.append(h)
        tokens.append(tk)
    pw_all, sh_all = _xwait(handles[0], tokens[-1])
    pw = pw_all.transpose(1, 0, 2, 3).reshape(G, Dg, Dg)
    dw_full = sh_all[:, 0:Kc].transpose(1, 0, 2).reshape(Kc, D)

    def sh_row(i):
        return sh_all[:, i].reshape(1, D)

    dwb_full, cg_full, cb_full, bout_full = (sh_row(Kc + i) for i in range(4))
    bin_full = sh_all[:, Kc + 4:Kc + 6].reshape(1, 2 * D)

    h0, z_m0, h0b = _pool_fwd(x2d, pw, pool_scale, mix_ln_g[0:1], mix_ln_b[0:1], alpha, tm, tuple(tokens))
    (w1b0,) = _xwait(handles[1], h0)
    u0 = _mlp_up("mlp_up0", h0b, w1b0, mlp_b1[0:1], tm)
    (w2b0,) = _xwait(handles[2], u0)
    x1, z_f0, x1b = _proj_ln("mlp_down0", u0, w2b0.reshape(Fd, D), mlp_b2[0:1], h0, mlp_ln_g[0:1], mlp_ln_b[0:1],
                        alpha, tm, act=True)
    win_b, wout_all = _xwait(handles[3], x1)
    w_out = wout_all.reshape(D, D)
    p, glu = _conv_in(x1b, win_b, bin_full, tm)
    cz, s = _dwconv_fwd(glu, dw_full, dwb_full, cg_full, cb_full, tm_conv)
    h1, z_m1, h1b = _proj_ln("conv_out", s, w_out, bout_full, x1, mix_ln_g[1:2], mix_ln_b[1:2], alpha, tm)
    (w1b1,) = _xwait(handles[4], h1)
    u1 = _mlp_up("mlp_up1", h1b, w1b1, mlp_b1[1:2], tm)
    (w2b1,) = _xwait(handles[5], u1)

    dz, dzb, g_fg1, g_fb1, g_b2_1, loss_cols = _proj_ln_loss(
        "mlp_down1", u1, w2b1.reshape(Fd, D), mlp_b2[1:2], h1, mlp_ln_g[1:2], mlp_ln_b[1:2], tgt, alpha, tm)
    gw2_1 = _wgrad("gw2_1", u1, dzb, False, 4, 1, tm_wg, act=True)
    e_w2_1, tk = _xstart("grads_w2_1", [gw2_1], [True])
    du, g_b1_1 = _mlp_act_bwd("mlp_act_bwd1", dzb, u1, w2b1, tm, (tk,))
    dz, dzb, g_mg1, g_mb1, g_bout = _dx_proj("mlp_in_bwd1", dz, du, w1b1, z_m1, mix_ln_g[1:2], alpha, tm_wide)
    gw1_1 = _wgrad("gw1_1", h1b, du, True, 1, 2, tm_wg)
    e_w1_1, tk = _xstart("grads_w1_1", [gw1_1], [True])
    dc, g_cg, g_cb, g_dwb, gwout = _conv_out_bwd(dzb, w_out, cz, s, cg_full, cb_full, tm, (tk,))
    dp, g_dw, g_bin = _dwconv_bwd(dc, glu, p, dw_full, tm_conv)
    gwin = _wgrad("gw_in", x1b, dp, True, 1, 2, tm_wg)
    e_conv, tk = _xstart("grads_conv", [gwin, gwout], [True, True])
    dz, dzb, g_fg0, g_fb0, g_b2_0 = _dx_proj("conv_in_bwd", dz, dp, win_b, z_f0, mlp_ln_g[0:1], alpha, tm, (tk,))
    gw2_0 = _wgrad("gw2_0", u0, dzb, False, 4, 1, tm_wg, act=True)
    e_w2_0, tk = _xstart("grads_w2_0", [gw2_0], [True])
    du, g_b1_0 = _mlp_act_bwd("mlp_act_bwd0", dzb, u0, w2b0, tm, (tk,))
    dz, dzb, g_mg0, g_mb0, _ = _dx_proj("mlp_in_bwd0", dz, du, w1b0, z_m0, mix_ln_g[0:1], alpha, tm_wide)
    gw1_0 = _wgrad("gw1_0", h0b, du, True, 1, 2, tm_wg)
    e_w1_0, tk = _xstart("grads_w1_0", [gw1_0], [True])
    grad_x, g_pw, g_ps = _pool_bwd(dz, x2d, pw, pool_scale, alpha, tm, (tk,))

    loss = lax.psum(0.5 / D * jnp.sum(loss_cols), MESH_AXES)

    gpw_b = g_pw.reshape(G, N_DEV, Dg // N_DEV, Dg).transpose(1, 0, 2, 3).astype(BF16)

    def to_dev(vec, rows):
        return vec.reshape(rows, N_DEV, Dc).transpose(1, 0, 2)

    g_sh = jnp.concatenate(
        [to_dev(g_dw[0:Kc], Kc), to_dev(g_dwb, 1), to_dev(g_cg, 1), to_dev(g_cb, 1), to_dev(g_bout, 1),
         g_bin.reshape(N_DEV, 2, Dc), jnp.zeros((N_DEV, SH_ROWS - Kc - 6, Dc), F32)], axis=1)
    g_rep = _pad_rows(jnp.concatenate(
        [g_ps, g_mg0, g_mg1, g_mb0, g_mb1, g_b1_0.reshape(Fd // D, D), g_b1_1.reshape(Fd // D, D),
         g_b2_0, g_b2_1, g_fg0, g_fg1, g_fb0, g_fb1], axis=0), REP_ROWS)

    e_small, tk = _xstart("grads_small", [gpw_b, g_sh, g_rep], [True, True, False])

    def upd(name, recv, w, m, v):
        shape = w.shape
        C = shape[-1]
        R = w.size // C
        outs = _adamw(name, recv.reshape(N_DEV, R, C), w.reshape(R, C), m.reshape(R, C), v.reshape(R, C), _tile(R, 256))
        return [o.reshape(shape) for o in outs]

    def upd_layer(name, recv, w, m, v, layer, prev):
        return _adamw(name, recv, w, m, v, _tile(w.shape[1], 256), layer, prev)

    (r_w2_1,) = _xwait(e_w2_1, tk)
    o_w2 = upd_layer("adam_w2_1", r_w2_1, mlp_w2, m_mlp_w2, v_mlp_w2, 1, None)
    (r_w1_1,) = _xwait(e_w1_1, o_w2[0])
    o_w1 = upd_layer("adam_w1_1", r_w1_1, mlp_w1, m_mlp_w1, v_mlp_w1, 1, None)
    r_win, r_wout = _xwait(e_conv, o_w1[0])
    o_win = upd("adam_w_in", r_win, conv_w_in, m_conv_w_in, v_conv_w_in)
    o_wout = upd("adam_w_out", r_wout, conv_w_out, m_conv_w_out, v_conv_w_out)
    (r_w2_0,) = _xwait(e_w2_0, o_wout[0])
    o_w2 = upd_layer("adam_w2_0", r_w2_0, mlp_w2, m_mlp_w2, v_mlp_w2, 0, o_w2)
    (r_w1_0,) = _xwait(e_w1_0, o_w2[0])
    o_w1 = upd_layer("adam_w1_0", r_w1_0, mlp_w1, m_mlp_w1, v_mlp_w1, 0, o_w1)
    r_pw, r_sh, r_rep = _xwait(e_small, o_w1[0])
    o_pw = upd("adam_pool_w", r_pw, pool_w, m_pool_w, v_pool_w)
    sh_pieces = [(0, Kc, (1, Kc, Dc))] + [(Kc + i, 1, (1, Dc)) for i in range(4)] + [(Kc + 4, 2, (1, 2 * Dc))]
    rep_pieces, o = [], 0
    for rows, shape in ((1, (1, D)), (L, (L, D)), (L, (L, D)), (L * Fd // D, (L, Fd)), (L, (L, D)), (L, (L, D)),
                        (L, (L, D))):
        rep_pieces.append((o, rows, shape))
        o += rows
    o_sh = _adamw_small("adam_conv_vec", r_sh, w_sh, m_sh, v_sh, sh_pieces)
    o_rep = _adamw_small("adam_replicated", r_rep, w_rep, m_rep, v_rep, rep_pieces)

    results = []
    for kind in range(4):
        dwv, dwb, lg, lb, bo, bi = o_sh[kind]
        ps, mg, mb, b1, b2, fg, fb = o_rep[kind]
        results.append([o_pw[kind], ps, o_win[kind], bi, dwv, dwb, lg, lb, o_wout[kind], bo, mg, mb,
                        o_w1[kind], b1, o_w2[kind], b2, fg, fb])
    return (loss, grad_x[None], *results[0], *results[1], *results[2], *results[3])
```
